```python
import math, functools
import jax, jax.numpy as jnp
from jax import lax
import numpy as np

D_MODEL = 1024
BATCH = 8
SEQ = 2048
DEPTH = 1
DEC_BATCH = 32
DEC_SEQ = 8
PAST_LEN = 16384
PAGE_SIZE = 128

ATT_HEADS = 4
ATT_HEAD_DIM = 64
ATT_V_DIM = 2 * ATT_HEAD_DIM
ATT_WIDTH = ATT_HEADS * ATT_V_DIM
SSD_HEADS = 8
SSD_HEAD_DIM = 64
SSD_WIDTH = SSD_HEADS * SSD_HEAD_DIM
SSD_GROUPS = 2
D_STATE = 128
CONV_W = 4
CONV_DIM = SSD_WIDTH + 2 * SSD_GROUPS * D_STATE
SSD_CHUNK = 128
MIX_WIDTH = ATT_WIDTH + SSD_WIDTH
IN_WIDTH = 3 * ATT_WIDTH + SSD_WIDTH + CONV_DIM + SSD_HEADS
SPLITS = [ATT_WIDTH, 2 * ATT_WIDTH, 3 * ATT_WIDTH, 3 * ATT_WIDTH + SSD_WIDTH,
          3 * ATT_WIDTH + SSD_WIDTH + CONV_DIM]
D_FF = 4 * D_MODEL
N_BUCKETS = 32
MAX_DISTANCE = 128
MAX_EXACT = N_BUCKETS // 2
Q_BLOCK = 128
EPS = 1e-6

kernel_name = "hymba_diffattn_ssd_decoder_step"

F32 = jnp.float32


def _rmsnorm(x, g):
    xf = x.astype(F32)
    y = xf * lax.rsqrt(jnp.mean(xf * xf, axis=-1, keepdims=True) + EPS)
    return (y * g.astype(F32)).astype(x.dtype)


def _rel_bucket(dist):
    n = jnp.maximum(dist, 0)
    nf = jnp.maximum(n, 1).astype(F32)
    large = MAX_EXACT + (jnp.log(nf / MAX_EXACT) / math.log(MAX_DISTANCE / MAX_EXACT)
                         * (N_BUCKETS - MAX_EXACT)).astype(jnp.int32)
    large = jnp.minimum(large, N_BUCKETS - 1)
    return jnp.where(n < MAX_EXACT, n, large)


def _rel_bias(rel_table, q_pos, k_pos):
    b = rel_table[_rel_bucket(q_pos[:, None] - k_pos[None, :])]
    return jnp.transpose(b, (2, 0, 1)).astype(F32)


def _modulation(c, w, b):
    m = (jax.nn.silu(c) @ w + b)[:, None, :]
    return jnp.split(m, 6, axis=-1)


def _diff_lambda(lq1, lk1, lq2, lk2, lam_init):
    e1 = jnp.exp(jnp.sum(lq1.astype(F32) * lk1.astype(F32)))
    e2 = jnp.exp(jnp.sum(lq2.astype(F32) * lk2.astype(F32)))
    return e1 - e2 + lam_init


def _diff_combine(o, lam, subln_g, lam_init):
    d = o[..., 0, :] - lam * o[..., 1, :]
    d = d * lax.rsqrt(jnp.mean(d * d, axis=-1, keepdims=True) + EPS) * subln_g.astype(F32)
    d = d * (1.0 - lam_init)
    return d.reshape(o.shape[0], o.shape[1], ATT_WIDTH)


def _diff_attn_prompt(q, k, v, rel_table):
    B, S = q.shape[:2]
    nb = S // Q_BLOCK
    scale = ATT_HEAD_DIM ** -0.5
    k_pos = jnp.arange(S)
    qb = jnp.moveaxis(q.reshape(B, nb, Q_BLOCK, ATT_HEADS, 2, ATT_HEAD_DIM), 1, 0)

    def one_block(args):
        q_blk, start = args
        q_pos = start + jnp.arange(Q_BLOCK)
        s = jnp.einsum('bqhjd,bkhjd->bhjqk', q_blk, k).astype(F32) * scale
        s = s + _rel_bias(rel_table, q_pos, k_pos)[:, None]
        s = jnp.where(k_pos[None, :] <= q_pos[:, None], s, -jnp.inf)
        p = jax.nn.softmax(s, axis=-1)
        return jnp.einsum('bhjqk,bkhe->bqhje', p.astype(v.dtype), v).astype(F32)

    o = lax.map(one_block, (qb, jnp.arange(nb) * Q_BLOCK))
    return jnp.moveaxis(o, 0, 1).reshape(B, S, ATT_HEADS, 2, ATT_V_DIM)


def _online_step(carry, s, v):
    m, l, acc = carry
    m_new = jnp.maximum(m, jnp.max(s, axis=-1))
    alpha = jnp.exp(m - m_new)
    p = jnp.exp(s - m_new[..., None])
    l = l * alpha + jnp.sum(p, axis=-1)
    acc = acc * alpha[..., None] + jnp.einsum('bhjqk,bkhe->bhjqe', p, v.astype(F32))
    return (m_new, l, acc)


def _diff_attn_sample(q, k_new, v_new, cache_k, cache_v, layer, page_table, rel_table):
    Bd, T = q.shape[:2]
    n_pages = page_table.shape[1]
    past = n_pages * PAGE_SIZE
    q_pos = past + jnp.arange(T)
    scale = ATT_HEAD_DIM ** -0.5
    qf = q.astype(F32)
    init = (jnp.full((Bd, ATT_HEADS, 2, T), -jnp.inf, F32),
            jnp.zeros((Bd, ATT_HEADS, 2, T), F32),
            jnp.zeros((Bd, ATT_HEADS, 2, T, ATT_V_DIM), F32))

    def page_step(carry, xs):
        phys, p_idx = xs
        kp = cache_k[layer, phys]
        vp = cache_v[layer, phys]
        k_pos = p_idx * PAGE_SIZE + jnp.arange(PAGE_SIZE)
        s = jnp.einsum('bqhjd,bkhjd->bhjqk', qf, kp.astype(F32)) * scale
        s = s + _rel_bias(rel_table, q_pos, k_pos)[:, None]
        return _online_step(carry, s, vp), None

    carry, _ = lax.scan(page_step, init, (page_table.T, jnp.arange(n_pages)))
    s = jnp.einsum('bqhjd,bkhjd->bhjqk', qf, k_new.astype(F32)) * scale
    s = s + _rel_bias(rel_table, q_pos, q_pos)[:, None]
    s = jnp.where(q_pos[None, :] <= q_pos[:, None], s, -jnp.inf)
    m, l, acc = _online_step(carry, s, v_new)
    o = acc / l[..., None]
    return jnp.transpose(o, (0, 3, 1, 2, 4))


def _causal_conv(xbc, prefix, w, b):
    T = xbc.shape[1]
    xp = jnp.concatenate([prefix.astype(xbc.dtype), xbc], axis=1)
    y = b
    for i in range(CONV_W):
        y = y + xp[:, i:i + T] * w[i]
    return jax.nn.silu(y), xp[:, xp.shape[1] - (CONV_W - 1):]


def _segsum(a):
    T = a.shape[-1]
    xx = jnp.broadcast_to(a[..., None], a.shape + (T,))
    xx = jnp.where(jnp.tril(jnp.ones((T, T), bool), -1), xx, 0.0)
    ss = jnp.cumsum(xx, axis=-2)
    return jnp.where(jnp.tril(jnp.ones((T, T), bool)), ss, -jnp.inf)


def _ssd_scan(xdt, dA, Bm, Cm, h0, chunk):
    b, L = xdt.shape[:2]
    c = L // chunk
    X = xdt.reshape(b, c, chunk, SSD_HEADS, SSD_HEAD_DIM)
    Bc = Bm.reshape(b, c, chunk, SSD_HEADS, D_STATE)
    Cc = Cm.reshape(b, c, chunk, SSD_HEADS, D_STATE)
    A = jnp.transpose(dA.reshape(b, c, chunk, SSD_HEADS), (0, 3, 1, 2))
    A_cs = jnp.cumsum(A, axis=-1)
    Lmat = jnp.exp(_segsum(A))
    scores = jnp.einsum('bclhn,bcshn->bhcls', Cc, Bc) * Lmat
    y_diag = jnp.einsum('bhcls,bcshp->bclhp', scores, X)
    decay_states = jnp.exp(A_cs[..., -1:] - A_cs)
    states = jnp.einsum('bclhn,bhcl,bclhp->bchpn', Bc, decay_states, X)
    states = jnp.concatenate([h0[:, None], states], axis=1)
    decay_chunk = jnp.exp(_segsum(jnp.pad(A_cs[..., -1], ((0, 0), (0, 0), (1, 0)))))
    new_states = jnp.einsum('bhzc,bchpn->bzhpn', decay_chunk, states)
    states, final = new_states[:, :-1], new_states[:, -1]
    y_off = jnp.einsum('bclhn,bchpn,bhcl->bclhp', Cc, states, jnp.exp(A_cs))
    return (y_diag + y_off).reshape(b, L, SSD_HEADS, SSD_HEAD_DIM), final


def _ssd_branch(xbc_act, z, dt_raw, h0, dt_bias, a_log, d_skip, norm_g, chunk):
    b, T = xbc_act.shape[:2]
    xs, Bm, Cm = jnp.split(xbc_act, [SSD_WIDTH, SSD_WIDTH + SSD_GROUPS * D_STATE], axis=-1)
    xs = xs.reshape(b, T, SSD_HEADS, SSD_HEAD_DIM).astype(F32)
    rep = SSD_HEADS // SSD_GROUPS
    Bm = jnp.repeat(Bm.reshape(b, T, SSD_GROUPS, D_STATE).astype(F32), rep, axis=2)
    Cm = jnp.repeat(Cm.reshape(b, T, SSD_GROUPS, D_STATE).astype(F32), rep, axis=2)
    dt = jax.nn.softplus(dt_raw.astype(F32) + dt_bias.astype(F32))
    A = -jnp.exp(a_log.astype(F32))
    y, h_T = _ssd_scan(xs * dt[..., None], dt * A, Bm, Cm, h0.astype(F32), chunk)
    y = y + d_skip.astype(F32)[:, None] * xs
    y = y.reshape(b, T, SSD_WIDTH) * jax.nn.silu(z.astype(F32))
    yg = y.reshape(b, T, SSD_GROUPS, SSD_WIDTH // SSD_GROUPS)
    yg = yg * lax.rsqrt(jnp.mean(yg * yg, axis=-1, keepdims=True) + EPS)
    return yg.reshape(b, T, SSD_WIDTH) * norm_g.astype(F32), h_T


def _layer(x, mods, lw, attend, conv_prefix, h0, chunk, lam_init):
    (g_mix, g_ffn, w_in, w_out, lq1, lk1, lq2, lk2, subln_g, conv_w, conv_b,
     dt_bias, a_log, d_skip, ssd_norm_g, w_up, w_down) = lw
    sh1, sc1, ga1, sh2, sc2, ga2 = mods
    B, T, _ = x.shape
    h = _rmsnorm(x, g_mix) * (1 + sc1) + sh1
    proj = h @ w_in
    q, k, v, z, xbc, dt_raw = jnp.split(proj, SPLITS, axis=-1)
    q = q.reshape(B, T, ATT_HEADS, 2, ATT_HEAD_DIM)
    k = k.reshape(B, T, ATT_HEADS, 2, ATT_HEAD_DIM)
    v = v.reshape(B, T, ATT_HEADS, ATT_V_DIM)
    o = attend(q, k, v)
    att = _diff_combine(o, _diff_lambda(lq1, lk1, lq2, lk2, lam_init), subln_g, lam_init)
    xbc_act, conv_state = _causal_conv(xbc, conv_prefix, conv_w, conv_b)
    ssd, h_T = _ssd_branch(xbc_act, z, dt_raw, h0, dt_bias, a_log, d_skip, ssd_norm_g, chunk)
    mix = jnp.concatenate([att, ssd], axis=-1).astype(x.dtype) @ w_out
    x = x + ga1 * mix
    h2 = _rmsnorm(x, g_ffn) * (1 + sc2) + sh2
    u = jax.nn.relu(h2 @ w_up)
    x = x + ga2 * ((u * u) @ w_down)
    return x, k, v, conv_state, h_T.astype(x.dtype)


def setup_inputs(seed: int = 0) -> dict:
    key = jax.random.key(seed)
    ks = jax.random.split(key, 40)
    n_pages = PAST_LEN // PAGE_SIZE
    n_used = DEC_BATCH * n_pages
    n_pool = (n_used * 5) // 4
    nrm = lambda k, shape, s: jax.random.normal(k, shape, F32) * s
    dt0 = jnp.exp(jax.random.uniform(ks[20], (DEPTH, SSD_HEADS), F32)
                  * (math.log(0.1) - math.log(0.001)) + math.log(0.001))
    return {
        "x_prompt": nrm(ks[0], (BATCH, SEQ, D_MODEL), 1.0),
        "x_sample": nrm(ks[1], (DEC_BATCH, DEC_SEQ, D_MODEL), 1.0),
        "cache_k": nrm(ks[2], (DEPTH, n_pool, PAGE_SIZE, ATT_HEADS, 2, ATT_HEAD_DIM), 1.0),
        "cache_v": nrm(ks[3], (DEPTH, n_pool, PAGE_SIZE, ATT_HEADS, ATT_V_DIM), 1.0),
        "state_conv": nrm(ks[4], (DEPTH, DEC_BATCH, CONV_W - 1, CONV_DIM), 1.0),
        "state_ssm": nrm(ks[5], (DEPTH, DEC_BATCH, SSD_HEADS, SSD_HEAD_DIM, D_STATE), 0.5),
        "page_table": jax.random.permutation(ks[6], n_pool)[:n_used].reshape(DEC_BATCH, n_pages).astype(jnp.int32),
        "c_prompt": nrm(ks[7], (BATCH, D_MODEL), 1.0),
        "c_sample": nrm(ks[8], (DEC_BATCH, D_MODEL), 1.0),
        "rel_bias": nrm(ks[9], (N_BUCKETS, ATT_HEADS), 0.5),
        "w_ada": nrm(ks[10], (DEPTH, D_MODEL, 6 * D_MODEL), 0.5 * D_MODEL ** -0.5),
        "b_ada": nrm(ks[11], (DEPTH, 6 * D_MODEL), 0.01),
        "g_mix": 1.0 + nrm(ks[12], (DEPTH, D_MODEL), 0.05),
        "g_ffn": 1.0 + nrm(ks[13], (DEPTH, D_MODEL), 0.05),
        "w_in": nrm(ks[14], (DEPTH, D_MODEL, IN_WIDTH), D_MODEL ** -0.5),
        "w_out": nrm(ks[15], (DEPTH, MIX_WIDTH, D_MODEL), MIX_WIDTH ** -0.5),
        "lam_q1": nrm(ks[16], (DEPTH, ATT_HEAD_DIM), 0.1),
        "lam_k1": nrm(ks[17], (DEPTH, ATT_HEAD_DIM), 0.1),
        "lam_q2": nrm(ks[18], (DEPTH, ATT_HEAD_DIM), 0.1),
        "lam_k2": nrm(ks[19], (DEPTH, ATT_HEAD_DIM), 0.1),
        "subln_g": 1.0 + nrm(ks[21], (DEPTH, ATT_V_DIM), 0.05),
        "conv_w": nrm(ks[22], (DEPTH, CONV_W, CONV_DIM), CONV_W ** -0.5),
        "conv_b": nrm(ks[23], (DEPTH, CONV_DIM), 0.01),
        "dt_bias": dt0 + jnp.log(-jnp.expm1(-dt0)),
        "a_log": jnp.log(jax.random.uniform(ks[24], (DEPTH, SSD_HEADS), F32, 1.0, 16.0)),
        "d_skip": 1.0 + nrm(ks[25], (DEPTH, SSD_HEADS), 0.1),
        "ssd_norm_g": 1.0 + nrm(ks[26], (DEPTH, SSD_WIDTH), 0.05),
        "w_up": nrm(ks[27], (DEPTH, D_MODEL, D_FF), D_MODEL ** -0.5),
        "w_down": nrm(ks[28], (DEPTH, D_FF, D_MODEL), D_FF ** -0.5),
        "g_final": 1.0 + nrm(ks[29], (D_MODEL,), 0.05),
    }


def reference(x_prompt, x_sample, cache_k, cache_v, state_conv, state_ssm, page_table,
              c_prompt, c_sample, rel_bias, w_ada, b_ada, g_mix, g_ffn, w_in, w_out,
              lam_q1, lam_k1, lam_q2, lam_k2, subln_g, conv_w, conv_b, dt_bias, a_log,
              d_skip, ssd_norm_g, w_up, w_down, g_final):
    yp, ys = x_prompt, x_sample
    kp_l, vp_l, cp_l, hp_l = [], [], [], []
    ks_l, vs_l, cs_l, hs_l = [], [], [], []
    for l in range(DEPTH):
        lw = (g_mix[l], g_ffn[l], w_in[l], w_out[l], lam_q1[l], lam_k1[l], lam_q2[l],
              lam_k2[l], subln_g[l], conv_w[l], conv_b[l], dt_bias[l], a_log[l],
              d_skip[l], ssd_norm_g[l], w_up[l], w_down[l])
        lam_init = 0.8 - 0.6 * math.exp(-0.3 * l)
        attend_p = functools.partial(_diff_attn_prompt, rel_table=rel_bias)
        prefix_p = jnp.zeros((yp.shape[0], CONV_W - 1, CONV_DIM), yp.dtype)
        h0_p = jnp.zeros((yp.shape[0], SSD_HEADS, SSD_HEAD_DIM, D_STATE), F32)
        yp, kp, vp, cp, hp = _layer(yp, _modulation(c_prompt, w_ada[l], b_ada[l]), lw,
                                    attend_p, prefix_p, h0_p, SSD_CHUNK, lam_init)
        attend_s = functools.partial(_diff_attn_sample, cache_k=cache_k, cache_v=cache_v,
                                     layer=l, page_table=page_table, rel_table=rel_bias)
        ys, k_s, v_s, c_s, h_s = _layer(ys, _modulation(c_sample, w_ada[l], b_ada[l]), lw,
                                        attend_s, state_conv[l], state_ssm[l], ys.shape[1], lam_init)
        kp_l.append(kp); vp_l.append(vp); cp_l.append(cp); hp_l.append(hp)
        ks_l.append(k_s); vs_l.append(v_s); cs_l.append(c_s); hs_l.append(h_s)
    y_prompt = _rmsnorm(yp, g_final)
    y_sample = _rmsnorm(ys, g_final)
    return (y_prompt, y_sample,
            jnp.stack(kp_l), jnp.stack(vp_l), jnp.stack(cp_l), jnp.stack(hp_l),
            jnp.stack(ks_l), jnp.stack(vs_l), jnp.stack(cs_l), jnp.stack(hs_l))
```

```python
import functools
import math

import numpy as np
import jax
import jax.numpy as jnp
from jax import lax
from jax.experimental import pallas as pl
from jax.experimental.pallas import tpu as pltpu

F32 = jnp.float32
BF16 = jnp.bfloat16

D_MODEL = 1024
PAGE_SIZE = 128
ATT_HEADS = 4
ATT_HEAD_DIM = 64
ATT_V_DIM = 2 * ATT_HEAD_DIM
ATT_WIDTH = ATT_HEADS * ATT_V_DIM
SSD_HEADS = 8
SSD_HEAD_DIM = 64
SSD_WIDTH = SSD_HEADS * SSD_HEAD_DIM
SSD_GROUPS = 2
D_STATE = 128
CONV_W = 4
CONV_DIM = SSD_WIDTH + 2 * SSD_GROUPS * D_STATE
SSD_CHUNK = 128
IN_WIDTH = 3 * ATT_WIDTH + SSD_WIDTH + CONV_DIM + SSD_HEADS
D_FF = 4 * D_MODEL
N_BUCKETS = 32
MAX_DISTANCE = 128
MAX_EXACT = N_BUCKETS // 2
EPS = 1e-6
LAM_INIT = 0.8 - 0.6 * math.exp(-0.3 * 0)

LANES = 128
SUBLANES = 8
IN_PAD = 3 * ATT_WIDTH + SSD_WIDTH + CONV_DIM + LANES
LOG2E = 1.4426950408889634
NEG = -1e30
VMEM_LIMIT = 56 * 1024 * 1024

ATT_T = 256
PAGES_PER_STEP = 8


def _bucket_lower_bounds():
  d = np.arange(0, 4 * MAX_DISTANCE)
  nf = np.maximum(d, 1).astype(np.float64)
  large = MAX_EXACT + (np.log(nf / MAX_EXACT) / math.log(MAX_DISTANCE / MAX_EXACT)
                       * (N_BUCKETS - MAX_EXACT)).astype(np.int64)
  large = np.minimum(large, N_BUCKETS - 1)
  bucket = np.where(d < MAX_EXACT, d, large)
  return [int(np.argmax(bucket >= b)) for b in range(N_BUCKETS)]


_LOWER = _bucket_lower_bounds()
FAR_DIST = _LOWER[N_BUCKETS - 1]
assert FAR_DIST <= PAGE_SIZE and FAR_DIST <= ATT_T


def _silu(x):
  return x * jax.nn.sigmoid(x)


def _dot(a, b):
  return jnp.dot(a, b, preferred_element_type=F32)


def _dot_nt(a, b):
  return lax.dot_general(a, b, (((1,), (1,)), ((), ())), preferred_element_type=F32)


def _split3(x):
  hi = x.astype(BF16)
  r1 = x - hi.astype(F32)
  mid = r1.astype(BF16)
  lo = (r1 - mid.astype(F32)).astype(BF16)
  return hi, mid, lo


def _mod_kernel(c_ref, w_ref, b_ref, o_ref):
  s = _silu(c_ref[...]).astype(BF16)
  o_ref[...] = _dot(s, w_ref[...].astype(BF16)) + b_ref[...]


def _modulation(c_all, w_ada, b_ada):
  n = c_all.shape[0]
  tn = D_MODEL
  return pl.pallas_call(
      _mod_kernel,
      grid=(6 * D_MODEL // tn,),
      in_specs=[pl.BlockSpec((n, D_MODEL), lambda j: (0, 0)),
                pl.BlockSpec((D_MODEL, tn), lambda j: (0, j)),
                pl.BlockSpec((1, tn), lambda j: (0, j))],
      out_specs=pl.BlockSpec((n, tn), lambda j: (0, j)),
      out_shape=jax.ShapeDtypeStruct((n, 6 * D_MODEL), F32),
      compiler_params=pltpu.CompilerParams(vmem_limit_bytes=VMEM_LIMIT),
      name="modulation",
  )(c_all, w_ada, b_ada.reshape(1, -1))


def _bias_kernel(tab_ref, near_ref, diag_ref, slast_ref, snew_ref):
  def bias_of(dist, h):
    val = jnp.full(dist.shape, tab_ref[0, h], F32)
    for b in range(1, N_BUCKETS):
      val = jnp.where(dist >= _LOWER[b], tab_ref[b, h], val)
    return (val - tab_ref[N_BUCKETS - 1, h]) * LOG2E

  t = ATT_T
  r = lax.broadcasted_iota(jnp.int32, (t, t), 0)
  c = lax.broadcasted_iota(jnp.int32, (t, t), 1)
  r8 = lax.broadcasted_iota(jnp.int32, (SUBLANES, LANES), 0)
  c8 = lax.broadcasted_iota(jnp.int32, (SUBLANES, LANES), 1)
  for h in range(ATT_HEADS):
    near_ref[h] = bias_of(t + r - c, h)
    d = r - c
    diag_ref[h] = jnp.where(d >= 0, bias_of(d, h), NEG)
    slast_ref[h] = bias_of(PAGE_SIZE + r8 - c8, h)
    dn = r8 - c8
    snew_ref[h] = jnp.where(dn >= 0, bias_of(dn, h), NEG)


def _bias_tiles(rel_bias):
  t = ATT_T
  return pl.pallas_call(
      _bias_kernel,
      in_specs=[pl.BlockSpec(memory_space=pltpu.SMEM)],
      out_shape=(jax.ShapeDtypeStruct((ATT_HEADS, t, t), F32),
                 jax.ShapeDtypeStruct((ATT_HEADS, t, t), F32),
                 jax.ShapeDtypeStruct((ATT_HEADS, SUBLANES, LANES), F32),
                 jax.ShapeDtypeStruct((ATT_HEADS, SUBLANES, LANES), F32)),
      compiler_params=pltpu.CompilerParams(vmem_limit_bytes=VMEM_LIMIT),
      name="bias_tiles",
  )(rel_bias)


def _inproj_kernel(x_ref, sc_ref, sh_ref, g_ref, w_ref, wkt_ref, *out_refs, prompt):
  x = x_ref[0]
  var = jnp.mean(x * x, axis=-1, keepdims=True)
  h = x * lax.rsqrt(var + EPS) * g_ref[...]
  h = h * (1.0 + sc_ref[0]) + sh_ref[0]
  hb = h.astype(BF16)
  aw = ATT_WIDTH
  q = _dot(hb, w_ref[:, 0:aw]) * (ATT_HEAD_DIM ** -0.5 * LOG2E)
  v = _dot(hb, w_ref[:, 2 * aw:3 * aw])
  if prompt:
    q_ref, ktb_ref, vb_ref, kt_ref, v_ref, z_ref, xbc_ref, dt_ref = out_refs
    kt = _dot_nt(wkt_ref[...], hb)
    kt_ref[0] = kt
    t = ATT_T
    for hd in range(ATT_HEADS):
      sl = slice(hd * ATT_V_DIM, (hd + 1) * ATT_V_DIM)
      vb_ref[0, hd] = v[:, sl].astype(BF16)
      for cc in range(kt.shape[1] // t):
        ktb_ref[0, hd, cc] = kt[sl, cc * t:(cc + 1) * t].astype(BF16)
  else:
    q_ref, k_ref, v_ref, z_ref, xbc_ref, dt_ref = out_refs
    k_ref[0] = _dot(hb, w_ref[:, aw:2 * aw])
  v_ref[0] = v
  for hd in range(ATT_HEADS):
    sl = slice(hd * ATT_V_DIM, (hd + 1) * ATT_V_DIM)
    q_ref[0, hd] = q[:, sl].astype(BF16)
  o = 3 * aw
  z_ref[0] = _dot(hb, w_ref[:, o:o + SSD_WIDTH])
  o += SSD_WIDTH
  xbc_ref[0] = _dot(hb, w_ref[:, o:o + CONV_DIM])
  o += CONV_DIM
  dt_ref[0] = _dot(hb, w_ref[:, o:o + LANES])


def _in_proj(x3, sc, sh, g_mix, w_in_p, w_kt, tm, prompt):
  nb, tb, d = x3.shape
  nt = tb // tm
  per_row = sc.shape[1] != 1
  if per_row:
    mspec = pl.BlockSpec((1, tm, d), lambda b, i: (b, i, 0))
  else:
    mspec = pl.BlockSpec((1, 1, d), lambda b, i: (b, 0, 0))
  hm = pl.BlockSpec((1, ATT_HEADS, tm, ATT_V_DIM), lambda b, i: (b, 0, i, 0))
  tok = lambda w: pl.BlockSpec((1, tm, w), lambda b, i: (b, i, 0))
  hm_shape = jax.ShapeDtypeStruct((nb, ATT_HEADS, tb, ATT_V_DIM), BF16)
  tok_shape = lambda w: jax.ShapeDtypeStruct((nb, tb, w), F32)
  tail_specs = [tok(ATT_WIDTH), tok(SSD_WIDTH), tok(CONV_DIM), tok(LANES)]
  tail_shapes = [tok_shape(ATT_WIDTH), tok_shape(SSD_WIDTH), tok_shape(CONV_DIM), tok_shape(LANES)]
  if prompt:
    t = ATT_T
    out_specs = [hm,
                 pl.BlockSpec((1, ATT_HEADS, tm // t, ATT_V_DIM, t), lambda b, i: (b, 0, i, 0, 0)),
                 hm,
                 pl.BlockSpec((1, ATT_WIDTH, tm), lambda b, i: (b, 0, i))] + tail_specs
    out_shape = [hm_shape,
                 jax.ShapeDtypeStruct((nb, ATT_HEADS, tb // t, ATT_V_DIM, t), BF16),
                 hm_shape,
                 jax.ShapeDtypeStruct((nb, ATT_WIDTH, tb), F32)] + tail_shapes
  else:
    out_specs = [hm, tok(ATT_WIDTH)] + tail_specs
    out_shape = [hm_shape, tok_shape(ATT_WIDTH)] + tail_shapes
  return pl.pallas_call(
      functools.partial(_inproj_kernel, prompt=prompt),
      grid=(nb, nt),
      in_specs=[tok(d), mspec, mspec,
                pl.BlockSpec((1, d), lambda b, i: (0, 0)),
                pl.BlockSpec((d, IN_PAD), lambda b, i: (0, 0)),
                pl.BlockSpec((ATT_WIDTH, d), lambda b, i: (0, 0))],
      out_specs=out_specs,
      out_shape=out_shape,
      compiler_params=pltpu.CompilerParams(
          dimension_semantics=("arbitrary", "arbitrary"), vmem_limit_bytes=VMEM_LIMIT),
      name="in_proj",
  )(x3, sc, sh, g_mix.reshape(1, d), w_in_p, w_kt)


def _diff_lambda(lam_ref):
  lv = lam_ref[...]
  s1 = jnp.sum(lv[0:1] * lv[1:2], axis=-1, keepdims=True)
  s2 = jnp.sum(lv[2:3] * lv[3:4], axis=-1, keepdims=True)
  return jnp.exp(s1) - jnp.exp(s2) + LAM_INIT


def _diff_combine(o0, o1, lam, g):
  d = o0 - lam * o1
  d = d * lax.rsqrt(jnp.mean(d * d, axis=-1, keepdims=True) + EPS) * g
  return d * (1.0 - LAM_INIT)


def _attn_kernel(q_ref, k_ref, v_ref, near_ref, diag_ref, lam_ref, g_ref, o_ref,
                 m_sc, acc_sc):
  t = ATT_T
  qi = pl.program_id(2)
  q = q_ref[0, 0]
  lane = lax.broadcasted_iota(jnp.int32, (t, ATT_V_DIM), 1)
  zero = jnp.zeros_like(q)
  q2 = jnp.concatenate([jnp.where(lane < ATT_HEAD_DIM, q, zero),
                        jnp.where(lane >= ATT_HEAD_DIM, q, zero)], axis=0)
  m_sc[...] = jnp.full(m_sc.shape, NEG, F32)
  acc_sc[...] = jnp.zeros(acc_sc.shape, F32)
  ones = jnp.ones((t, LANES), BF16)

  def step(c, bias):
    start = pl.multiple_of(c * t, t)
    kc = k_ref[0, 0, c]
    vc = v_ref[0, 0, pl.ds(start, t), :]
    s = _dot(q2, kc)
    if bias is not None:
      s = s + jnp.concatenate([bias, bias], axis=0)
    m_prev = m_sc[...]
    m_new = jnp.maximum(m_prev, jnp.max(s, axis=-1, keepdims=True))
    alpha = jnp.exp2(m_prev - m_new)
    p = jnp.exp2(s - jnp.concatenate([m_new] * (t // LANES), axis=1))
    pv = _dot(p.astype(BF16), jnp.concatenate([vc, ones], axis=1))
    acc_sc[...] = acc_sc[...] * jnp.concatenate([alpha, alpha], axis=1) + pv
    m_sc[...] = m_new

  def far_body(c, carry):
    step(c, None)
    return carry

  lax.fori_loop(0, jnp.maximum(qi - 1, 0), far_body, 0)

  @pl.when(qi >= 1)
  def _():
    step(qi - 1, near_ref[0])

  step(qi, diag_ref[0])

  acc = acc_sc[...]
  o = acc[:, :ATT_V_DIM] / acc[:, ATT_V_DIM:]
  d = _diff_combine(o[:t], o[t:], _diff_lambda(lam_ref), g_ref[...])
  o_ref[0] = d.astype(o_ref.dtype)


def _prompt_attention(q_hm, kt_hm, v_hm, near, diag, lam4, subln_g):
  b, nh, s, e = q_hm.shape
  t = ATT_T
  nq = s // t
  return pl.pallas_call(
      _attn_kernel,
      grid=(b, nh, nq),
      in_specs=[pl.BlockSpec((1, 1, t, e), lambda bi, h, qi: (bi, h, qi, 0)),
                pl.BlockSpec((1, 1, nq, e, t), lambda bi, h, qi: (bi, h, 0, 0, 0)),
                pl.BlockSpec((1, 1, s, e), lambda bi, h, qi: (bi, h, 0, 0)),
                pl.BlockSpec((1, t, t), lambda bi, h, qi: (h, 0, 0)),
                pl.BlockSpec((1, t, t), lambda bi, h, qi: (h, 0, 0)),
                pl.BlockSpec((4, ATT_HEAD_DIM), lambda bi, h, qi: (0, 0)),
                pl.BlockSpec((1, e), lambda bi, h, qi: (0, 0))],
      out_specs=pl.BlockSpec((1, t, e), lambda bi, h, qi: (bi, qi, h)),
      out_shape=jax.ShapeDtypeStruct((b, s, nh * e), BF16),
      scratch_shapes=[pltpu.VMEM((2 * t, LANES), F32),
                      pltpu.VMEM((2 * t, 2 * LANES), F32)],
      compiler_params=pltpu.CompilerParams(
          dimension_semantics=("arbitrary", "arbitrary", "arbitrary"),
          vmem_limit_bytes=VMEM_LIMIT),
      name="prompt_attention",
  )(q_hm, kt_hm, v_hm, near, diag, lam4, subln_g)


def _sample_attn_kernel(pt_ref, qb_ref, kn_ref, vn_ref, blast_ref, bnew_ref, lam_ref, g_ref,
                        ck_hbm, cv_hbm, o_ref,
                        kbuf, vbuf, sem, m_sc, l_sc, acc_sc, kpad, vpad, *, n_seq, n_grp):
  g_pages = PAGES_PER_STEP
  b = pl.program_id(0)
  g = pl.program_id(1)
  step_idx = b * n_grp + g
  slot = step_idx % 2

  def page_copies(bb, gg, sl):
    copies = []
    for i in range(g_pages):
      pg = pt_ref[bb, gg * g_pages + i]
      copies.append(pltpu.make_async_copy(ck_hbm.at[pg], kbuf.at[sl, i], sem.at[0, sl]))
      copies.append(pltpu.make_async_copy(cv_hbm.at[pg], vbuf.at[sl, i], sem.at[1, sl]))
    return copies

  @pl.when(step_idx == 0)
  def _():
    kpad[...] = jnp.zeros(kpad.shape, kpad.dtype)
    vpad[...] = jnp.zeros(vpad.shape, vpad.dtype)
    for cp in page_copies(0, 0, 0):
      cp.start()

  @pl.when(step_idx + 1 < n_seq * n_grp)
  def _():
    last = g == n_grp - 1
    nb = jnp.where(last, b + 1, b)
    ng = jnp.where(last, 0, g + 1)
    for cp in page_copies(nb, ng, 1 - slot):
      cp.start()

  @pl.when(g == 0)
  def _():
    m_sc[...] = jnp.full(m_sc.shape, NEG, F32)
    l_sc[...] = jnp.zeros(l_sc.shape, F32)
    acc_sc[...] = jnp.zeros(acc_sc.shape, F32)

  for cp in page_copies(b, g, slot):
    cp.wait()

  qb = qb_ref[0]

  rows_per_head = 2 * SUBLANES

  def update(s, values_of_head):
    m_prev = m_sc[...]
    m_new = jnp.maximum(m_prev, jnp.max(s, axis=-1, keepdims=True))
    alpha = jnp.exp2(m_prev - m_new)
    p = jnp.exp2(s - m_new[:, 0:1])
    l_sc[...] = l_sc[...] * alpha + jnp.sum(p, axis=-1, keepdims=True)
    pb = p.astype(BF16)
    pv = [_dot(pb[h * rows_per_head:(h + 1) * rows_per_head, :], values_of_head(h))
          for h in range(ATT_HEADS)]
    acc_sc[...] = acc_sc[...] * alpha + jnp.concatenate(pv, axis=0)
    m_sc[...] = m_new

  is_last = g == n_grp - 1

  def cache_step(sl):
    s = [_dot(qb, kbuf[sl, i].astype(BF16)) for i in range(g_pages)]
    s[-1] = s[-1] + jnp.where(is_last, blast_ref[...], 0.0)

    def values_of_head(h):
      v = [vbuf[sl, i, pl.ds(h, PAGE_SIZE, stride=ATT_HEADS), :] for i in range(g_pages)]
      return jnp.concatenate(v, axis=0).astype(BF16)

    update(jnp.concatenate(s, axis=1), values_of_head)

  for sl in range(2):
    pl.when(slot == sl)(functools.partial(cache_step, sl))

  @pl.when(is_last)
  def _():
    t_new = kn_ref.shape[1]
    kpad[0:t_new, :] = kn_ref[0].astype(BF16)
    vpad[0:t_new, :] = vn_ref[0].astype(BF16)
    sn = _dot_nt(qb, kpad[...]) + bnew_ref[...]
    update(sn, lambda h: vpad[:, h * ATT_V_DIM:(h + 1) * ATT_V_DIM])
    o = acc_sc[...] / l_sc[...]
    lam = _diff_lambda(lam_ref)
    for h in range(ATT_HEADS):
      o0 = o[(2 * h) * SUBLANES:(2 * h + 1) * SUBLANES, :]
      o1 = o[(2 * h + 1) * SUBLANES:(2 * h + 2) * SUBLANES, :]
      o_ref[0, :, h * ATT_V_DIM:(h + 1) * ATT_V_DIM] = _diff_combine(
          o0, o1, lam, g_ref[...]).astype(o_ref.dtype)


def _sample_attention(page_table, qb, k_new, v_new, blast, bnew, lam4, subln_g, cache_k, cache_v):
  n_seq, n_pages = page_table.shape
  t_new = k_new.shape[1]
  assert t_new == SUBLANES and n_pages % PAGES_PER_STEP == 0
  n_grp = n_pages // PAGES_PER_STEP
  w = ATT_WIDTH
  assert cache_k.shape[1:] == (w, PAGE_SIZE) and cache_v.shape[1:] == (PAGE_SIZE * ATT_HEADS, ATT_V_DIM)
  full = lambda shape: pl.BlockSpec(shape, lambda b, g, pt: (0,) * len(shape))
  grid_spec = pltpu.PrefetchScalarGridSpec(
      num_scalar_prefetch=1,
      grid=(n_seq, n_grp),
      in_specs=[pl.BlockSpec((1, 8 * SUBLANES, w), lambda b, g, pt: (b, 0, 0)),
                pl.BlockSpec((1, t_new, w), lambda b, g, pt: (b, 0, 0)),
                pl.BlockSpec((1, t_new, w), lambda b, g, pt: (b, 0, 0)),
                full((8 * SUBLANES, LANES)), full((8 * SUBLANES, LANES)),
                full((4, ATT_HEAD_DIM)), full((1, ATT_V_DIM)),
                pl.BlockSpec(memory_space=pl.ANY),
                pl.BlockSpec(memory_space=pl.ANY)],
      out_specs=pl.BlockSpec((1, t_new, w), lambda b, g, pt: (b, 0, 0)),
      scratch_shapes=[pltpu.VMEM((2, PAGES_PER_STEP) + cache_k.shape[1:], F32),
                      pltpu.VMEM((2, PAGES_PER_STEP) + cache_v.shape[1:], F32),
                      pltpu.SemaphoreType.DMA((2, 2)),
                      pltpu.VMEM((8 * SUBLANES, LANES), F32),
                      pltpu.VMEM((8 * SUBLANES, LANES), F32),
                      pltpu.VMEM((8 * SUBLANES, ATT_V_DIM), F32),
                      pltpu.VMEM((PAGE_SIZE, w), BF16),
                      pltpu.VMEM((PAGE_SIZE, w), BF16)])
  return pl.pallas_call(
      functools.partial(_sample_attn_kernel, n_seq=n_seq, n_grp=n_grp),
      grid_spec=grid_spec,
      out_shape=jax.ShapeDtypeStruct((n_seq, t_new, w), BF16),
      compiler_params=pltpu.CompilerParams(
          dimension_semantics=("arbitrary", "arbitrary"), vmem_limit_bytes=VMEM_LIMIT),
      name="sample_attention",
  )(page_table, qb, k_new, v_new, blast, bnew, lam4, subln_g, cache_k, cache_v)


def _ssd_kernel(xbc_ref, z_ref, dt_ref, pre_ref, h0_ref, cw_ref, cb_ref, dtb_ref, alog_ref,
                dsk_ref, ng_ref, e_ref, tri_ref,
                ssd_ref, conv_ref, h_ref, xp_sc, *, tv, nc):
  L = SSD_CHUNK
  hp = SUBLANES
  b = pl.program_id(0)
  c = pl.program_id(1)

  @pl.when((b == 0) & (c == 0))
  def _():
    xp_sc[...] = jnp.zeros(xp_sc.shape, F32)

  @pl.when(c == 0)
  def _():
    xp_sc[hp - (CONV_W - 1):hp, :] = pre_ref[0]
    h_ref[0] = h0_ref[0]

  xp_sc[hp:hp + tv, :] = xbc_ref[0]
  cw = cw_ref[...]
  y = cb_ref[...]
  for i in range(CONV_W):
    o = hp - (CONV_W - 1) + i
    y = y + xp_sc[o:o + L, :] * cw[i:i + 1]
  xa = _silu(y)

  @pl.when(c == nc - 1)
  def _():
    conv_ref[0] = xp_sc[hp + tv - (CONV_W - 1):hp + tv, :]

  xp_sc[0:hp, :] = xp_sc[tv:tv + hp, :]

  xs = xa[:, :SSD_WIDTH]
  gw = SSD_GROUPS * D_STATE
  bmat = xa[:, SSD_WIDTH:SSD_WIDTH + gw].astype(BF16)
  cmat = xa[:, SSD_WIDTH + gw:].astype(BF16)

  x = dt_ref[0] + dtb_ref[...]
  dtv = jnp.maximum(x, 0.0) + jnp.log1p(jnp.exp(-jnp.abs(x)))
  if tv < L:
    dtv = jnp.concatenate([dtv, jnp.zeros((L - tv, LANES), F32)], axis=0)
  a = dtv * (-jnp.exp(alog_ref[...]))

  tri = tri_ref[...]
  acs = sum(_dot(tri, part) for part in _split3(a))
  e = e_ref[...]
  aexp = sum(_dot(part, e) for part in _split3(acs))
  dtexp = _dot(dtv.astype(BF16), e)
  xd = xs * dtexp
  a_last = acs[L - 1:L, :]
  xdd_t = (xd * jnp.exp(aexp[L - 1:L, :] - aexp)).T
  acs_t = acs.T

  ri = lax.broadcasted_iota(jnp.int32, (L, L), 0)
  ci = lax.broadcasted_iota(jnp.int32, (L, L), 1)
  causal = ri >= ci
  lane = lax.broadcasted_iota(jnp.int32, (L, LANES), 1)
  heads_per_group = SSD_HEADS // SSD_GROUPS
  hprev = h_ref[0]
  hb = hprev.astype(BF16)

  ydiag, yoff, upd = [], [], []
  for g in range(SSD_GROUPS):
    cg = cmat[:, g * D_STATE:(g + 1) * D_STATE]
    bg = bmat[:, g * D_STATE:(g + 1) * D_STATE]
    scores = _dot_nt(cg, bg)
    for pair in range(heads_per_group // 2):
      h0 = g * heads_per_group + 2 * pair
      ms = []
      for h in (h0, h0 + 1):
        col = jnp.broadcast_to(acs[:, h:h + 1], (L, L))
        row = acs_t[h:h + 1, :]
        lm = jnp.exp(jnp.where(causal, col - row, NEG))
        ms.append((scores * lm).astype(BF16))
      xpair = xd[:, h0 * SSD_HEAD_DIM:(h0 + 2) * SSD_HEAD_DIM]
      xbd = jnp.concatenate([jnp.where(lane < SSD_HEAD_DIM, xpair, 0.0),
                             jnp.where(lane >= SSD_HEAD_DIM, xpair, 0.0)], axis=0)
      ydiag.append(_dot(jnp.concatenate(ms, axis=1), xbd.astype(BF16)))
    rows = slice(g * heads_per_group * SSD_HEAD_DIM, (g + 1) * heads_per_group * SSD_HEAD_DIM)
    yoff.append(_dot_nt(cg, hb[rows, :]))
    upd.append(_dot(xdd_t[rows, :].astype(BF16), bg))

  y = (jnp.concatenate(ydiag, axis=1) + jnp.concatenate(yoff, axis=1) * jnp.exp(aexp)
       + dsk_ref[...] * xs)

  decayed = []
  chunk_decay = jnp.exp(a_last)
  for h in range(SSD_HEADS):
    dec = jnp.broadcast_to(chunk_decay[:, h:h + 1], (SSD_HEAD_DIM, D_STATE))
    decayed.append(hprev[h * SSD_HEAD_DIM:(h + 1) * SSD_HEAD_DIM, :] * dec)
  h_ref[0] = jnp.concatenate(decayed, axis=0) + jnp.concatenate(upd, axis=0)

  yv = y[:tv] * _silu(z_ref[0])
  sq = yv * yv
  half = SSD_WIDTH // SSD_GROUPS
  r0 = lax.rsqrt(jnp.mean(sq[:, :half], axis=-1, keepdims=True) + EPS)
  r1 = lax.rsqrt(jnp.mean(sq[:, half:], axis=-1, keepdims=True) + EPS)
  out = jnp.concatenate([yv[:, :half] * r0, yv[:, half:] * r1], axis=1) * ng_ref[...]
  ssd_ref[0] = out.astype(ssd_ref.dtype)


def _ssd(xbc, z, dt, prefix, h0, conv_w, conv_b, dtb_p, alog_p, dsk_e, norm_g, tv):
  nb, tb, _ = xbc.shape
  nc = tb // tv
  e_np = np.zeros((LANES, SSD_WIDTH), np.float32)
  for h in range(SSD_HEADS):
    e_np[h, h * SSD_HEAD_DIM:(h + 1) * SSD_HEAD_DIM] = 1.0
  tri_np = np.tril(np.ones((SSD_CHUNK, SSD_CHUNK), np.float32))
  full = lambda shape: pl.BlockSpec(shape, lambda b, c: (0,) * len(shape))
  tok = lambda w: pl.BlockSpec((1, tv, w), lambda b, c: (b, c, 0))
  per_b = lambda r, w: pl.BlockSpec((1, r, w), lambda b, c: (b, 0, 0))
  return pl.pallas_call(
      functools.partial(_ssd_kernel, tv=tv, nc=nc),
      grid=(nb, nc),
      in_specs=[tok(CONV_DIM), tok(SSD_WIDTH), tok(LANES),
                per_b(CONV_W - 1, CONV_DIM), per_b(SSD_WIDTH, D_STATE),
                full((CONV_W, CONV_DIM)), full((1, CONV_DIM)), full((1, LANES)), full((1, LANES)),
                full((1, SSD_WIDTH)), full((1, SSD_WIDTH)),
                full((LANES, SSD_WIDTH)), full((SSD_CHUNK, SSD_CHUNK))],
      out_specs=[tok(SSD_WIDTH), per_b(CONV_W - 1, CONV_DIM), per_b(SSD_WIDTH, D_STATE)],
      out_shape=[jax.ShapeDtypeStruct((nb, tb, SSD_WIDTH), BF16),
                 jax.ShapeDtypeStruct((nb, CONV_W - 1, CONV_DIM), F32),
                 jax.ShapeDtypeStruct((nb, SSD_WIDTH, D_STATE), F32)],
      scratch_shapes=[pltpu.VMEM((SSD_CHUNK + SUBLANES, CONV_DIM), F32)],
      compiler_params=pltpu.CompilerParams(
          dimension_semantics=("arbitrary", "arbitrary"), vmem_limit_bytes=VMEM_LIMIT),
      name="ssd_scan",
  )(xbc, z, dt, prefix, h0, conv_w, conv_b, dtb_p, alog_p, dsk_e, norm_g,
    jnp.asarray(e_np, BF16), jnp.asarray(tri_np, BF16))


def _ffn_kernel(x_ref, att_ref, ssd_ref, ga1_ref, sc2_ref, sh2_ref, ga2_ref, gf_ref, gl_ref,
                wo_ref, wu_ref, wd_ref, y_ref):
  x = x_ref[0]
  mix = _dot(att_ref[0], wo_ref[0:ATT_WIDTH, :]) + _dot(ssd_ref[0], wo_ref[ATT_WIDTH:, :])
  x1 = x + ga1_ref[0] * mix
  var = jnp.mean(x1 * x1, axis=-1, keepdims=True)
  h2 = x1 * lax.rsqrt(var + EPS) * gf_ref[...]
  h2 = (h2 * (1.0 + sc2_ref[0]) + sh2_ref[0]).astype(BF16)
  fc = D_MODEL
  acc = jnp.zeros(x.shape, F32)
  for f in range(D_FF // fc):
    u = jnp.maximum(_dot(h2, wu_ref[:, f * fc:(f + 1) * fc]), 0.0)
    acc = acc + _dot((u * u).astype(BF16), wd_ref[f * fc:(f + 1) * fc, :])
  x2 = x1 + ga2_ref[0] * acc
  var2 = jnp.mean(x2 * x2, axis=-1, keepdims=True)
  y_ref[0] = x2 * lax.rsqrt(var2 + EPS) * gl_ref[...]


def _out_ffn(x3, att, ssd, ga1, sc2, sh2, ga2, g_ffn, g_final, w_out_b, w_up_b, w_down_b, tm):
  nb, tb, d = x3.shape
  nt = tb // tm
  per_row = ga1.shape[1] != 1
  if per_row:
    mspec = pl.BlockSpec((1, tm, d), lambda b, i: (b, i, 0))
  else:
    mspec = pl.BlockSpec((1, 1, d), lambda b, i: (b, 0, 0))
  tok = lambda w: pl.BlockSpec((1, tm, w), lambda b, i: (b, i, 0))
  const = lambda shape: pl.BlockSpec(shape, lambda b, i: (0, 0), pipeline_mode=pl.Buffered(1))
  return pl.pallas_call(
      _ffn_kernel,
      grid=(nb, nt),
      in_specs=[tok(d), tok(ATT_WIDTH), tok(SSD_WIDTH), mspec, mspec, mspec, mspec,
                const((1, d)), const((1, d)),
                const((d, d)), const((d, D_FF)), const((D_FF, d))],
      out_specs=tok(d),
      out_shape=jax.ShapeDtypeStruct((nb, tb, d), F32),
      compiler_params=pltpu.CompilerParams(
          dimension_semantics=("arbitrary", "arbitrary"), vmem_limit_bytes=VMEM_LIMIT),
      name="out_ffn",
  )(x3, att, ssd, ga1, sc2, sh2, ga2, g_ffn.reshape(1, d), g_final.reshape(1, d),
    w_out_b, w_up_b, w_down_b)


def kernel(x_prompt, x_sample, cache_k, cache_v, state_conv, state_ssm, page_table, c_prompt,
           c_sample, rel_bias, w_ada, b_ada, g_mix, g_ffn, w_in, w_out, lam_q1, lam_k1, lam_q2,
           lam_k2, subln_g, conv_w, conv_b, dt_bias, a_log, d_skip, ssd_norm_g, w_up, w_down,
           g_final):
  assert w_ada.shape[0] == 1, "single-layer step"
  bp, sp, d = x_prompt.shape
  bs, ts, _ = x_sample.shape
  n_pool = cache_k.shape[1]

  w_in_p = jnp.pad(w_in[0], ((0, 0), (0, IN_PAD - IN_WIDTH))).astype(BF16)
  w_kt = w_in[0][:, ATT_WIDTH:2 * ATT_WIDTH].T.astype(BF16)
  w_out_b = w_out[0].astype(BF16)
  w_up_b = w_up[0].astype(BF16)
  w_down_b = w_down[0].astype(BF16)
  lam4 = jnp.concatenate([lam_q1, lam_k1, lam_q2, lam_k2], axis=0)
  subln = subln_g.reshape(1, ATT_V_DIM)
  dtb_p = jnp.pad(dt_bias, ((0, 0), (0, LANES - SSD_HEADS)))
  alog_p = jnp.pad(a_log, ((0, 0), (0, LANES - SSD_HEADS)))
  dsk_e = jnp.repeat(d_skip[0], SSD_HEAD_DIM).reshape(1, SSD_WIDTH)
  ssd_args = (conv_w[0], conv_b, dtb_p, alog_p, dsk_e, ssd_norm_g)

  mods = _modulation(jnp.concatenate([c_prompt, c_sample], axis=0), w_ada[0], b_ada[0])
  mp = [m.reshape(bp, 1, d) for m in jnp.split(mods[:bp], 6, axis=-1)]
  ms = [jnp.repeat(m, ts, axis=0).reshape(1, bs * ts, d) for m in jnp.split(mods[bp:], 6, axis=-1)]
  near, diag, slast, snew = _bias_tiles(rel_bias)

  q_hm, kt_hm, v_hm, kt_p, v_p, z_p, xbc_p, dt_p = _in_proj(
      x_prompt, mp[1], mp[0], g_mix[0], w_in_p, w_kt, tm=512, prompt=True)
  att_p = _prompt_attention(q_hm, kt_hm, v_hm, near, diag, lam4, subln)
  ssd_p, conv_p, h_p = _ssd(
      xbc_p, z_p, dt_p, jnp.zeros((bp, CONV_W - 1, CONV_DIM), F32),
      jnp.zeros((bp, SSD_WIDTH, D_STATE), F32), *ssd_args, tv=SSD_CHUNK)
  y_p = _out_ffn(x_prompt, att_p, ssd_p, mp[2], mp[4], mp[3], mp[5], g_ffn[0], g_final,
                 w_out_b, w_up_b, w_down_b, tm=512)

  n_tok = bs * ts
  xs3 = x_sample.reshape(1, n_tok, d)
  q_s, k_s, v_s, z_s, xbc_s, dt_s = _in_proj(
      xs3, ms[1], ms[0], g_mix[0], w_in_p, w_kt, tm=n_tok, prompt=False)
  qt = q_s.reshape(ATT_HEADS, bs, ts, 2, ATT_HEAD_DIM).transpose(1, 0, 3, 2, 4)
  qt = qt.reshape(bs, 2 * ATT_HEADS, ts, ATT_HEAD_DIM)
  eye = jnp.eye(2 * ATT_HEADS, dtype=BF16)
  qb = (qt[:, :, :, None, :] * eye[None, :, None, :, None]).reshape(bs, 2 * ATT_HEADS * ts, ATT_WIDTH)
  blast = jnp.repeat(slast, 2, axis=0).reshape(2 * ATT_HEADS * SUBLANES, LANES)
  bnew = jnp.repeat(snew, 2, axis=0).reshape(2 * ATT_HEADS * SUBLANES, LANES)
  att_s = _sample_attention(
      page_table, qb, k_s.reshape(bs, ts, ATT_WIDTH), v_s.reshape(bs, ts, ATT_WIDTH),
      blast, bnew, lam4, subln,
      jnp.transpose(cache_k[0], (0, 2, 3, 4, 1)).reshape(n_pool, ATT_WIDTH, PAGE_SIZE),
      cache_v[0].reshape(n_pool, PAGE_SIZE * ATT_HEADS, ATT_V_DIM))
  ssd_s, conv_s, h_s = _ssd(
      xbc_s.reshape(bs, ts, CONV_DIM), z_s.reshape(bs, ts, SSD_WIDTH), dt_s.reshape(bs, ts, LANES),
      state_conv[0], state_ssm[0].reshape(bs, SSD_WIDTH, D_STATE), *ssd_args, tv=ts)
  y_s = _out_ffn(xs3, att_s.reshape(1, n_tok, ATT_WIDTH), ssd_s.reshape(1, n_tok, SSD_WIDTH),
                 ms[2], ms[4], ms[3], ms[5], g_ffn[0], g_final, w_out_b, w_up_b, w_down_b, tm=n_tok)

  hshape = (SSD_HEADS, SSD_HEAD_DIM, D_STATE)
  return (y_p, y_s.reshape(bs, ts, d),
          kt_p.reshape(1, bp, ATT_HEADS, 2, ATT_HEAD_DIM, sp).transpose(0, 1, 5, 2, 3, 4),
          v_p.reshape(1, bp, sp, ATT_HEADS, ATT_V_DIM),
          conv_p[None], h_p.reshape(1, bp, *hshape),
          k_s.reshape(1, bs, ts, ATT_HEADS, 2, ATT_HEAD_DIM),
          v_s.reshape(1, bs, ts, ATT_HEADS, ATT_V_DIM),
          conv_s[None], h_s.reshape(1, bs, *hshape))
```

```python
import functools
import math

import numpy as np
import jax
import jax.numpy as jnp
from jax import lax
from jax.experimental import pallas as pl
from jax.experimental.pallas import tpu as pltpu

F32 = jnp.float32
BF16 = jnp.bfloat16

D_MODEL = 1024
PAGE_SIZE = 128
ATT_HEADS = 4
ATT_HEAD_DIM = 64
ATT_V_DIM = 2 * ATT_HEAD_DIM
ATT_WIDTH = ATT_HEADS * ATT_V_DIM
SSD_HEADS = 8
SSD_HEAD_DIM = 64
SSD_WIDTH = SSD_HEADS * SSD_HEAD_DIM
SSD_GROUPS = 2
D_STATE = 128
CONV_W = 4
CONV_DIM = SSD_WIDTH + 2 * SSD_GROUPS * D_STATE
SSD_CHUNK = 128
IN_WIDTH = 3 * ATT_WIDTH + SSD_WIDTH + CONV_DIM + SSD_HEADS
D_FF = 4 * D_MODEL
N_BUCKETS = 32
MAX_DISTANCE = 128
MAX_EXACT = N_BUCKETS // 2
EPS = 1e-6
LAM_INIT = 0.8 - 0.6 * math.exp(-0.3 * 0)

LANES = 128
SUBLANES = 8
IN_PAD = 3 * ATT_WIDTH + SSD_WIDTH + CONV_DIM + LANES
LOG2E = 1.4426950408889634
NEG = -1e30
VMEM_LIMIT = 56 * 1024 * 1024

ATT_T = 256
PAGES_PER_STEP = 8


def _bucket_lower_bounds():
  d = np.arange(0, 4 * MAX_DISTANCE)
  nf = np.maximum(d, 1).astype(np.float64)
  large = MAX_EXACT + (np.log(nf / MAX_EXACT) / math.log(MAX_DISTANCE / MAX_EXACT)
                       * (N_BUCKETS - MAX_EXACT)).astype(np.int64)
  large = np.minimum(large, N_BUCKETS - 1)
  bucket = np.where(d < MAX_EXACT, d, large)
  return [int(np.argmax(bucket >= b)) for b in range(N_BUCKETS)]


_LOWER = _bucket_lower_bounds()
FAR_DIST = _LOWER[N_BUCKETS - 1]
assert FAR_DIST <= PAGE_SIZE and FAR_DIST <= ATT_T


def _silu(x):
  h = 0.5 * x
  return h * jnp.tanh(h) + h


def _dot(a, b):
  return jnp.dot(a, b, preferred_element_type=F32)


def _dot_nt(a, b):
  return lax.dot_general(a, b, (((1,), (1,)), ((), ())), preferred_element_type=F32)


def _split3(x):
  hi = x.astype(BF16)
  r1 = x - hi.astype(F32)
  mid = r1.astype(BF16)
  lo = (r1 - mid.astype(F32)).astype(BF16)
  return hi, mid, lo


def _mod_kernel(c_ref, w_ref, b_ref, o_ref):
  s = _silu(c_ref[...]).astype(BF16)
  o_ref[...] = _dot(s, w_ref[...].astype(BF16)) + b_ref[...]


def _modulation(c_all, w_ada, b_ada):
  n = c_all.shape[0]
  tn = D_MODEL
  return pl.pallas_call(
      _mod_kernel,
      grid=(6 * D_MODEL // tn,),
      in_specs=[pl.BlockSpec((n, D_MODEL), lambda j: (0, 0)),
                pl.BlockSpec((D_MODEL, tn), lambda j: (0, j)),
                pl.BlockSpec((1, tn), lambda j: (0, j))],
      out_specs=pl.BlockSpec((n, tn), lambda j: (0, j)),
      out_shape=jax.ShapeDtypeStruct((n, 6 * D_MODEL), F32),
      compiler_params=pltpu.CompilerParams(vmem_limit_bytes=VMEM_LIMIT),
      name="modulation",
  )(c_all, w_ada, b_ada.reshape(1, -1))


def _bias_kernel(tab_ref, near_ref, diag_ref, slast_ref, snew_ref):
  def bias_of(dist, h):
    val = jnp.full(dist.shape, tab_ref[0, h], F32)
    for b in range(1, N_BUCKETS):
      val = jnp.where(dist >= _LOWER[b], tab_ref[b, h], val)
    return (val - tab_ref[N_BUCKETS - 1, h]) * LOG2E

  t = ATT_T
  r = lax.broadcasted_iota(jnp.int32, (t, t), 0)
  c = lax.broadcasted_iota(jnp.int32, (t, t), 1)
  r8 = lax.broadcasted_iota(jnp.int32, (SUBLANES, LANES), 0)
  c8 = lax.broadcasted_iota(jnp.int32, (SUBLANES, LANES), 1)
  for h in range(ATT_HEADS):
    near_ref[h] = bias_of(t + r - c, h)
    d = r - c
    diag_ref[h] = jnp.where(d >= 0, bias_of(d, h), NEG)
    slast_ref[h] = bias_of(PAGE_SIZE + r8 - c8, h)
    dn = r8 - c8
    snew_ref[h] = jnp.where(dn >= 0, bias_of(dn, h), NEG)


def _bias_tiles(rel_bias):
  t = ATT_T
  return pl.pallas_call(
      _bias_kernel,
      in_specs=[pl.BlockSpec(memory_space=pltpu.SMEM)],
      out_shape=(jax.ShapeDtypeStruct((ATT_HEADS, t, t), F32),
                 jax.ShapeDtypeStruct((ATT_HEADS, t, t), F32),
                 jax.ShapeDtypeStruct((ATT_HEADS, SUBLANES, LANES), F32),
                 jax.ShapeDtypeStruct((ATT_HEADS, SUBLANES, LANES), F32)),
      compiler_params=pltpu.CompilerParams(vmem_limit_bytes=VMEM_LIMIT),
      name="bias_tiles",
  )(rel_bias)


def _inproj_kernel(x_ref, sc_ref, sh_ref, g_ref, w_ref, wkt_ref, *out_refs, prompt):
  x = x_ref[0]
  var = jnp.mean(x * x, axis=-1, keepdims=True)
  h = x * lax.rsqrt(var + EPS) * g_ref[...]
  h = h * (1.0 + sc_ref[0]) + sh_ref[0]
  hb = h.astype(BF16)
  aw = ATT_WIDTH
  q = _dot(hb, w_ref[:, 0:aw]) * (ATT_HEAD_DIM ** -0.5 * LOG2E)
  v = _dot(hb, w_ref[:, 2 * aw:3 * aw])
  if prompt:
    q_ref, ktb_ref, vb_ref, kt_ref, v_ref, z_ref, xbc_ref, dt_ref = out_refs
    kt = _dot_nt(wkt_ref[...], hb)
    kt_ref[0] = kt
    t = ATT_T
    for hd in range(ATT_HEADS):
      sl = slice(hd * ATT_V_DIM, (hd + 1) * ATT_V_DIM)
      vb_ref[0, hd] = v[:, sl].astype(BF16)
      v_ref[0, pl.ds(hd, v.shape[0], stride=ATT_HEADS), :] = v[:, sl]
      for cc in range(kt.shape[1] // t):
        ktb_ref[0, hd, cc] = kt[sl, cc * t:(cc + 1) * t].astype(BF16)
  else:
    q_ref, k_ref, v_ref, z_ref, xbc_ref, dt_ref = out_refs
    k_ref[0] = _dot(hb, w_ref[:, aw:2 * aw])
    v_ref[0] = v
  for hd in range(ATT_HEADS):
    sl = slice(hd * ATT_V_DIM, (hd + 1) * ATT_V_DIM)
    q_ref[0, hd] = q[:, sl].astype(BF16)
  o = 3 * aw
  z_ref[0] = _dot(hb, w_ref[:, o:o + SSD_WIDTH])
  o += SSD_WIDTH
  xbc_ref[0] = _dot(hb, w_ref[:, o:o + CONV_DIM])
  o += CONV_DIM
  dt_ref[0] = _dot(hb, w_ref[:, o:o + LANES])


def _in_proj(x3, sc, sh, g_mix, w_in_p, w_kt, tm, prompt):
  nb, tb, d = x3.shape
  nt = tb // tm
  per_row = sc.shape[1] != 1
  if per_row:
    mspec = pl.BlockSpec((1, tm, d), lambda b, i: (b, i, 0))
  else:
    mspec = pl.BlockSpec((1, 1, d), lambda b, i: (b, 0, 0))
  hm = pl.BlockSpec((1, ATT_HEADS, tm, ATT_V_DIM), lambda b, i: (b, 0, i, 0))
  tok = lambda w: pl.BlockSpec((1, tm, w), lambda b, i: (b, i, 0))
  hm_shape = jax.ShapeDtypeStruct((nb, ATT_HEADS, tb, ATT_V_DIM), BF16)
  tok_shape = lambda w: jax.ShapeDtypeStruct((nb, tb, w), F32)
  tail_specs = [tok(SSD_WIDTH), tok(CONV_DIM), tok(LANES)]
  tail_shapes = [tok_shape(SSD_WIDTH), tok_shape(CONV_DIM), tok_shape(LANES)]
  if prompt:
    t = ATT_T
    out_specs = [hm,
                 pl.BlockSpec((1, ATT_HEADS, tm // t, ATT_V_DIM, t), lambda b, i: (b, 0, i, 0, 0)),
                 hm,
                 pl.BlockSpec((1, ATT_WIDTH, tm), lambda b, i: (b, 0, i)),
                 pl.BlockSpec((1, tm * ATT_HEADS, ATT_V_DIM), lambda b, i: (b, i, 0))] + tail_specs
    out_shape = [hm_shape,
                 jax.ShapeDtypeStruct((nb, ATT_HEADS, tb // t, ATT_V_DIM, t), BF16),
                 hm_shape,
                 jax.ShapeDtypeStruct((nb, ATT_WIDTH, tb), F32),
                 jax.ShapeDtypeStruct((nb, tb * ATT_HEADS, ATT_V_DIM), F32)] + tail_shapes
  else:
    out_specs = [hm, tok(ATT_WIDTH), tok(ATT_WIDTH)] + tail_specs
    out_shape = [hm_shape, tok_shape(ATT_WIDTH), tok_shape(ATT_WIDTH)] + tail_shapes
  return pl.pallas_call(
      functools.partial(_inproj_kernel, prompt=prompt),
      grid=(nb, nt),
      in_specs=[tok(d), mspec, mspec,
                pl.BlockSpec((1, d), lambda b, i: (0, 0)),
                pl.BlockSpec((d, IN_PAD), lambda b, i: (0, 0)),
                pl.BlockSpec((ATT_WIDTH, d), lambda b, i: (0, 0))],
      out_specs=out_specs,
      out_shape=out_shape,
      compiler_params=pltpu.CompilerParams(
          dimension_semantics=("arbitrary", "arbitrary"), vmem_limit_bytes=VMEM_LIMIT),
      name="in_proj",
  )(x3, sc, sh, g_mix.reshape(1, d), w_in_p, w_kt)


def _diff_lambda(lam_ref):
  lv = lam_ref[...]
  s1 = jnp.sum(lv[0:1] * lv[1:2], axis=-1, keepdims=True)
  s2 = jnp.sum(lv[2:3] * lv[3:4], axis=-1, keepdims=True)
  return jnp.exp(s1) - jnp.exp(s2) + LAM_INIT


def _diff_combine(o0, o1, lam, g):
  d = o0 - lam * o1
  d = d * lax.rsqrt(jnp.mean(d * d, axis=-1, keepdims=True) + EPS) * g
  return d * (1.0 - LAM_INIT)


def _attn_kernel(q_ref, k_ref, v_ref, near_ref, diag_ref, lam_ref, g_ref, o_ref,
                 m_sc, acc_sc):
  t = ATT_T
  qi = pl.program_id(1)
  lane = lax.broadcasted_iota(jnp.int32, (t, ATT_V_DIM), 1)
  m_sc[...] = jnp.full(m_sc.shape, NEG, F32)
  acc_sc[...] = jnp.zeros(acc_sc.shape, F32)
  ones = jnp.ones((t, LANES), BF16)

  def head_step(h, c, bias_ref):
    q = q_ref[0, h]
    zero = jnp.zeros_like(q)
    q2 = jnp.concatenate([jnp.where(lane < ATT_HEAD_DIM, q, zero),
                          jnp.where(lane >= ATT_HEAD_DIM, q, zero)], axis=0)
    start = pl.multiple_of(c * t, t)
    kc = k_ref[0, h, c]
    vc = v_ref[0, h, pl.ds(start, t), :]
    s = _dot(q2, kc)
    if bias_ref is not None:
      bias = bias_ref[h]
      s = s + jnp.concatenate([bias, bias], axis=0)
    m_prev = m_sc[h]
    m_new = jnp.maximum(m_prev, jnp.max(s, axis=-1, keepdims=True))
    alpha = jnp.exp2(m_prev - m_new)
    p = jnp.exp2(s - jnp.concatenate([m_new] * (t // LANES), axis=1))
    pv = _dot(p.astype(BF16), jnp.concatenate([vc, ones], axis=1))
    acc_sc[h] = acc_sc[h] * jnp.concatenate([alpha, alpha], axis=1) + pv
    m_sc[h] = m_new

  def step(c, bias_ref):
    for h in range(ATT_HEADS):
      head_step(h, c, bias_ref)

  def far_body(c, carry):
    step(c, None)
    return carry

  lax.fori_loop(0, jnp.maximum(qi - 1, 0), far_body, 0)

  @pl.when(qi >= 1)
  def _():
    step(qi - 1, near_ref)

  step(qi, diag_ref)

  lam = _diff_lambda(lam_ref)
  for h in range(ATT_HEADS):
    acc = acc_sc[h]
    o = acc[:, :ATT_V_DIM] / acc[:, ATT_V_DIM:]
    d = _diff_combine(o[:t], o[t:], lam, g_ref[...])
    o_ref[0, :, h * ATT_V_DIM:(h + 1) * ATT_V_DIM] = d.astype(o_ref.dtype)


def _prompt_attention(q_hm, kt_hm, v_hm, near, diag, lam4, subln_g):
  b, nh, s, e = q_hm.shape
  t = ATT_T
  nq = s // t
  return pl.pallas_call(
      _attn_kernel,
      grid=(b, nq),
      in_specs=[pl.BlockSpec((1, nh, t, e), lambda bi, qi: (bi, 0, qi, 0)),
                pl.BlockSpec((1, nh, nq, e, t), lambda bi, qi: (bi, 0, 0, 0, 0)),
                pl.BlockSpec((1, nh, s, e), lambda bi, qi: (bi, 0, 0, 0)),
                pl.BlockSpec((nh, t, t), lambda bi, qi: (0, 0, 0)),
                pl.BlockSpec((nh, t, t), lambda bi, qi: (0, 0, 0)),
                pl.BlockSpec((4, ATT_HEAD_DIM), lambda bi, qi: (0, 0)),
                pl.BlockSpec((1, e), lambda bi, qi: (0, 0))],
      out_specs=pl.BlockSpec((1, t, nh * e), lambda bi, qi: (bi, qi, 0)),
      out_shape=jax.ShapeDtypeStruct((b, s, nh * e), BF16),
      scratch_shapes=[pltpu.VMEM((nh, 2 * t, LANES), F32),
                      pltpu.VMEM((nh, 2 * t, 2 * LANES), F32)],
      compiler_params=pltpu.CompilerParams(
          dimension_semantics=("arbitrary", "arbitrary"), vmem_limit_bytes=VMEM_LIMIT),
      name="prompt_attention",
  )(q_hm, kt_hm, v_hm, near, diag, lam4, subln_g)


def _sample_attn_kernel(pt_ref, qb_ref, kn_ref, vn_ref, blast_ref, bnew_ref, lam_ref, g_ref,
                        ck_hbm, cv_hbm, o_ref,
                        kbuf, vbuf, sem, m_sc, l_sc, acc_sc, kpad, vpad, *, n_seq, n_grp):
  g_pages = PAGES_PER_STEP
  b = pl.program_id(0)
  g = pl.program_id(1)
  step_idx = b * n_grp + g
  slot = step_idx % 2

  def page_copies(bb, gg, sl):
    copies = []
    for i in range(g_pages):
      pg = pt_ref[bb, gg * g_pages + i]
      copies.append((pltpu.make_async_copy(ck_hbm.at[pg], kbuf.at[sl, i], sem.at[0, sl]), 0))
      copies.append((pltpu.make_async_copy(cv_hbm.at[pg], vbuf.at[sl, i], sem.at[1, sl]), 1))
    return copies

  @pl.when(step_idx == 0)
  def _():
    kpad[...] = jnp.zeros(kpad.shape, kpad.dtype)
    vpad[...] = jnp.zeros(vpad.shape, vpad.dtype)
    for cp, prio in page_copies(0, 0, 0):
      cp.start(priority=prio)

  @pl.when(step_idx + 1 < n_seq * n_grp)
  def _():
    last = g == n_grp - 1
    nb = jnp.where(last, b + 1, b)
    ng = jnp.where(last, 0, g + 1)
    for cp, prio in page_copies(nb, ng, 1 - slot):
      cp.start(priority=prio)

  @pl.when(g == 0)
  def _():
    m_sc[...] = jnp.full(m_sc.shape, NEG, F32)
    l_sc[...] = jnp.zeros(l_sc.shape, F32)
    acc_sc[...] = jnp.zeros(acc_sc.shape, F32)

  for cp, _ in page_copies(b, g, slot):
    cp.wait()

  qb = qb_ref[0]

  rows_per_head = 2 * SUBLANES

  def update(s, values_of_head):
    m_prev = m_sc[...]
    m_new = jnp.maximum(m_prev, jnp.max(s, axis=-1, keepdims=True))
    alpha = jnp.exp2(m_prev - m_new)
    p = jnp.exp2(s - m_new[:, 0:1])
    l_sc[...] = l_sc[...] * alpha + jnp.sum(p, axis=-1, keepdims=True)
    pb = p.astype(BF16)
    pv = [_dot(pb[h * rows_per_head:(h + 1) * rows_per_head, :], values_of_head(h))
          for h in range(ATT_HEADS)]
    acc_sc[...] = acc_sc[...] * alpha + jnp.concatenate(pv, axis=0)
    m_sc[...] = m_new

  is_last = g == n_grp - 1

  def cache_step(sl):
    s = [_dot(qb, kbuf[sl, i].astype(BF16)) for i in range(g_pages)]
    s[-1] = s[-1] + jnp.where(is_last, blast_ref[...], 0.0)

    def values_of_head(h):
      v = [vbuf[sl, i, pl.ds(h, PAGE_SIZE, stride=ATT_HEADS), :] for i in range(g_pages)]
      return jnp.concatenate(v, axis=0).astype(BF16)

    update(jnp.concatenate(s, axis=1), values_of_head)

  for sl in range(2):
    pl.when(slot == sl)(functools.partial(cache_step, sl))

  @pl.when(is_last)
  def _():
    t_new = kn_ref.shape[1]
    kpad[0:t_new, :] = kn_ref[0].astype(BF16)
    vpad[0:t_new, :] = vn_ref[0].astype(BF16)
    sn = _dot_nt(qb, kpad[...]) + bnew_ref[...]
    update(sn, lambda h: vpad[:, h * ATT_V_DIM:(h + 1) * ATT_V_DIM])
    o = acc_sc[...] / l_sc[...]
    lam = _diff_lambda(lam_ref)
    for h in range(ATT_HEADS):
      o0 = o[(2 * h) * SUBLANES:(2 * h + 1) * SUBLANES, :]
      o1 = o[(2 * h + 1) * SUBLANES:(2 * h + 2) * SUBLANES, :]
      o_ref[0, :, h * ATT_V_DIM:(h + 1) * ATT_V_DIM] = _diff_combine(
          o0, o1, lam, g_ref[...]).astype(o_ref.dtype)


def _sample_attention(page_table, qb, k_new, v_new, blast, bnew, lam4, subln_g, cache_k, cache_v):
  n_seq, n_pages = page_table.shape
  t_new = k_new.shape[1]
  assert t_new == SUBLANES and n_pages % PAGES_PER_STEP == 0
  n_grp = n_pages // PAGES_PER_STEP
  w = ATT_WIDTH
  assert cache_k.shape[1:] == (w, PAGE_SIZE) and cache_v.shape[1:] == (PAGE_SIZE * ATT_HEADS, ATT_V_DIM)
  full = lambda shape: pl.BlockSpec(shape, lambda b, g, pt: (0,) * len(shape))
  grid_spec = pltpu.PrefetchScalarGridSpec(
      num_scalar_prefetch=1,
      grid=(n_seq, n_grp),
      in_specs=[pl.BlockSpec((1, 8 * SUBLANES, w), lambda b, g, pt: (b, 0, 0)),
                pl.BlockSpec((1, t_new, w), lambda b, g, pt: (b, 0, 0)),
                pl.BlockSpec((1, t_new, w), lambda b, g, pt: (b, 0, 0)),
                full((8 * SUBLANES, LANES)), full((8 * SUBLANES, LANES)),
                full((4, ATT_HEAD_DIM)), full((1, ATT_V_DIM)),
                pl.BlockSpec(memory_space=pl.ANY),
                pl.BlockSpec(memory_space=pl.ANY)],
      out_specs=pl.BlockSpec((1, t_new, w), lambda b, g, pt: (b, 0, 0)),
      scratch_shapes=[pltpu.VMEM((2, PAGES_PER_STEP) + cache_k.shape[1:], F32),
                      pltpu.VMEM((2, PAGES_PER_STEP) + cache_v.shape[1:], F32),
                      pltpu.SemaphoreType.DMA((2, 2)),
                      pltpu.VMEM((8 * SUBLANES, LANES), F32),
                      pltpu.VMEM((8 * SUBLANES, LANES), F32),
                      pltpu.VMEM((8 * SUBLANES, ATT_V_DIM), F32),
                      pltpu.VMEM((PAGE_SIZE, w), BF16),
                      pltpu.VMEM((PAGE_SIZE, w), BF16)])
  return pl.pallas_call(
      functools.partial(_sample_attn_kernel, n_seq=n_seq, n_grp=n_grp),
      grid_spec=grid_spec,
      out_shape=jax.ShapeDtypeStruct((n_seq, t_new, w), BF16),
      compiler_params=pltpu.CompilerParams(
          dimension_semantics=("arbitrary", "arbitrary"), vmem_limit_bytes=VMEM_LIMIT),
      name="sample_attention",
  )(page_table, qb, k_new, v_new, blast, bnew, lam4, subln_g, cache_k, cache_v)


def _ssd_kernel(xbc_ref, z_ref, dt_ref, pre_ref, h0_ref, cw_ref, cb_ref, dtb_ref, alog_ref,
                dsk_ref, ng_ref, e_ref, tri_ref,
                ssd_ref, conv_ref, h_ref, xp_sc, *, tv, nc):
  L = SSD_CHUNK
  hp = SUBLANES
  b = pl.program_id(0)
  c = pl.program_id(1)

  @pl.when((b == 0) & (c == 0))
  def _():
    xp_sc[...] = jnp.zeros(xp_sc.shape, F32)

  @pl.when(c == 0)
  def _():
    xp_sc[hp - (CONV_W - 1):hp, :] = pre_ref[0]
    h_ref[0] = h0_ref[0]

  xp_sc[hp:hp + tv, :] = xbc_ref[0]
  cw = cw_ref[...]
  y = cb_ref[...]
  for i in range(CONV_W):
    o = hp - (CONV_W - 1) + i
    y = y + xp_sc[o:o + L, :] * cw[i:i + 1]
  xa = _silu(y)

  @pl.when(c == nc - 1)
  def _():
    conv_ref[0] = xp_sc[hp + tv - (CONV_W - 1):hp + tv, :]

  xp_sc[0:hp, :] = xp_sc[tv:tv + hp, :]

  xs = xa[:, :SSD_WIDTH]
  gw = SSD_GROUPS * D_STATE
  bmat = xa[:, SSD_WIDTH:SSD_WIDTH + gw].astype(BF16)
  cmat = xa[:, SSD_WIDTH + gw:].astype(BF16)

  x = dt_ref[0] + dtb_ref[...]
  dtv = jnp.maximum(x, 0.0) + jnp.log1p(jnp.exp(-jnp.abs(x)))
  if tv < L:
    dtv = jnp.concatenate([dtv, jnp.zeros((L - tv, LANES), F32)], axis=0)
  a = dtv * (-jnp.exp(alog_ref[...]))

  tri = tri_ref[...]
  acs = sum(_dot(tri, part) for part in _split3(a))
  e = e_ref[...]
  aexp = sum(_dot(part, e) for part in _split3(acs))
  dtexp = _dot(dtv.astype(BF16), e)
  xd = xs * dtexp
  a_last = acs[L - 1:L, :]
  xdd_t = (xd * jnp.exp(aexp[L - 1:L, :] - aexp)).T
  acs_t = acs.T

  ri = lax.broadcasted_iota(jnp.int32, (L, L), 0)
  ci = lax.broadcasted_iota(jnp.int32, (L, L), 1)
  causal = ri >= ci
  lane = lax.broadcasted_iota(jnp.int32, (L, LANES), 1)
  heads_per_group = SSD_HEADS // SSD_GROUPS
  hprev = h_ref[0]
  hb = hprev.astype(BF16)

  ydiag, yoff, upd = [], [], []
  for g in range(SSD_GROUPS):
    cg = cmat[:, g * D_STATE:(g + 1) * D_STATE]
    bg = bmat[:, g * D_STATE:(g + 1) * D_STATE]
    scores = _dot_nt(cg, bg)
    for pair in range(heads_per_group // 2):
      h0 = g * heads_per_group + 2 * pair
      ms = []
      for h in (h0, h0 + 1):
        col = jnp.broadcast_to(acs[:, h:h + 1], (L, L))
        row = acs_t[h:h + 1, :]
        lm = jnp.exp(jnp.where(causal, col - row, NEG))
        ms.append((scores * lm).astype(BF16))
      xpair = xd[:, h0 * SSD_HEAD_DIM:(h0 + 2) * SSD_HEAD_DIM]
      xbd = jnp.concatenate([jnp.where(lane < SSD_HEAD_DIM, xpair, 0.0),
                             jnp.where(lane >= SSD_HEAD_DIM, xpair, 0.0)], axis=0)
      ydiag.append(_dot(jnp.concatenate(ms, axis=1), xbd.astype(BF16)))
    rows = slice(g * heads_per_group * SSD_HEAD_DIM, (g + 1) * heads_per_group * SSD_HEAD_DIM)
    yoff.append(_dot_nt(cg, hb[rows, :]))
    upd.append(_dot(xdd_t[rows, :].astype(BF16), bg))

  y = (jnp.concatenate(ydiag, axis=1) + jnp.concatenate(yoff, axis=1) * jnp.exp(aexp)
       + dsk_ref[...] * xs)

  decayed = []
  chunk_decay = jnp.exp(a_last)
  for h in range(SSD_HEADS):
    dec = jnp.broadcast_to(chunk_decay[:, h:h + 1], (SSD_HEAD_DIM, D_STATE))
    decayed.append(hprev[h * SSD_HEAD_DIM:(h + 1) * SSD_HEAD_DIM, :] * dec)
  h_ref[0] = jnp.concatenate(decayed, axis=0) + jnp.concatenate(upd, axis=0)

  yv = y[:tv] * _silu(z_ref[0])
  sq = yv * yv
  half = SSD_WIDTH // SSD_GROUPS
  r0 = lax.rsqrt(jnp.mean(sq[:, :half], axis=-1, keepdims=True) + EPS)
  r1 = lax.rsqrt(jnp.mean(sq[:, half:], axis=-1, keepdims=True) + EPS)
  out = jnp.concatenate([yv[:, :half] * r0, yv[:, half:] * r1], axis=1) * ng_ref[...]
  ssd_ref[0] = out.astype(ssd_ref.dtype)


def _ssd(xbc, z, dt, prefix, h0, conv_w, conv_b, dtb_p, alog_p, dsk_e, norm_g, tv):
  nb, tb, _ = xbc.shape
  nc = tb // tv
  e_np = np.zeros((LANES, SSD_WIDTH), np.float32)
  for h in range(SSD_HEADS):
    e_np[h, h * SSD_HEAD_DIM:(h + 1) * SSD_HEAD_DIM] = 1.0
  tri_np = np.tril(np.ones((SSD_CHUNK, SSD_CHUNK), np.float32))
  full = lambda shape: pl.BlockSpec(shape, lambda b, c: (0,) * len(shape))
  tok = lambda w: pl.BlockSpec((1, tv, w), lambda b, c: (b, c, 0))
  per_b = lambda r, w: pl.BlockSpec((1, r, w), lambda b, c: (b, 0, 0))
  return pl.pallas_call(
      functools.partial(_ssd_kernel, tv=tv, nc=nc),
      grid=(nb, nc),
      in_specs=[tok(CONV_DIM), tok(SSD_WIDTH), tok(LANES),
                per_b(CONV_W - 1, CONV_DIM), per_b(SSD_WIDTH, D_STATE),
                full((CONV_W, CONV_DIM)), full((1, CONV_DIM)), full((1, LANES)), full((1, LANES)),
                full((1, SSD_WIDTH)), full((1, SSD_WIDTH)),
                full((LANES, SSD_WIDTH)), full((SSD_CHUNK, SSD_CHUNK))],
      out_specs=[tok(SSD_WIDTH), per_b(CONV_W - 1, CONV_DIM), per_b(SSD_WIDTH, D_STATE)],
      out_shape=[jax.ShapeDtypeStruct((nb, tb, SSD_WIDTH), BF16),
                 jax.ShapeDtypeStruct((nb, CONV_W - 1, CONV_DIM), F32),
                 jax.ShapeDtypeStruct((nb, SSD_WIDTH, D_STATE), F32)],
      scratch_shapes=[pltpu.VMEM((SSD_CHUNK + SUBLANES, CONV_DIM), F32)],
      compiler_params=pltpu.CompilerParams(
          dimension_semantics=("arbitrary", "arbitrary"), vmem_limit_bytes=VMEM_LIMIT),
      name="ssd_scan",
  )(xbc, z, dt, prefix, h0, conv_w, conv_b, dtb_p, alog_p, dsk_e, norm_g,
    jnp.asarray(e_np, BF16), jnp.asarray(tri_np, BF16))


def _ffn_kernel(x_ref, att_ref, ssd_ref, ga1_ref, sc2_ref, sh2_ref, ga2_ref, gf_ref, gl_ref,
                wo_ref, wu_ref, wd_ref, y_ref):
  x = x_ref[0]
  mix = _dot(att_ref[0], wo_ref[0:ATT_WIDTH, :]) + _dot(ssd_ref[0], wo_ref[ATT_WIDTH:, :])
  x1 = x + ga1_ref[0] * mix
  var = jnp.mean(x1 * x1, axis=-1, keepdims=True)
  h2 = x1 * lax.rsqrt(var + EPS) * gf_ref[...]
  h2 = (h2 * (1.0 + sc2_ref[0]) + sh2_ref[0]).astype(BF16)
  fc = D_MODEL
  acc = jnp.zeros(x.shape, F32)
  for f in range(D_FF // fc):
    u = jnp.maximum(_dot(h2, wu_ref[:, f * fc:(f + 1) * fc]), 0.0)
    acc = acc + _dot((u * u).astype(BF16), wd_ref[f * fc:(f + 1) * fc, :])
  x2 = x1 + ga2_ref[0] * acc
  var2 = jnp.mean(x2 * x2, axis=-1, keepdims=True)
  y_ref[0] = x2 * lax.rsqrt(var2 + EPS) * gl_ref[...]


def _out_ffn(x3, att, ssd, ga1, sc2, sh2, ga2, g_ffn, g_final, w_out_b, w_up_b, w_down_b, tm):
  nb, tb, d = x3.shape
  nt = tb // tm
  per_row = ga1.shape[1] != 1
  if per_row:
    mspec = pl.BlockSpec((1, tm, d), lambda b, i: (b, i, 0))
  else:
    mspec = pl.BlockSpec((1, 1, d), lambda b, i: (b, 0, 0))
  tok = lambda w: pl.BlockSpec((1, tm, w), lambda b, i: (b, i, 0))
  const = lambda shape: pl.BlockSpec(shape, lambda b, i: (0, 0), pipeline_mode=pl.Buffered(1))
  return pl.pallas_call(
      _ffn_kernel,
      grid=(nb, nt),
      in_specs=[tok(d), tok(ATT_WIDTH), tok(SSD_WIDTH), mspec, mspec, mspec, mspec,
                const((1, d)), const((1, d)),
                const((d, d)), const((d, D_FF)), const((D_FF, d))],
      out_specs=tok(d),
      out_shape=jax.ShapeDtypeStruct((nb, tb, d), F32),
      compiler_params=pltpu.CompilerParams(
          dimension_semantics=("arbitrary", "arbitrary"), vmem_limit_bytes=VMEM_LIMIT),
      name="out_ffn",
  )(x3, att, ssd, ga1, sc2, sh2, ga2, g_ffn.reshape(1, d), g_final.reshape(1, d),
    w_out_b, w_up_b, w_down_b)


def kernel(x_prompt, x_sample, cache_k, cache_v, state_conv, state_ssm, page_table, c_prompt,
           c_sample, rel_bias, w_ada, b_ada, g_mix, g_ffn, w_in, w_out, lam_q1, lam_k1, lam_q2,
           lam_k2, subln_g, conv_w, conv_b, dt_bias, a_log, d_skip, ssd_norm_g, w_up, w_down,
           g_final):
  assert w_ada.shape[0] == 1, "single-layer step"
  bp, sp, d = x_prompt.shape
  bs, ts, _ = x_sample.shape
  n_pool = cache_k.shape[1]

  w_in_p = jnp.pad(w_in[0], ((0, 0), (0, IN_PAD - IN_WIDTH))).astype(BF16)
  w_kt = w_in[0][:, ATT_WIDTH:2 * ATT_WIDTH].T.astype(BF16)
  w_out_b = w_out[0].astype(BF16)
  w_up_b = w_up[0].astype(BF16)
  w_down_b = w_down[0].astype(BF16)
  lam4 = jnp.concatenate([lam_q1, lam_k1, lam_q2, lam_k2], axis=0)
  subln = subln_g.reshape(1, ATT_V_DIM)
  dtb_p = jnp.pad(dt_bias, ((0, 0), (0, LANES - SSD_HEADS)))
  alog_p = jnp.pad(a_log, ((0, 0), (0, LANES - SSD_HEADS)))
  dsk_e = jnp.repeat(d_skip[0], SSD_HEAD_DIM).reshape(1, SSD_WIDTH)
  ssd_args = (conv_w[0], conv_b, dtb_p, alog_p, dsk_e, ssd_norm_g)

  mods = _modulation(jnp.concatenate([c_prompt, c_sample], axis=0), w_ada[0], b_ada[0])
  mp = [m.reshape(bp, 1, d) for m in jnp.split(mods[:bp], 6, axis=-1)]
  ms = [jnp.repeat(m, ts, axis=0).reshape(1, bs * ts, d) for m in jnp.split(mods[bp:], 6, axis=-1)]
  near, diag, slast, snew = _bias_tiles(rel_bias)

  q_hm, kt_hm, v_hm, kt_p, v_p, z_p, xbc_p, dt_p = _in_proj(
      x_prompt, mp[1], mp[0], g_mix[0], w_in_p, w_kt, tm=512, prompt=True)
  att_p = _prompt_attention(q_hm, kt_hm, v_hm, near, diag, lam4, subln)
  ssd_p, conv_p, h_p = _ssd(
      xbc_p, z_p, dt_p, jnp.zeros((bp, CONV_W - 1, CONV_DIM), F32),
      jnp.zeros((bp, SSD_WIDTH, D_STATE), F32), *ssd_args, tv=SSD_CHUNK)
  y_p = _out_ffn(x_prompt, att_p, ssd_p, mp[2], mp[4], mp[3], mp[5], g_ffn[0], g_final,
                 w_out_b, w_up_b, w_down_b, tm=512)

  n_tok = bs * ts
  xs3 = x_sample.reshape(1, n_tok, d)
  q_s, k_s, v_s, z_s, xbc_s, dt_s = _in_proj(
      xs3, ms[1], ms[0], g_mix[0], w_in_p, w_kt, tm=n_tok, prompt=False)
  qt = q_s.reshape(ATT_HEADS, bs, ts, 2, ATT_HEAD_DIM).transpose(1, 0, 3, 2, 4)
  qt = qt.reshape(bs, 2 * ATT_HEADS, ts, ATT_HEAD_DIM)
  eye = jnp.eye(2 * ATT_HEADS, dtype=BF16)
  qb = (qt[:, :, :, None, :] * eye[None, :, None, :, None]).reshape(bs, 2 * ATT_HEADS * ts, ATT_WIDTH)
  blast = jnp.repeat(slast, 2, axis=0).reshape(2 * ATT_HEADS * SUBLANES, LANES)
  bnew = jnp.repeat(snew, 2, axis=0).reshape(2 * ATT_HEADS * SUBLANES, LANES)
  att_s = _sample_attention(
      page_table, qb, k_s.reshape(bs, ts, ATT_WIDTH), v_s.reshape(bs, ts, ATT_WIDTH),
      blast, bnew, lam4, subln,
      jnp.transpose(cache_k[0], (0, 2, 3, 4, 1)).reshape(n_pool, ATT_WIDTH, PAGE_SIZE),
      cache_v[0].reshape(n_pool, PAGE_SIZE * ATT_HEADS, ATT_V_DIM))
  ssd_s, conv_s, h_s = _ssd(
      xbc_s.reshape(bs, ts, CONV_DIM), z_s.reshape(bs, ts, SSD_WIDTH), dt_s.reshape(bs, ts, LANES),
      state_conv[0], state_ssm[0].reshape(bs, SSD_WIDTH, D_STATE), *ssd_args, tv=ts)
  y_s = _out_ffn(xs3, att_s.reshape(1, n_tok, ATT_WIDTH), ssd_s.reshape(1, n_tok, SSD_WIDTH),
                 ms[2], ms[4], ms[3], ms[5], g_ffn[0], g_final, w_out_b, w_up_b, w_down_b, tm=n_tok)

  hshape = (SSD_HEADS, SSD_HEAD_DIM, D_STATE)
  return (y_p, y_s.reshape(bs, ts, d),
          kt_p.reshape(1, bp, ATT_HEADS, 2, ATT_HEAD_DIM, sp).transpose(0, 1, 5, 2, 3, 4),
          v_p.reshape(1, bp, sp, ATT_HEADS, ATT_V_DIM),
          conv_p[None], h_p.reshape(1, bp, *hshape),
          k_s.reshape(1, bs, ts, ATT_HEADS, 2, ATT_HEAD_DIM),
          v_s.reshape(1, bs, ts, ATT_HEADS, ATT_V_DIM),
          conv_s[None], h_s.reshape(1, bs, *hshape))
```

```python
import functools
import math

import numpy as np
import jax
import jax.numpy as jnp
from jax import lax
from jax.experimental import pallas as pl
from jax.experimental.pallas import tpu as pltpu

F32 = jnp.float32
BF16 = jnp.bfloat16

D_MODEL = 1024
PAGE_SIZE = 128
ATT_HEADS = 4
ATT_HEAD_DIM = 64
ATT_V_DIM = 2 * ATT_HEAD_DIM
ATT_WIDTH = ATT_HEADS * ATT_V_DIM
SSD_HEADS = 8
SSD_HEAD_DIM = 64
SSD_WIDTH = SSD_HEADS * SSD_HEAD_DIM
SSD_GROUPS = 2
D_STATE = 128
CONV_W = 4
CONV_DIM = SSD_WIDTH + 2 * SSD_GROUPS * D_STATE
SSD_CHUNK = 128
IN_WIDTH = 3 * ATT_WIDTH + SSD_WIDTH + CONV_DIM + SSD_HEADS
D_FF = 4 * D_MODEL
N_BUCKETS = 32
MAX_DISTANCE = 128
MAX_EXACT = N_BUCKETS // 2
EPS = 1e-6
LAM_INIT = 0.8 - 0.6 * math.exp(-0.3 * 0)

LANES = 128
SUBLANES = 8
IN_PAD = 3 * ATT_WIDTH + SSD_WIDTH + CONV_DIM + LANES
LOG2E = 1.4426950408889634
NEG = -1e30
VMEM_LIMIT = 56 * 1024 * 1024

ATT_T = 256
PAGES_PER_STEP = 16
CACHE_SLOTS = 3


def _bucket_lower_bounds():
  d = np.arange(0, 4 * MAX_DISTANCE)
  nf = np.maximum(d, 1).astype(np.float64)
  large = MAX_EXACT + (np.log(nf / MAX_EXACT) / math.log(MAX_DISTANCE / MAX_EXACT)
                       * (N_BUCKETS - MAX_EXACT)).astype(np.int64)
  large = np.minimum(large, N_BUCKETS - 1)
  bucket = np.where(d < MAX_EXACT, d, large)
  return [int(np.argmax(bucket >= b)) for b in range(N_BUCKETS)]


_LOWER = _bucket_lower_bounds()
FAR_DIST = _LOWER[N_BUCKETS - 1]
assert FAR_DIST <= PAGE_SIZE and FAR_DIST <= ATT_T


def _silu(x):
  h = 0.5 * x
  return h * jnp.tanh(h) + h


def _dot(a, b):
  return jnp.dot(a, b, preferred_element_type=F32)


def _dot_nt(a, b):
  return lax.dot_general(a, b, (((1,), (1,)), ((), ())), preferred_element_type=F32)


def _split3(x):
  hi = x.astype(BF16)
  r1 = x - hi.astype(F32)
  mid = r1.astype(BF16)
  lo = (r1 - mid.astype(F32)).astype(BF16)
  return hi, mid, lo


def _mod_kernel(c_ref, w_ref, b_ref, o_ref):
  s = _silu(c_ref[...]).astype(BF16)
  o_ref[...] = _dot(s, w_ref[...].astype(BF16)) + b_ref[...]


def _modulation(c_all, w_ada, b_ada):
  n = c_all.shape[0]
  tn = D_MODEL
  return pl.pallas_call(
      _mod_kernel,
      grid=(6 * D_MODEL // tn,),
      in_specs=[pl.BlockSpec((n, D_MODEL), lambda j: (0, 0)),
                pl.BlockSpec((D_MODEL, tn), lambda j: (0, j)),
                pl.BlockSpec((1, tn), lambda j: (0, j))],
      out_specs=pl.BlockSpec((n, tn), lambda j: (0, j)),
      out_shape=jax.ShapeDtypeStruct((n, 6 * D_MODEL), F32),
      compiler_params=pltpu.CompilerParams(vmem_limit_bytes=VMEM_LIMIT),
      name="modulation",
  )(c_all, w_ada, b_ada.reshape(1, -1))


def _bias_kernel(tab_ref, near_ref, diag_ref, slast_ref, snew_ref):
  def bias_of(dist, h):
    val = jnp.full(dist.shape, tab_ref[0, h], F32)
    for b in range(1, N_BUCKETS):
      val = jnp.where(dist >= _LOWER[b], tab_ref[b, h], val)
    return (val - tab_ref[N_BUCKETS - 1, h]) * LOG2E

  t = ATT_T
  r = lax.broadcasted_iota(jnp.int32, (t, t), 0)
  c = lax.broadcasted_iota(jnp.int32, (t, t), 1)
  r8 = lax.broadcasted_iota(jnp.int32, (SUBLANES, LANES), 0)
  c8 = lax.broadcasted_iota(jnp.int32, (SUBLANES, LANES), 1)
  for h in range(ATT_HEADS):
    near_ref[h] = bias_of(t + r - c, h)
    d = r - c
    diag_ref[h] = jnp.where(d >= 0, bias_of(d, h), NEG)
    slast_ref[h] = bias_of(PAGE_SIZE + r8 - c8, h)
    dn = r8 - c8
    snew_ref[h] = jnp.where(dn >= 0, bias_of(dn, h), NEG)


def _bias_tiles(rel_bias):
  t = ATT_T
  return pl.pallas_call(
      _bias_kernel,
      in_specs=[pl.BlockSpec(memory_space=pltpu.SMEM)],
      out_shape=(jax.ShapeDtypeStruct((ATT_HEADS, t, t), F32),
                 jax.ShapeDtypeStruct((ATT_HEADS, t, t), F32),
                 jax.ShapeDtypeStruct((ATT_HEADS, SUBLANES, LANES), F32),
                 jax.ShapeDtypeStruct((ATT_HEADS, SUBLANES, LANES), F32)),
      compiler_params=pltpu.CompilerParams(vmem_limit_bytes=VMEM_LIMIT),
      name="bias_tiles",
  )(rel_bias)


def _inproj_kernel(x_ref, sc_ref, sh_ref, g_ref, w_ref, wkt_ref, *out_refs, prompt):
  x = x_ref[0]
  var = jnp.mean(x * x, axis=-1, keepdims=True)
  h = x * lax.rsqrt(var + EPS) * g_ref[...]
  h = h * (1.0 + sc_ref[0]) + sh_ref[0]
  hb = h.astype(BF16)
  aw = ATT_WIDTH
  q = _dot(hb, w_ref[:, 0:aw]) * (ATT_HEAD_DIM ** -0.5 * LOG2E)
  v = _dot(hb, w_ref[:, 2 * aw:3 * aw])
  if prompt:
    q_ref, ktb_ref, vb_ref, kt_ref, v_ref, z_ref, xbc_ref, dt_ref = out_refs
    kt = _dot_nt(wkt_ref[...], hb)
    kt_ref[0] = kt
    t = ATT_T
    for hd in range(ATT_HEADS):
      sl = slice(hd * ATT_V_DIM, (hd + 1) * ATT_V_DIM)
      vb_ref[0, hd] = v[:, sl].astype(BF16)
      v_ref[0, pl.ds(hd, v.shape[0], stride=ATT_HEADS), :] = v[:, sl]
      for cc in range(kt.shape[1] // t):
        ktb_ref[0, hd, cc] = kt[sl, cc * t:(cc + 1) * t].astype(BF16)
  else:
    q_ref, k_ref, v_ref, z_ref, xbc_ref, dt_ref = out_refs
    k_ref[0] = _dot(hb, w_ref[:, aw:2 * aw])
    v_ref[0] = v
  for hd in range(ATT_HEADS):
    sl = slice(hd * ATT_V_DIM, (hd + 1) * ATT_V_DIM)
    q_ref[0, hd] = q[:, sl].astype(BF16)
  o = 3 * aw
  z_ref[0] = _dot(hb, w_ref[:, o:o + SSD_WIDTH])
  o += SSD_WIDTH
  xbc_ref[0] = _dot(hb, w_ref[:, o:o + CONV_DIM])
  o += CONV_DIM
  dt_ref[0] = _dot(hb, w_ref[:, o:o + LANES])


def _in_proj(x3, sc, sh, g_mix, w_in_p, w_kt, tm, prompt):
  nb, tb, d = x3.shape
  nt = tb // tm
  per_row = sc.shape[1] != 1
  if per_row:
    mspec = pl.BlockSpec((1, tm, d), lambda b, i: (b, i, 0))
  else:
    mspec = pl.BlockSpec((1, 1, d), lambda b, i: (b, 0, 0))
  hm = pl.BlockSpec((1, ATT_HEADS, tm, ATT_V_DIM), lambda b, i: (b, 0, i, 0))
  tok = lambda w: pl.BlockSpec((1, tm, w), lambda b, i: (b, i, 0))
  hm_shape = jax.ShapeDtypeStruct((nb, ATT_HEADS, tb, ATT_V_DIM), BF16)
  tok_shape = lambda w: jax.ShapeDtypeStruct((nb, tb, w), F32)
  tail_specs = [tok(SSD_WIDTH), tok(CONV_DIM), tok(LANES)]
  tail_shapes = [tok_shape(SSD_WIDTH), tok_shape(CONV_DIM), tok_shape(LANES)]
  if prompt:
    t = ATT_T
    out_specs = [hm,
                 pl.BlockSpec((1, ATT_HEADS, tm // t, ATT_V_DIM, t), lambda b, i: (b, 0, i, 0, 0)),
                 hm,
                 pl.BlockSpec((1, ATT_WIDTH, tm), lambda b, i: (b, 0, i)),
                 pl.BlockSpec((1, tm * ATT_HEADS, ATT_V_DIM), lambda b, i: (b, i, 0))] + tail_specs
    out_shape = [hm_shape,
                 jax.ShapeDtypeStruct((nb, ATT_HEADS, tb // t, ATT_V_DIM, t), BF16),
                 hm_shape,
                 jax.ShapeDtypeStruct((nb, ATT_WIDTH, tb), F32),
                 jax.ShapeDtypeStruct((nb, tb * ATT_HEADS, ATT_V_DIM), F32)] + tail_shapes
  else:
    out_specs = [hm, tok(ATT_WIDTH), tok(ATT_WIDTH)] + tail_specs
    out_shape = [hm_shape, tok_shape(ATT_WIDTH), tok_shape(ATT_WIDTH)] + tail_shapes
  return pl.pallas_call(
      functools.partial(_inproj_kernel, prompt=prompt),
      grid=(nb, nt),
      in_specs=[tok(d), mspec, mspec,
                pl.BlockSpec((1, d), lambda b, i: (0, 0)),
                pl.BlockSpec((d, IN_PAD), lambda b, i: (0, 0)),
                pl.BlockSpec((ATT_WIDTH, d), lambda b, i: (0, 0))],
      out_specs=out_specs,
      out_shape=out_shape,
      compiler_params=pltpu.CompilerParams(
          dimension_semantics=("arbitrary", "arbitrary"), vmem_limit_bytes=VMEM_LIMIT),
      name="in_proj",
  )(x3, sc, sh, g_mix.reshape(1, d), w_in_p, w_kt)


def _diff_lambda(lam_ref):
  lv = lam_ref[...]
  s1 = jnp.sum(lv[0:1] * lv[1:2], axis=-1, keepdims=True)
  s2 = jnp.sum(lv[2:3] * lv[3:4], axis=-1, keepdims=True)
  return jnp.exp(s1) - jnp.exp(s2) + LAM_INIT


def _diff_combine(o0, o1, lam, g):
  d = o0 - lam * o1
  d = d * lax.rsqrt(jnp.mean(d * d, axis=-1, keepdims=True) + EPS) * g
  return d * (1.0 - LAM_INIT)


def _attn_kernel(q_ref, k_ref, v_ref, near_ref, diag_ref, lam_ref, g_ref, o_ref,
                 m_sc, acc_sc):
  t = ATT_T
  qi = pl.program_id(1)
  lane = lax.broadcasted_iota(jnp.int32, (t, ATT_V_DIM), 1)
  m_sc[...] = jnp.full(m_sc.shape, NEG, F32)
  acc_sc[...] = jnp.zeros(acc_sc.shape, F32)
  ones = jnp.ones((t, LANES), BF16)

  def head_step(h, c, bias_ref):
    q = q_ref[0, h]
    zero = jnp.zeros_like(q)
    q2 = jnp.concatenate([jnp.where(lane < ATT_HEAD_DIM, q, zero),
                          jnp.where(lane >= ATT_HEAD_DIM, q, zero)], axis=0)
    start = pl.multiple_of(c * t, t)
    kc = k_ref[0, h, c]
    vc = v_ref[0, h, pl.ds(start, t), :]
    s = _dot(q2, kc)
    if bias_ref is not None:
      bias = bias_ref[h]
      s = s + jnp.concatenate([bias, bias], axis=0)
    m_prev = m_sc[h]
    m_new = jnp.maximum(m_prev, jnp.max(s, axis=-1, keepdims=True))
    alpha = jnp.exp2(m_prev - m_new)
    p = jnp.exp2(s - jnp.concatenate([m_new] * (t // LANES), axis=1))
    pv = _dot(p.astype(BF16), jnp.concatenate([vc, ones], axis=1))
    acc_sc[h] = acc_sc[h] * jnp.concatenate([alpha, alpha], axis=1) + pv
    m_sc[h] = m_new

  def step(c, bias_ref):
    for h in range(ATT_HEADS):
      head_step(h, c, bias_ref)

  def far_body(c, carry):
    step(c, None)
    return carry

  lax.fori_loop(0, jnp.maximum(qi - 1, 0), far_body, 0)

  @pl.when(qi >= 1)
  def _():
    step(qi - 1, near_ref)

  step(qi, diag_ref)

  lam = _diff_lambda(lam_ref)
  for h in range(ATT_HEADS):
    acc = acc_sc[h]
    o = acc[:, :ATT_V_DIM] / acc[:, ATT_V_DIM:]
    d = _diff_combine(o[:t], o[t:], lam, g_ref[...])
    o_ref[0, :, h * ATT_V_DIM:(h + 1) * ATT_V_DIM] = d.astype(o_ref.dtype)


def _prompt_attention(q_hm, kt_hm, v_hm, near, diag, lam4, subln_g):
  b, nh, s, e = q_hm.shape
  t = ATT_T
  nq = s // t
  return pl.pallas_call(
      _attn_kernel,
      grid=(b, nq),
      in_specs=[pl.BlockSpec((1, nh, t, e), lambda bi, qi: (bi, 0, qi, 0)),
                pl.BlockSpec((1, nh, nq, e, t), lambda bi, qi: (bi, 0, 0, 0, 0)),
                pl.BlockSpec((1, nh, s, e), lambda bi, qi: (bi, 0, 0, 0)),
                pl.BlockSpec((nh, t, t), lambda bi, qi: (0, 0, 0)),
                pl.BlockSpec((nh, t, t), lambda bi, qi: (0, 0, 0)),
                pl.BlockSpec((4, ATT_HEAD_DIM), lambda bi, qi: (0, 0)),
                pl.BlockSpec((1, e), lambda bi, qi: (0, 0))],
      out_specs=pl.BlockSpec((1, t, nh * e), lambda bi, qi: (bi, qi, 0)),
      out_shape=jax.ShapeDtypeStruct((b, s, nh * e), BF16),
      scratch_shapes=[pltpu.VMEM((nh, 2 * t, LANES), F32),
                      pltpu.VMEM((nh, 2 * t, 2 * LANES), F32)],
      compiler_params=pltpu.CompilerParams(
          dimension_semantics=("arbitrary", "arbitrary"), vmem_limit_bytes=VMEM_LIMIT),
      name="prompt_attention",
  )(q_hm, kt_hm, v_hm, near, diag, lam4, subln_g)


def _sample_attn_kernel(pt_ref, qb_ref, kn_ref, vn_ref, blast_ref, bnew_ref, lam_ref, g_ref,
                        ck_hbm, cv_hbm, o_ref,
                        kbuf, vbuf, sem, m_sc, l_sc, acc_sc, kpad, vpad, *, n_seq, n_grp):
  g_pages = PAGES_PER_STEP
  b = pl.program_id(0)
  g = pl.program_id(1)
  step_idx = b * n_grp + g
  n_slots = kbuf.shape[0]
  ahead = n_slots - 1
  slot = step_idx % n_slots

  def page_copies(bb, gg, sl):
    copies = []
    for i in range(g_pages):
      pg = pt_ref[bb, gg * g_pages + i]
      copies.append((pltpu.make_async_copy(ck_hbm.at[pg], kbuf.at[sl, i], sem.at[0, sl]), 0))
      copies.append((pltpu.make_async_copy(cv_hbm.at[pg], vbuf.at[sl, i], sem.at[1, sl]), 1))
    return copies

  @pl.when(step_idx == 0)
  def _():
    kpad[...] = jnp.zeros(kpad.shape, kpad.dtype)
    vpad[...] = jnp.zeros(vpad.shape, vpad.dtype)
    for first in range(ahead):
      for cp, prio in page_copies(first // n_grp, first % n_grp, first % n_slots):
        cp.start(priority=prio)

  @pl.when(step_idx + ahead < n_seq * n_grp)
  def _():
    wrap = g + ahead >= n_grp
    nb = jnp.where(wrap, b + 1, b)
    ng = jnp.where(wrap, g + ahead - n_grp, g + ahead)
    for cp, prio in page_copies(nb, ng, (step_idx + ahead) % n_slots):
      cp.start(priority=prio)

  @pl.when(g == 0)
  def _():
    m_sc[...] = jnp.full(m_sc.shape, NEG, F32)
    l_sc[...] = jnp.zeros(l_sc.shape, F32)
    acc_sc[...] = jnp.zeros(acc_sc.shape, F32)

  for cp, _ in page_copies(b, g, slot):
    cp.wait()

  qb = qb_ref[0]

  rows_per_head = 2 * SUBLANES

  def update(s, values_of_head):
    m_prev = m_sc[...]
    m_new = jnp.maximum(m_prev, jnp.max(s, axis=-1, keepdims=True))
    alpha = jnp.exp2(m_prev - m_new)
    p = jnp.exp2(s - m_new[:, 0:1])
    l_sc[...] = l_sc[...] * alpha + jnp.sum(p, axis=-1, keepdims=True)
    pb = p.astype(BF16)
    pv = [_dot(pb[h * rows_per_head:(h + 1) * rows_per_head, :], values_of_head(h))
          for h in range(ATT_HEADS)]
    acc_sc[...] = acc_sc[...] * alpha + jnp.concatenate(pv, axis=0)
    m_sc[...] = m_new

  is_last = g == n_grp - 1

  def cache_step(sl):
    s = [_dot(qb, kbuf[sl, i].astype(BF16)) for i in range(g_pages)]
    s[-1] = s[-1] + jnp.where(is_last, blast_ref[...], 0.0)

    def values_of_head(h):
      v = [vbuf[sl, i, pl.ds(h, PAGE_SIZE, stride=ATT_HEADS), :] for i in range(g_pages)]
      return jnp.concatenate(v, axis=0).astype(BF16)

    update(jnp.concatenate(s, axis=1), values_of_head)

  for sl in range(n_slots):
    pl.when(slot == sl)(functools.partial(cache_step, sl))

  @pl.when(is_last)
  def _():
    t_new = kn_ref.shape[1]
    kpad[0:t_new, :] = kn_ref[0].astype(BF16)
    vpad[0:t_new, :] = vn_ref[0].astype(BF16)
    sn = _dot_nt(qb, kpad[...]) + bnew_ref[...]
    update(sn, lambda h: vpad[:, h * ATT_V_DIM:(h + 1) * ATT_V_DIM])
    o = acc_sc[...] / l_sc[...]
    lam = _diff_lambda(lam_ref)
    for h in range(ATT_HEADS):
      o0 = o[(2 * h) * SUBLANES:(2 * h + 1) * SUBLANES, :]
      o1 = o[(2 * h + 1) * SUBLANES:(2 * h + 2) * SUBLANES, :]
      o_ref[0, :, h * ATT_V_DIM:(h + 1) * ATT_V_DIM] = _diff_combine(
          o0, o1, lam, g_ref[...]).astype(o_ref.dtype)


def _sample_attention(page_table, qb, k_new, v_new, blast, bnew, lam4, subln_g, cache_k, cache_v):
  n_seq, n_pages = page_table.shape
  t_new = k_new.shape[1]
  assert t_new == SUBLANES and n_pages % PAGES_PER_STEP == 0
  n_grp = n_pages // PAGES_PER_STEP
  assert n_grp >= CACHE_SLOTS - 1
  w = ATT_WIDTH
  assert cache_k.shape[1:] == (w, PAGE_SIZE) and cache_v.shape[1:] == (PAGE_SIZE * ATT_HEADS, ATT_V_DIM)
  full = lambda shape: pl.BlockSpec(shape, lambda b, g, pt: (0,) * len(shape))
  grid_spec = pltpu.PrefetchScalarGridSpec(
      num_scalar_prefetch=1,
      grid=(n_seq, n_grp),
      in_specs=[pl.BlockSpec((1, 8 * SUBLANES, w), lambda b, g, pt: (b, 0, 0)),
                pl.BlockSpec((1, t_new, w), lambda b, g, pt: (b, 0, 0)),
                pl.BlockSpec((1, t_new, w), lambda b, g, pt: (b, 0, 0)),
                full((8 * SUBLANES, LANES)), full((8 * SUBLANES, LANES)),
                full((4, ATT_HEAD_DIM)), full((1, ATT_V_DIM)),
                pl.BlockSpec(memory_space=pl.ANY),
                pl.BlockSpec(memory_space=pl.ANY)],
      out_specs=pl.BlockSpec((1, t_new, w), lambda b, g, pt: (b, 0, 0)),
      scratch_shapes=[pltpu.VMEM((CACHE_SLOTS, PAGES_PER_STEP) + cache_k.shape[1:], F32),
                      pltpu.VMEM((CACHE_SLOTS, PAGES_PER_STEP) + cache_v.shape[1:], F32),
                      pltpu.SemaphoreType.DMA((2, CACHE_SLOTS)),
                      pltpu.VMEM((8 * SUBLANES, LANES), F32),
                      pltpu.VMEM((8 * SUBLANES, LANES), F32),
                      pltpu.VMEM((8 * SUBLANES, ATT_V_DIM), F32),
                      pltpu.VMEM((PAGE_SIZE, w), BF16),
                      pltpu.VMEM((PAGE_SIZE, w), BF16)])
  return pl.pallas_call(
      functools.partial(_sample_attn_kernel, n_seq=n_seq, n_grp=n_grp),
      grid_spec=grid_spec,
      out_shape=jax.ShapeDtypeStruct((n_seq, t_new, w), BF16),
      compiler_params=pltpu.CompilerParams(
          dimension_semantics=("arbitrary", "arbitrary"), vmem_limit_bytes=VMEM_LIMIT),
      name="sample_attention",
  )(page_table, qb, k_new, v_new, blast, bnew, lam4, subln_g, cache_k, cache_v)


def _ssd_kernel(xbc_ref, z_ref, dt_ref, pre_ref, h0_ref, cw_ref, cb_ref, dtb_ref, alog_ref,
                dsk_ref, ng_ref, e_ref, tri_ref,
                ssd_ref, conv_ref, h_ref, xp_sc, *, tv, nc):
  L = SSD_CHUNK
  hp = SUBLANES
  b = pl.program_id(0)
  c = pl.program_id(1)

  @pl.when((b == 0) & (c == 0))
  def _():
    xp_sc[...] = jnp.zeros(xp_sc.shape, F32)

  @pl.when(c == 0)
  def _():
    xp_sc[hp - (CONV_W - 1):hp, :] = pre_ref[0]
    h_ref[0] = h0_ref[0]

  xp_sc[hp:hp + tv, :] = xbc_ref[0]
  cw = cw_ref[...]
  y = cb_ref[...]
  cur = xp_sc[hp:hp + L, :]
  hist = xp_sc[0:hp, :]
  head_row = lax.broadcasted_iota(jnp.int32, (hp, CONV_DIM), 0)
  for i in range(CONV_W):
    back = CONV_W - 1 - i
    if back == 0:
      tap = cur
    else:
      rolled = pltpu.roll(cur, back, 0)
      head = jnp.where(head_row < back, pltpu.roll(hist, back, 0), rolled[0:hp])
      tap = jnp.concatenate([head, rolled[hp:]], axis=0)
    y = y + tap * cw[i:i + 1]
  xa = _silu(y)

  @pl.when(c == nc - 1)
  def _():
    conv_ref[0] = xp_sc[hp + tv - (CONV_W - 1):hp + tv, :]

  xp_sc[0:hp, :] = xp_sc[tv:tv + hp, :]

  xs = xa[:, :SSD_WIDTH]
  gw = SSD_GROUPS * D_STATE
  bmat = xa[:, SSD_WIDTH:SSD_WIDTH + gw].astype(BF16)
  cmat = xa[:, SSD_WIDTH + gw:].astype(BF16)

  x = dt_ref[0] + dtb_ref[...]
  dtv = jnp.maximum(x, 0.0) + jnp.log1p(jnp.exp(-jnp.abs(x)))
  if tv < L:
    dtv = jnp.concatenate([dtv, jnp.zeros((L - tv, LANES), F32)], axis=0)
  a = dtv * (-jnp.exp(alog_ref[...]))

  tri = tri_ref[...]
  acs = sum(_dot(tri, part) for part in _split3(a))
  e = e_ref[...]
  aexp = sum(_dot(part, e) for part in _split3(acs))
  dtexp = _dot(dtv.astype(BF16), e)
  xd = xs * dtexp
  a_last = acs[L - 1:L, :]
  xdd_t = (xd * jnp.exp(aexp[L - 1:L, :] - aexp)).T
  acs_t = acs.T

  ri = lax.broadcasted_iota(jnp.int32, (L, L), 0)
  ci = lax.broadcasted_iota(jnp.int32, (L, L), 1)
  causal = ri >= ci
  lane = lax.broadcasted_iota(jnp.int32, (L, LANES), 1)
  heads_per_group = SSD_HEADS // SSD_GROUPS
  hprev = h_ref[0]
  hb = hprev.astype(BF16)

  ydiag, yoff, upd = [], [], []
  for g in range(SSD_GROUPS):
    cg = cmat[:, g * D_STATE:(g + 1) * D_STATE]
    bg = bmat[:, g * D_STATE:(g + 1) * D_STATE]
    scores = _dot_nt(cg, bg)
    for pair in range(heads_per_group // 2):
      h0 = g * heads_per_group + 2 * pair
      ms = []
      for h in (h0, h0 + 1):
        col = jnp.broadcast_to(acs[:, h:h + 1], (L, L))
        row = acs_t[h:h + 1, :]
        lm = jnp.exp(jnp.where(causal, col - row, NEG))
        ms.append((scores * lm).astype(BF16))
      xpair = xd[:, h0 * SSD_HEAD_DIM:(h0 + 2) * SSD_HEAD_DIM]
      xbd = jnp.concatenate([jnp.where(lane < SSD_HEAD_DIM, xpair, 0.0),
                             jnp.where(lane >= SSD_HEAD_DIM, xpair, 0.0)], axis=0)
      ydiag.append(_dot(jnp.concatenate(ms, axis=1), xbd.astype(BF16)))
    rows = slice(g * heads_per_group * SSD_HEAD_DIM, (g + 1) * heads_per_group * SSD_HEAD_DIM)
    yoff.append(_dot_nt(cg, hb[rows, :]))
    upd.append(_dot(xdd_t[rows, :].astype(BF16), bg))

  y = (jnp.concatenate(ydiag, axis=1) + jnp.concatenate(yoff, axis=1) * jnp.exp(aexp)
       + dsk_ref[...] * xs)

  decayed = []
  chunk_decay = jnp.exp(a_last)
  for h in range(SSD_HEADS):
    dec = jnp.broadcast_to(chunk_decay[:, h:h + 1], (SSD_HEAD_DIM, D_STATE))
    decayed.append(hprev[h * SSD_HEAD_DIM:(h + 1) * SSD_HEAD_DIM, :] * dec)
  h_ref[0] = jnp.concatenate(decayed, axis=0) + jnp.concatenate(upd, axis=0)

  yv = y[:tv] * _silu(z_ref[0])
  sq = yv * yv
  half = SSD_WIDTH // SSD_GROUPS
  r0 = lax.rsqrt(jnp.mean(sq[:, :half], axis=-1, keepdims=True) + EPS)
  r1 = lax.rsqrt(jnp.mean(sq[:, half:], axis=-1, keepdims=True) + EPS)
  out = jnp.concatenate([yv[:, :half] * r0, yv[:, half:] * r1], axis=1) * ng_ref[...]
  ssd_ref[0] = out.astype(ssd_ref.dtype)


def _ssd(xbc, z, dt, prefix, h0, conv_w, conv_b, dtb_p, alog_p, dsk_e, norm_g, tv):
  nb, tb, _ = xbc.shape
  nc = tb // tv
  e_np = np.zeros((LANES, SSD_WIDTH), np.float32)
  for h in range(SSD_HEADS):
    e_np[h, h * SSD_HEAD_DIM:(h + 1) * SSD_HEAD_DIM] = 1.0
  tri_np = np.tril(np.ones((SSD_CHUNK, SSD_CHUNK), np.float32))
  full = lambda shape: pl.BlockSpec(shape, lambda b, c: (0,) * len(shape))
  tok = lambda w: pl.BlockSpec((1, tv, w), lambda b, c: (b, c, 0))
  per_b = lambda r, w: pl.BlockSpec((1, r, w), lambda b, c: (b, 0, 0))
  return pl.pallas_call(
      functools.partial(_ssd_kernel, tv=tv, nc=nc),
      grid=(nb, nc),
      in_specs=[tok(CONV_DIM), tok(SSD_WIDTH), tok(LANES),
                per_b(CONV_W - 1, CONV_DIM), per_b(SSD_WIDTH, D_STATE),
                full((CONV_W, CONV_DIM)), full((1, CONV_DIM)), full((1, LANES)), full((1, LANES)),
                full((1, SSD_WIDTH)), full((1, SSD_WIDTH)),
                full((LANES, SSD_WIDTH)), full((SSD_CHUNK, SSD_CHUNK))],
      out_specs=[tok(SSD_WIDTH), per_b(CONV_W - 1, CONV_DIM), per_b(SSD_WIDTH, D_STATE)],
      out_shape=[jax.ShapeDtypeStruct((nb, tb, SSD_WIDTH), BF16),
                 jax.ShapeDtypeStruct((nb, CONV_W - 1, CONV_DIM), F32),
                 jax.ShapeDtypeStruct((nb, SSD_WIDTH, D_STATE), F32)],
      scratch_shapes=[pltpu.VMEM((SSD_CHUNK + SUBLANES, CONV_DIM), F32)],
      compiler_params=pltpu.CompilerParams(
          dimension_semantics=("arbitrary", "arbitrary"), vmem_limit_bytes=VMEM_LIMIT),
      name="ssd_scan",
  )(xbc, z, dt, prefix, h0, conv_w, conv_b, dtb_p, alog_p, dsk_e, norm_g,
    jnp.asarray(e_np, BF16), jnp.asarray(tri_np, BF16))


def _ffn_kernel(x_ref, att_ref, ssd_ref, ga1_ref, sc2_ref, sh2_ref, ga2_ref, gf_ref, gl_ref,
                wo_ref, wu_ref, wd_ref, y_ref):
  x = x_ref[0]
  mix = _dot(att_ref[0], wo_ref[0:ATT_WIDTH, :]) + _dot(ssd_ref[0], wo_ref[ATT_WIDTH:, :])
  x1 = x + ga1_ref[0] * mix
  var = jnp.mean(x1 * x1, axis=-1, keepdims=True)
  h2 = x1 * lax.rsqrt(var + EPS) * gf_ref[...]
  h2 = (h2 * (1.0 + sc2_ref[0]) + sh2_ref[0]).astype(BF16)
  fc = D_MODEL
  acc = jnp.zeros(x.shape, F32)
  for f in range(D_FF // fc):
    u = jnp.maximum(_dot(h2, wu_ref[:, f * fc:(f + 1) * fc]), 0.0)
    acc = acc + _dot((u * u).astype(BF16), wd_ref[f * fc:(f + 1) * fc, :])
  x2 = x1 + ga2_ref[0] * acc
  var2 = jnp.mean(x2 * x2, axis=-1, keepdims=True)
  y_ref[0] = x2 * lax.rsqrt(var2 + EPS) * gl_ref[...]


def _out_ffn(x3, att, ssd, ga1, sc2, sh2, ga2, g_ffn, g_final, w_out_b, w_up_b, w_down_b, tm):
  nb, tb, d = x3.shape
  nt = tb // tm
  per_row = ga1.shape[1] != 1
  if per_row:
    mspec = pl.BlockSpec((1, tm, d), lambda b, i: (b, i, 0))
  else:
    mspec = pl.BlockSpec((1, 1, d), lambda b, i: (b, 0, 0))
  tok = lambda w: pl.BlockSpec((1, tm, w), lambda b, i: (b, i, 0))
  const = lambda shape: pl.BlockSpec(shape, lambda b, i: (0, 0), pipeline_mode=pl.Buffered(1))
  return pl.pallas_call(
      _ffn_kernel,
      grid=(nb, nt),
      in_specs=[tok(d), tok(ATT_WIDTH), tok(SSD_WIDTH), mspec, mspec, mspec, mspec,
                const((1, d)), const((1, d)),
                const((d, d)), const((d, D_FF)), const((D_FF, d))],
      out_specs=tok(d),
      out_shape=jax.ShapeDtypeStruct((nb, tb, d), F32),
      compiler_params=pltpu.CompilerParams(
          dimension_semantics=("arbitrary", "arbitrary"), vmem_limit_bytes=VMEM_LIMIT),
      name="out_ffn",
  )(x3, att, ssd, ga1, sc2, sh2, ga2, g_ffn.reshape(1, d), g_final.reshape(1, d),
    w_out_b, w_up_b, w_down_b)


def kernel(x_prompt, x_sample, cache_k, cache_v, state_conv, state_ssm, page_table, c_prompt,
           c_sample, rel_bias, w_ada, b_ada, g_mix, g_ffn, w_in, w_out, lam_q1, lam_k1, lam_q2,
           lam_k2, subln_g, conv_w, conv_b, dt_bias, a_log, d_skip, ssd_norm_g, w_up, w_down,
           g_final):
  assert w_ada.shape[0] == 1, "single-layer step"
  bp, sp, d = x_prompt.shape
  bs, ts, _ = x_sample.shape
  n_pool = cache_k.shape[1]

  w_in_p = jnp.pad(w_in[0], ((0, 0), (0, IN_PAD - IN_WIDTH))).astype(BF16)
  w_kt = w_in[0][:, ATT_WIDTH:2 * ATT_WIDTH].T.astype(BF16)
  w_out_b = w_out[0].astype(BF16)
  w_up_b = w_up[0].astype(BF16)
  w_down_b = w_down[0].astype(BF16)
  lam4 = jnp.concatenate([lam_q1, lam_k1, lam_q2, lam_k2], axis=0)
  subln = subln_g.reshape(1, ATT_V_DIM)
  dtb_p = jnp.pad(dt_bias, ((0, 0), (0, LANES - SSD_HEADS)))
  alog_p = jnp.pad(a_log, ((0, 0), (0, LANES - SSD_HEADS)))
  dsk_e = jnp.repeat(d_skip[0], SSD_HEAD_DIM).reshape(1, SSD_WIDTH)
  ssd_args = (conv_w[0], conv_b, dtb_p, alog_p, dsk_e, ssd_norm_g)

  mods = _modulation(jnp.concatenate([c_prompt, c_sample], axis=0), w_ada[0], b_ada[0])
  mp = [m.reshape(bp, 1, d) for m in jnp.split(mods[:bp], 6, axis=-1)]
  ms = [jnp.repeat(m, ts, axis=0).reshape(1, bs * ts, d) for m in jnp.split(mods[bp:], 6, axis=-1)]
  near, diag, slast, snew = _bias_tiles(rel_bias)

  q_hm, kt_hm, v_hm, kt_p, v_p, z_p, xbc_p, dt_p = _in_proj(
      x_prompt, mp[1], mp[0], g_mix[0], w_in_p, w_kt, tm=512, prompt=True)
  att_p = _prompt_attention(q_hm, kt_hm, v_hm, near, diag, lam4, subln)
  ssd_p, conv_p, h_p = _ssd(
      xbc_p, z_p, dt_p, jnp.zeros((bp, CONV_W - 1, CONV_DIM), F32),
      jnp.zeros((bp, SSD_WIDTH, D_STATE), F32), *ssd_args, tv=SSD_CHUNK)
  y_p = _out_ffn(x_prompt, att_p, ssd_p, mp[2], mp[4], mp[3], mp[5], g_ffn[0], g_final,
                 w_out_b, w_up_b, w_down_b, tm=512)

  n_tok = bs * ts
  xs3 = x_sample.reshape(1, n_tok, d)
  q_s, k_s, v_s, z_s, xbc_s, dt_s = _in_proj(
      xs3, ms[1], ms[0], g_mix[0], w_in_p, w_kt, tm=n_tok, prompt=False)
  qt = q_s.reshape(ATT_HEADS, bs, ts, 2, ATT_HEAD_DIM).transpose(1, 0, 3, 2, 4)
  qt = qt.reshape(bs, 2 * ATT_HEADS, ts, ATT_HEAD_DIM)
  eye = jnp.eye(2 * ATT_HEADS, dtype=BF16)
  qb = (qt[:, :, :, None, :] * eye[None, :, None, :, None]).reshape(bs, 2 * ATT_HEADS * ts, ATT_WIDTH)
  blast = jnp.repeat(slast, 2, axis=0).reshape(2 * ATT_HEADS * SUBLANES, LANES)
  bnew = jnp.repeat(snew, 2, axis=0).reshape(2 * ATT_HEADS * SUBLANES, LANES)
  att_s = _sample_attention(
      page_table, qb, k_s.reshape(bs, ts, ATT_WIDTH), v_s.reshape(bs, ts, ATT_WIDTH),
      blast, bnew, lam4, subln,
      jnp.transpose(cache_k[0], (0, 2, 3, 4, 1)).reshape(n_pool, ATT_WIDTH, PAGE_SIZE),
      cache_v[0].reshape(n_pool, PAGE_SIZE * ATT_HEADS, ATT_V_DIM))
  ssd_s, conv_s, h_s = _ssd(
      xbc_s.reshape(bs, ts, CONV_DIM), z_s.reshape(bs, ts, SSD_WIDTH), dt_s.reshape(bs, ts, LANES),
      state_conv[0], state_ssm[0].reshape(bs, SSD_WIDTH, D_STATE), *ssd_args, tv=ts)
  y_s = _out_ffn(xs3, att_s.reshape(1, n_tok, ATT_WIDTH), ssd_s.reshape(1, n_tok, SSD_WIDTH),
                 ms[2], ms[4], ms[3], ms[5], g_ffn[0], g_final, w_out_b, w_up_b, w_down_b, tm=n_tok)

  hshape = (SSD_HEADS, SSD_HEAD_DIM, D_STATE)
  return (y_p, y_s.reshape(bs, ts, d),
          kt_p.reshape(1, bp, ATT_HEADS, 2, ATT_HEAD_DIM, sp).transpose(0, 1, 5, 2, 3, 4),
          v_p.reshape(1, bp, sp, ATT_HEADS, ATT_V_DIM),
          conv_p[None], h_p.reshape(1, bp, *hshape),
          k_s.reshape(1, bs, ts, ATT_HEADS, 2, ATT_HEAD_DIM),
          v_s.reshape(1, bs, ts, ATT_HEADS, ATT_V_DIM),
          conv_s[None], h_s.reshape(1, bs, *hshape))
```

```python
import functools
import math

import numpy as np
import jax
import jax.numpy as jnp
from jax import lax
from jax.experimental import pallas as pl
from jax.experimental.pallas import tpu as pltpu

F32 = jnp.float32
BF16 = jnp.bfloat16

D_MODEL = 1024
PAGE_SIZE = 128
ATT_HEADS = 4
ATT_HEAD_DIM = 64
ATT_V_DIM = 2 * ATT_HEAD_DIM
ATT_WIDTH = ATT_HEADS * ATT_V_DIM
SSD_HEADS = 8
SSD_HEAD_DIM = 64
SSD_WIDTH = SSD_HEADS * SSD_HEAD_DIM
SSD_GROUPS = 2
D_STATE = 128
CONV_W = 4
CONV_DIM = SSD_WIDTH + 2 * SSD_GROUPS * D_STATE
SSD_CHUNK = 128
IN_WIDTH = 3 * ATT_WIDTH + SSD_WIDTH + CONV_DIM + SSD_HEADS
D_FF = 4 * D_MODEL
N_BUCKETS = 32
MAX_DISTANCE = 128
MAX_EXACT = N_BUCKETS // 2
EPS = 1e-6
LAM_INIT = 0.8 - 0.6 * math.exp(-0.3 * 0)

LANES = 128
SUBLANES = 8
IN_PAD = 3 * ATT_WIDTH + SSD_WIDTH + CONV_DIM + LANES
LOG2E = 1.4426950408889634
NEG = -1e30
VMEM_LIMIT = 56 * 1024 * 1024

ATT_T = 256
FFN_TM = 256
PAGES_PER_GROUP = 8
GROUPS_PER_STEP = 8
CACHE_SLOTS = 4


def _bucket_lower_bounds():
  d = np.arange(0, 4 * MAX_DISTANCE)
  nf = np.maximum(d, 1).astype(np.float64)
  large = MAX_EXACT + (np.log(nf / MAX_EXACT) / math.log(MAX_DISTANCE / MAX_EXACT)
                       * (N_BUCKETS - MAX_EXACT)).astype(np.int64)
  large = np.minimum(large, N_BUCKETS - 1)
  bucket = np.where(d < MAX_EXACT, d, large)
  return [int(np.argmax(bucket >= b)) for b in range(N_BUCKETS)]


_LOWER = _bucket_lower_bounds()
FAR_DIST = _LOWER[N_BUCKETS - 1]
assert FAR_DIST <= PAGE_SIZE and FAR_DIST <= ATT_T


def _silu(x):
  h = 0.5 * x
  return h * jnp.tanh(h) + h


def _dot(a, b):
  return jnp.dot(a, b, preferred_element_type=F32)


def _dot_nt(a, b):
  return lax.dot_general(a, b, (((1,), (1,)), ((), ())), preferred_element_type=F32)


def _split3(x):
  hi = x.astype(BF16)
  r1 = x - hi.astype(F32)
  mid = r1.astype(BF16)
  lo = (r1 - mid.astype(F32)).astype(BF16)
  return hi, mid, lo


def _mod_kernel(c_ref, w_ref, b_ref, o_ref):
  s = _silu(c_ref[...]).astype(BF16)
  o_ref[...] = _dot(s, w_ref[...].astype(BF16)) + b_ref[...]


def _modulation(c_all, w_ada, b_ada):
  n = c_all.shape[0]
  tn = D_MODEL
  return pl.pallas_call(
      _mod_kernel,
      grid=(6 * D_MODEL // tn,),
      in_specs=[pl.BlockSpec((n, D_MODEL), lambda j: (0, 0)),
                pl.BlockSpec((D_MODEL, tn), lambda j: (0, j)),
                pl.BlockSpec((1, tn), lambda j: (0, j))],
      out_specs=pl.BlockSpec((n, tn), lambda j: (0, j)),
      out_shape=jax.ShapeDtypeStruct((n, 6 * D_MODEL), F32),
      compiler_params=pltpu.CompilerParams(vmem_limit_bytes=VMEM_LIMIT),
      name="modulation",
  )(c_all, w_ada, b_ada.reshape(1, -1))


def _bias_kernel(tab_ref, near_ref, diag_ref, slast_ref, snew_ref):
  def bias_of(dist, h):
    val = jnp.full(dist.shape, tab_ref[0, h], F32)
    for b in range(1, N_BUCKETS):
      val = jnp.where(dist >= _LOWER[b], tab_ref[b, h], val)
    return (val - tab_ref[N_BUCKETS - 1, h]) * LOG2E

  t = ATT_T
  r = lax.broadcasted_iota(jnp.int32, (t, t), 0)
  c = lax.broadcasted_iota(jnp.int32, (t, t), 1)
  r8 = lax.broadcasted_iota(jnp.int32, (SUBLANES, LANES), 0)
  c8 = lax.broadcasted_iota(jnp.int32, (SUBLANES, LANES), 1)
  for h in range(ATT_HEADS):
    near_ref[h] = bias_of(t + r - c, h)
    d = r - c
    diag_ref[h] = jnp.where(d >= 0, bias_of(d, h), NEG)
    slast_ref[h] = bias_of(PAGE_SIZE + r8 - c8, h)
    dn = r8 - c8
    snew_ref[h] = jnp.where(dn >= 0, bias_of(dn, h), NEG)


def _bias_tiles(rel_bias):
  t = ATT_T
  return pl.pallas_call(
      _bias_kernel,
      in_specs=[pl.BlockSpec(memory_space=pltpu.SMEM)],
      out_shape=(jax.ShapeDtypeStruct((ATT_HEADS, t, t), F32),
                 jax.ShapeDtypeStruct((ATT_HEADS, t, t), F32),
                 jax.ShapeDtypeStruct((ATT_HEADS, SUBLANES, LANES), F32),
                 jax.ShapeDtypeStruct((ATT_HEADS, SUBLANES, LANES), F32)),
      compiler_params=pltpu.CompilerParams(vmem_limit_bytes=VMEM_LIMIT),
      name="bias_tiles",
  )(rel_bias)


def _inproj_kernel(x_ref, sc_ref, sh_ref, g_ref, w_ref, wkt_ref, *out_refs, prompt):
  x = x_ref[0]
  var = jnp.mean(x * x, axis=-1, keepdims=True)
  h = x * lax.rsqrt(var + EPS) * g_ref[...]
  h = h * (1.0 + sc_ref[0]) + sh_ref[0]
  hb = h.astype(BF16)
  aw = ATT_WIDTH
  q = _dot(hb, w_ref[:, 0:aw]) * (ATT_HEAD_DIM ** -0.5 * LOG2E)
  v = _dot(hb, w_ref[:, 2 * aw:3 * aw])
  if prompt:
    q_ref, ktb_ref, vb_ref, kt_ref, v_ref, z_ref, xbc_ref, dt_ref = out_refs
    kt = _dot_nt(wkt_ref[...], hb)
    kt_ref[0] = kt
    t = ATT_T
    for hd in range(ATT_HEADS):
      sl = slice(hd * ATT_V_DIM, (hd + 1) * ATT_V_DIM)
      vb_ref[0, hd] = v[:, sl].astype(BF16)
      v_ref[0, pl.ds(hd, v.shape[0], stride=ATT_HEADS), :] = v[:, sl]
      for cc in range(kt.shape[1] // t):
        ktb_ref[0, hd, cc] = kt[sl, cc * t:(cc + 1) * t].astype(BF16)
  else:
    q_ref, k_ref, v_ref, z_ref, xbc_ref, dt_ref = out_refs
    k_ref[0] = _dot(hb, w_ref[:, aw:2 * aw])
    v_ref[0] = v
  for hd in range(ATT_HEADS):
    sl = slice(hd * ATT_V_DIM, (hd + 1) * ATT_V_DIM)
    q_ref[0, hd] = q[:, sl].astype(BF16)
  o = 3 * aw
  z_ref[0] = _dot(hb, w_ref[:, o:o + SSD_WIDTH])
  o += SSD_WIDTH
  xbc_ref[0] = _dot(hb, w_ref[:, o:o + CONV_DIM])
  o += CONV_DIM
  dt_ref[0] = _dot(hb, w_ref[:, o:o + LANES])


def _in_proj(x3, sc, sh, g_mix, w_in_p, w_kt, tm, prompt):
  nb, tb, d = x3.shape
  nt = tb // tm
  per_row = sc.shape[1] != 1
  if per_row:
    mspec = pl.BlockSpec((1, tm, d), lambda b, i: (b, i, 0))
  else:
    mspec = pl.BlockSpec((1, 1, d), lambda b, i: (b, 0, 0))
  hm = pl.BlockSpec((1, ATT_HEADS, tm, ATT_V_DIM), lambda b, i: (b, 0, i, 0))
  tok = lambda w: pl.BlockSpec((1, tm, w), lambda b, i: (b, i, 0))
  hm_shape = jax.ShapeDtypeStruct((nb, ATT_HEADS, tb, ATT_V_DIM), BF16)
  tok_shape = lambda w: jax.ShapeDtypeStruct((nb, tb, w), F32)
  tail_specs = [tok(SSD_WIDTH), tok(CONV_DIM), tok(LANES)]
  tail_shapes = [tok_shape(SSD_WIDTH), tok_shape(CONV_DIM), tok_shape(LANES)]
  if prompt:
    t = ATT_T
    out_specs = [hm,
                 pl.BlockSpec((1, ATT_HEADS, tm // t, ATT_V_DIM, t), lambda b, i: (b, 0, i, 0, 0)),
                 hm,
                 pl.BlockSpec((1, ATT_WIDTH, tm), lambda b, i: (b, 0, i)),
                 pl.BlockSpec((1, tm * ATT_HEADS, ATT_V_DIM), lambda b, i: (b, i, 0))] + tail_specs
    out_shape = [hm_shape,
                 jax.ShapeDtypeStruct((nb, ATT_HEADS, tb // t, ATT_V_DIM, t), BF16),
                 hm_shape,
                 jax.ShapeDtypeStruct((nb, ATT_WIDTH, tb), F32),
                 jax.ShapeDtypeStruct((nb, tb * ATT_HEADS, ATT_V_DIM), F32)] + tail_shapes
  else:
    out_specs = [hm, tok(ATT_WIDTH), tok(ATT_WIDTH)] + tail_specs
    out_shape = [hm_shape, tok_shape(ATT_WIDTH), tok_shape(ATT_WIDTH)] + tail_shapes
  return pl.pallas_call(
      functools.partial(_inproj_kernel, prompt=prompt),
      grid=(nb, nt),
      in_specs=[tok(d), mspec, mspec,
                pl.BlockSpec((1, d), lambda b, i: (0, 0)),
                pl.BlockSpec((d, IN_PAD), lambda b, i: (0, 0)),
                pl.BlockSpec((ATT_WIDTH, d), lambda b, i: (0, 0))],
      out_specs=out_specs,
      out_shape=out_shape,
      compiler_params=pltpu.CompilerParams(
          dimension_semantics=("arbitrary", "arbitrary"), vmem_limit_bytes=VMEM_LIMIT),
      name="in_proj",
  )(x3, sc, sh, g_mix.reshape(1, d), w_in_p, w_kt)


def _diff_lambda(lam_ref):
  lv = lam_ref[...]
  s1 = jnp.sum(lv[0:1] * lv[1:2], axis=-1, keepdims=True)
  s2 = jnp.sum(lv[2:3] * lv[3:4], axis=-1, keepdims=True)
  return jnp.exp(s1) - jnp.exp(s2) + LAM_INIT


def _diff_combine(o0, o1, lam, g):
  d = o0 - lam * o1
  d = d * lax.rsqrt(jnp.mean(d * d, axis=-1, keepdims=True) + EPS) * g
  return d * (1.0 - LAM_INIT)


def _attn_kernel(q_ref, k_ref, v_ref, near_ref, diag_ref, lam_ref, g_ref, o_ref,
                 m_sc, acc_sc):
  t = ATT_T
  qi = pl.program_id(1)
  lane = lax.broadcasted_iota(jnp.int32, (t, ATT_V_DIM), 1)
  m_sc[...] = jnp.full(m_sc.shape, NEG, F32)
  acc_sc[...] = jnp.zeros(acc_sc.shape, F32)
  ones = jnp.ones((t, LANES), BF16)

  def head_step(h, c, bias_ref):
    q = q_ref[0, h]
    zero = jnp.zeros_like(q)
    q2 = jnp.concatenate([jnp.where(lane < ATT_HEAD_DIM, q, zero),
                          jnp.where(lane >= ATT_HEAD_DIM, q, zero)], axis=0)
    start = pl.multiple_of(c * t, t)
    kc = k_ref[0, h, c]
    vc = v_ref[0, h, pl.ds(start, t), :]
    s = _dot(q2, kc)
    if bias_ref is not None:
      bias = bias_ref[h]
      s = s + jnp.concatenate([bias, bias], axis=0)
    m_prev = m_sc[h]
    m_new = jnp.maximum(m_prev, jnp.max(s, axis=-1, keepdims=True))
    alpha = jnp.exp2(m_prev - m_new)
    p = jnp.exp2(s - jnp.concatenate([m_new] * (t // LANES), axis=1))
    pv = _dot(p.astype(BF16), jnp.concatenate([vc, ones], axis=1))
    acc_sc[h] = acc_sc[h] * jnp.concatenate([alpha, alpha], axis=1) + pv
    m_sc[h] = m_new

  def step(c, bias_ref):
    for h in range(ATT_HEADS):
      head_step(h, c, bias_ref)

  def far_body(c, carry):
    step(c, None)
    return carry

  lax.fori_loop(0, jnp.maximum(qi - 1, 0), far_body, 0)

  @pl.when(qi >= 1)
  def _():
    step(qi - 1, near_ref)

  step(qi, diag_ref)

  lam = _diff_lambda(lam_ref)
  for h in range(ATT_HEADS):
    acc = acc_sc[h]
    o = acc[:, :ATT_V_DIM] / acc[:, ATT_V_DIM:]
    d = _diff_combine(o[:t], o[t:], lam, g_ref[...])
    o_ref[0, :, h * ATT_V_DIM:(h + 1) * ATT_V_DIM] = d.astype(o_ref.dtype)


def _prompt_attention(q_hm, kt_hm, v_hm, near, diag, lam4, subln_g):
  b, nh, s, e = q_hm.shape
  t = ATT_T
  nq = s // t
  return pl.pallas_call(
      _attn_kernel,
      grid=(b, nq),
      in_specs=[pl.BlockSpec((1, nh, t, e), lambda bi, qi: (bi, 0, qi, 0)),
                pl.BlockSpec((1, nh, nq, e, t), lambda bi, qi: (bi, 0, 0, 0, 0)),
                pl.BlockSpec((1, nh, s, e), lambda bi, qi: (bi, 0, 0, 0)),
                pl.BlockSpec((nh, t, t), lambda bi, qi: (0, 0, 0)),
                pl.BlockSpec((nh, t, t), lambda bi, qi: (0, 0, 0)),
                pl.BlockSpec((4, ATT_HEAD_DIM), lambda bi, qi: (0, 0)),
                pl.BlockSpec((1, e), lambda bi, qi: (0, 0))],
      out_specs=pl.BlockSpec((1, t, nh * e), lambda bi, qi: (bi, qi, 0)),
      out_shape=jax.ShapeDtypeStruct((b, s, nh * e), BF16),
      scratch_shapes=[pltpu.VMEM((nh, 2 * t, LANES), F32),
                      pltpu.VMEM((nh, 2 * t, 2 * LANES), F32)],
      compiler_params=pltpu.CompilerParams(
          dimension_semantics=("arbitrary", "arbitrary"), vmem_limit_bytes=VMEM_LIMIT),
      name="prompt_attention",
  )(q_hm, kt_hm, v_hm, near, diag, lam4, subln_g)


def _ssd_kernel(xbc_ref, z_ref, dt_ref, pre_ref, h0_ref, cw_ref, cb_ref, dtb_ref, alog_ref,
                dsk_ref, ng_ref, e_ref, tri_ref,
                ssd_ref, conv_ref, h_ref, xp_sc, *, tv, nc):
  L = SSD_CHUNK
  hp = SUBLANES
  b = pl.program_id(0)
  c = pl.program_id(1)

  @pl.when((b == 0) & (c == 0))
  def _():
    xp_sc[...] = jnp.zeros(xp_sc.shape, F32)

  @pl.when(c == 0)
  def _():
    xp_sc[hp - (CONV_W - 1):hp, :] = pre_ref[0]
    h_ref[0] = h0_ref[0]

  xp_sc[hp:hp + tv, :] = xbc_ref[0]
  cw = cw_ref[...]
  y = cb_ref[...]
  cur = xp_sc[hp:hp + L, :]
  hist = xp_sc[0:hp, :]
  head_row = lax.broadcasted_iota(jnp.int32, (hp, CONV_DIM), 0)
  for i in range(CONV_W):
    back = CONV_W - 1 - i
    if back == 0:
      tap = cur
    else:
      rolled = pltpu.roll(cur, back, 0)
      head = jnp.where(head_row < back, pltpu.roll(hist, back, 0), rolled[0:hp])
      tap = jnp.concatenate([head, rolled[hp:]], axis=0)
    y = y + tap * cw[i:i + 1]
  xa = _silu(y)

  @pl.when(c == nc - 1)
  def _():
    conv_ref[0] = xp_sc[hp + tv - (CONV_W - 1):hp + tv, :]

  xp_sc[0:hp, :] = xp_sc[tv:tv + hp, :]

  xs = xa[:, :SSD_WIDTH]
  gw = SSD_GROUPS * D_STATE
  bmat = xa[:, SSD_WIDTH:SSD_WIDTH + gw].astype(BF16)
  cmat = xa[:, SSD_WIDTH + gw:].astype(BF16)

  x = dt_ref[0] + dtb_ref[...]
  dtv = jnp.maximum(x, 0.0) + jnp.log1p(jnp.exp(-jnp.abs(x)))
  if tv < L:
    dtv = jnp.concatenate([dtv, jnp.zeros((L - tv, LANES), F32)], axis=0)
  a = dtv * (-jnp.exp(alog_ref[...]))

  tri = tri_ref[...]
  acs = sum(_dot(tri, part) for part in _split3(a))
  e = e_ref[...]
  aexp = sum(_dot(part, e) for part in _split3(acs))
  dtexp = _dot(dtv.astype(BF16), e)
  xd = xs * dtexp
  a_last = acs[L - 1:L, :]
  xdd_t = (xd * jnp.exp(aexp[L - 1:L, :] - aexp)).T
  acs_t = acs.T

  ri = lax.broadcasted_iota(jnp.int32, (L, L), 0)
  ci = lax.broadcasted_iota(jnp.int32, (L, L), 1)
  causal = ri >= ci
  lane = lax.broadcasted_iota(jnp.int32, (L, LANES), 1)
  heads_per_group = SSD_HEADS // SSD_GROUPS
  hprev = h_ref[0]
  hb = hprev.astype(BF16)

  ydiag, yoff, upd = [], [], []
  for g in range(SSD_GROUPS):
    cg = cmat[:, g * D_STATE:(g + 1) * D_STATE]
    bg = bmat[:, g * D_STATE:(g + 1) * D_STATE]
    scores = _dot_nt(cg, bg)
    for pair in range(heads_per_group // 2):
      h0 = g * heads_per_group + 2 * pair
      ms = []
      for h in (h0, h0 + 1):
        col = jnp.broadcast_to(acs[:, h:h + 1], (L, L))
        row = acs_t[h:h + 1, :]
        lm = jnp.exp(jnp.where(causal, col - row, NEG))
        ms.append((scores * lm).astype(BF16))
      xpair = xd[:, h0 * SSD_HEAD_DIM:(h0 + 2) * SSD_HEAD_DIM]
      xbd = jnp.concatenate([jnp.where(lane < SSD_HEAD_DIM, xpair, 0.0),
                             jnp.where(lane >= SSD_HEAD_DIM, xpair, 0.0)], axis=0)
      ydiag.append(_dot(jnp.concatenate(ms, axis=1), xbd.astype(BF16)))
    rows = slice(g * heads_per_group * SSD_HEAD_DIM, (g + 1) * heads_per_group * SSD_HEAD_DIM)
    yoff.append(_dot_nt(cg, hb[rows, :]))
    upd.append(_dot(xdd_t[rows, :].astype(BF16), bg))

  y = (jnp.concatenate(ydiag, axis=1) + jnp.concatenate(yoff, axis=1) * jnp.exp(aexp)
       + dsk_ref[...] * xs)

  decayed = []
  chunk_decay = jnp.exp(a_last)
  for h in range(SSD_HEADS):
    dec = jnp.broadcast_to(chunk_decay[:, h:h + 1], (SSD_HEAD_DIM, D_STATE))
    decayed.append(hprev[h * SSD_HEAD_DIM:(h + 1) * SSD_HEAD_DIM, :] * dec)
  h_ref[0] = jnp.concatenate(decayed, axis=0) + jnp.concatenate(upd, axis=0)

  yv = y[:tv] * _silu(z_ref[0])
  sq = yv * yv
  half = SSD_WIDTH // SSD_GROUPS
  r0 = lax.rsqrt(jnp.mean(sq[:, :half], axis=-1, keepdims=True) + EPS)
  r1 = lax.rsqrt(jnp.mean(sq[:, half:], axis=-1, keepdims=True) + EPS)
  out = jnp.concatenate([yv[:, :half] * r0, yv[:, half:] * r1], axis=1) * ng_ref[...]
  ssd_ref[0] = out.astype(ssd_ref.dtype)


def _ssd(xbc, z, dt, prefix, h0, conv_w, conv_b, dtb_p, alog_p, dsk_e, norm_g, tv):
  nb, tb, _ = xbc.shape
  nc = tb // tv
  e_np = np.zeros((LANES, SSD_WIDTH), np.float32)
  for h in range(SSD_HEADS):
    e_np[h, h * SSD_HEAD_DIM:(h + 1) * SSD_HEAD_DIM] = 1.0
  tri_np = np.tril(np.ones((SSD_CHUNK, SSD_CHUNK), np.float32))
  full = lambda shape: pl.BlockSpec(shape, lambda b, c: (0,) * len(shape))
  tok = lambda w: pl.BlockSpec((1, tv, w), lambda b, c: (b, c, 0))
  per_b = lambda r, w: pl.BlockSpec((1, r, w), lambda b, c: (b, 0, 0))
  return pl.pallas_call(
      functools.partial(_ssd_kernel, tv=tv, nc=nc),
      grid=(nb, nc),
      in_specs=[tok(CONV_DIM), tok(SSD_WIDTH), tok(LANES),
                per_b(CONV_W - 1, CONV_DIM), per_b(SSD_WIDTH, D_STATE),
                full((CONV_W, CONV_DIM)), full((1, CONV_DIM)), full((1, LANES)), full((1, LANES)),
                full((1, SSD_WIDTH)), full((1, SSD_WIDTH)),
                full((LANES, SSD_WIDTH)), full((SSD_CHUNK, SSD_CHUNK))],
      out_specs=[tok(SSD_WIDTH), per_b(CONV_W - 1, CONV_DIM), per_b(SSD_WIDTH, D_STATE)],
      out_shape=[jax.ShapeDtypeStruct((nb, tb, SSD_WIDTH), BF16),
                 jax.ShapeDtypeStruct((nb, CONV_W - 1, CONV_DIM), F32),
                 jax.ShapeDtypeStruct((nb, SSD_WIDTH, D_STATE), F32)],
      scratch_shapes=[pltpu.VMEM((SSD_CHUNK + SUBLANES, CONV_DIM), F32)],
      compiler_params=pltpu.CompilerParams(
          dimension_semantics=("arbitrary", "arbitrary"), vmem_limit_bytes=VMEM_LIMIT),
      name="ssd_scan",
  )(xbc, z, dt, prefix, h0, conv_w, conv_b, dtb_p, alog_p, dsk_e, norm_g,
    jnp.asarray(e_np, BF16), jnp.asarray(tri_np, BF16))


def _ffn_pre(x_ref, att_ref, ssd_ref, ga1_ref, sc2_ref, sh2_ref, gf_ref, wo_ref):
  mix = _dot(att_ref[0], wo_ref[0:ATT_WIDTH, :]) + _dot(ssd_ref[0], wo_ref[ATT_WIDTH:, :])
  x1 = x_ref[0] + ga1_ref[0] * mix
  var = jnp.mean(x1 * x1, axis=-1, keepdims=True)
  h2 = x1 * lax.rsqrt(var + EPS) * gf_ref[...]
  return x1, (h2 * (1.0 + sc2_ref[0]) + sh2_ref[0]).astype(BF16)


def _ffn_piece(h2, wu_ref, wd_ref, lo, hi):
  u = jnp.maximum(_dot(h2, wu_ref[:, lo:hi]), 0.0)
  return _dot((u * u).astype(BF16), wd_ref[lo:hi, :])


def _ffn_post(x1, acc, ga2_ref, gl_ref):
  x2 = x1 + ga2_ref[0] * acc
  var2 = jnp.mean(x2 * x2, axis=-1, keepdims=True)
  return x2 * lax.rsqrt(var2 + EPS) * gl_ref[...]


def _ffn_kernel(x_ref, att_ref, ssd_ref, ga1_ref, sc2_ref, sh2_ref, ga2_ref, gf_ref, gl_ref,
                wo_ref, wu_ref, wd_ref, y_ref):
  x1, h2 = _ffn_pre(x_ref, att_ref, ssd_ref, ga1_ref, sc2_ref, sh2_ref, gf_ref, wo_ref)
  fc = D_MODEL
  acc = jnp.zeros(x1.shape, F32)
  for f in range(D_FF // fc):
    acc = acc + _ffn_piece(h2, wu_ref, wd_ref, f * fc, (f + 1) * fc)
  y_ref[0] = _ffn_post(x1, acc, ga2_ref, gl_ref)


def _out_ffn(x3, att, ssd, ga1, sc2, sh2, ga2, g_ffn, g_final, w_out_b, w_up_b, w_down_b, tm):
  nb, tb, d = x3.shape
  nt = tb // tm
  per_row = ga1.shape[1] != 1
  if per_row:
    mspec = pl.BlockSpec((1, tm, d), lambda b, i: (b, i, 0))
  else:
    mspec = pl.BlockSpec((1, 1, d), lambda b, i: (b, 0, 0))
  tok = lambda w: pl.BlockSpec((1, tm, w), lambda b, i: (b, i, 0))
  const = lambda shape: pl.BlockSpec(shape, lambda b, i: (0, 0), pipeline_mode=pl.Buffered(1))
  return pl.pallas_call(
      _ffn_kernel,
      grid=(nb, nt),
      in_specs=[tok(d), tok(ATT_WIDTH), tok(SSD_WIDTH), mspec, mspec, mspec, mspec,
                const((1, d)), const((1, d)),
                const((d, d)), const((d, D_FF)), const((D_FF, d))],
      out_specs=tok(d),
      out_shape=jax.ShapeDtypeStruct((nb, tb, d), F32),
      compiler_params=pltpu.CompilerParams(
          dimension_semantics=("arbitrary", "arbitrary"), vmem_limit_bytes=VMEM_LIMIT),
      name="out_ffn",
  )(x3, att, ssd, ga1, sc2, sh2, ga2, g_ffn.reshape(1, d), g_final.reshape(1, d),
    w_out_b, w_up_b, w_down_b)


def _ffn_sample_kernel(pt_ref, x_ref, att_ref, ssd_ref, ga1_ref, sc2_ref, sh2_ref, ga2_ref,
                       gf_ref, gl_ref, wo_ref, wu_ref, wd_ref,
                       qb_ref, kn_ref, vn_ref, blast_ref, bnew_ref, lam_ref, g_ref, ck_hbm, cv_hbm,
                       y_ref, o_ref,
                       kbuf, vbuf, sem, m_sc, l_sc, acc_sc, kpad, vpad, *, n_steps, steps_per_seq):
  i = pl.program_id(0)
  seq = i // steps_per_seq
  part = i % steps_per_seq
  gps = GROUPS_PER_STEP
  n_slots = kbuf.shape[0]
  ahead = n_slots - 1
  last_part = part == steps_per_seq - 1

  def page_copies(sq, grp, sl):
    copies = []
    for p in range(PAGES_PER_GROUP):
      pg = pt_ref[sq, grp * PAGES_PER_GROUP + p]
      copies.append((pltpu.make_async_copy(ck_hbm.at[pg], kbuf.at[sl, p], sem.at[0, sl]), 0))
      copies.append((pltpu.make_async_copy(cv_hbm.at[pg], vbuf.at[sl, p], sem.at[1, sl]), 1))
    return copies

  def group_copies(j):
    if j < gps:
      return page_copies(seq, part * gps + j, j % n_slots)
    nxt = i + 1
    return page_copies(nxt // steps_per_seq, (nxt % steps_per_seq) * gps + j - gps, j % n_slots)

  def start_group(j):
    for cp, prio in group_copies(j):
      cp.start(priority=prio)

  @pl.when(i == 0)
  def _():
    kpad[...] = jnp.zeros(kpad.shape, kpad.dtype)
    vpad[...] = jnp.zeros(vpad.shape, vpad.dtype)
    for j in range(ahead):
      start_group(j)

  @pl.when(part == 0)
  def _():
    m_sc[...] = jnp.full(m_sc.shape, NEG, F32)
    l_sc[...] = jnp.zeros(l_sc.shape, F32)
    acc_sc[...] = jnp.zeros(acc_sc.shape, F32)

  qb = qb_ref[0]
  rows_per_head = 2 * SUBLANES

  def update(s, values_of_head):
    m_prev = m_sc[...]
    m_new = jnp.maximum(m_prev, jnp.max(s, axis=-1, keepdims=True))
    alpha = jnp.exp2(m_prev - m_new)
    p = jnp.exp2(s - m_new[:, 0:1])
    l_sc[...] = l_sc[...] * alpha + jnp.sum(p, axis=-1, keepdims=True)
    pb = p.astype(BF16)
    pv = [_dot(pb[h * rows_per_head:(h + 1) * rows_per_head, :], values_of_head(h))
          for h in range(ATT_HEADS)]
    acc_sc[...] = acc_sc[...] * alpha + jnp.concatenate(pv, axis=0)
    m_sc[...] = m_new

  def cache_group(j):
    sl = j % n_slots
    s = [_dot(qb, kbuf[sl, p].astype(BF16)) for p in range(PAGES_PER_GROUP)]
    if j == gps - 1:
      s[-1] = s[-1] + jnp.where(last_part, blast_ref[...], 0.0)

    def values_of_head(h):
      v = [vbuf[sl, p, pl.ds(h, PAGE_SIZE, stride=ATT_HEADS), :] for p in range(PAGES_PER_GROUP)]
      return jnp.concatenate(v, axis=0).astype(BF16)

    update(jnp.concatenate(s, axis=1), values_of_head)

  x1, h2 = _ffn_pre(x_ref, att_ref, ssd_ref, ga1_ref, sc2_ref, sh2_ref, gf_ref, wo_ref)
  fc = D_FF // gps
  acc = jnp.zeros(x1.shape, F32)
  for j in range(gps):
    if j + ahead < gps:
      start_group(j + ahead)
    else:
      pl.when(i + 1 < n_steps)(functools.partial(start_group, j + ahead))
    for cp, _ in group_copies(j):
      cp.wait()
    cache_group(j)
    acc = acc + _ffn_piece(h2, wu_ref, wd_ref, j * fc, (j + 1) * fc)
  y_ref[0] = _ffn_post(x1, acc, ga2_ref, gl_ref)

  @pl.when(last_part)
  def _():
    t_new = kn_ref.shape[1]
    kpad[0:t_new, :] = kn_ref[0].astype(BF16)
    vpad[0:t_new, :] = vn_ref[0].astype(BF16)
    sn = _dot_nt(qb, kpad[...]) + bnew_ref[...]
    update(sn, lambda h: vpad[:, h * ATT_V_DIM:(h + 1) * ATT_V_DIM])
    o = acc_sc[...] / l_sc[...]
    lam = _diff_lambda(lam_ref)
    for h in range(ATT_HEADS):
      o0 = o[(2 * h) * SUBLANES:(2 * h + 1) * SUBLANES, :]
      o1 = o[(2 * h + 1) * SUBLANES:(2 * h + 2) * SUBLANES, :]
      o_ref[0, :, h * ATT_V_DIM:(h + 1) * ATT_V_DIM] = _diff_combine(
          o0, o1, lam, g_ref[...]).astype(o_ref.dtype)


def _ffn_with_sample_attention(x_p, att_p, ssd_p, ga1, sc2, sh2, ga2, g_ffn, g_final,
                               w_out_b, w_up_b, w_down_b,
                               page_table, qb, k_new, v_new, blast, bnew, lam4, subln_g,
                               cache_k, cache_v):
  nb, tb, d = x_p.shape
  tm = FFN_TM
  tiles_per_b = tb // tm
  n_steps = nb * tiles_per_b
  n_seq, n_pages = page_table.shape
  t_new = k_new.shape[1]
  pages_per_step = PAGES_PER_GROUP * GROUPS_PER_STEP
  steps_per_seq = n_pages // pages_per_step
  assert t_new == SUBLANES and n_pages % pages_per_step == 0
  assert n_seq * steps_per_seq == n_steps, "one page-group batch per FFN tile"
  assert GROUPS_PER_STEP % CACHE_SLOTS == 0, "ring slots must be static per group"
  w = ATT_WIDTH
  assert cache_k.shape[1:] == (w, PAGE_SIZE) and cache_v.shape[1:] == (PAGE_SIZE * ATT_HEADS, ATT_V_DIM)
  tok = lambda width: pl.BlockSpec((1, tm, width),
                                   lambda i, pt: (i // tiles_per_b, i % tiles_per_b, 0))
  mspec = pl.BlockSpec((1, 1, d), lambda i, pt: (i // tiles_per_b, 0, 0))
  const = lambda shape: pl.BlockSpec(shape, lambda i, pt: (0,) * len(shape),
                                     pipeline_mode=pl.Buffered(1))
  per_seq = lambda r: pl.BlockSpec((1, r, w), lambda i, pt: (i // steps_per_seq, 0, 0))
  grid_spec = pltpu.PrefetchScalarGridSpec(
      num_scalar_prefetch=1,
      grid=(n_steps,),
      in_specs=[tok(d), tok(ATT_WIDTH), tok(SSD_WIDTH), mspec, mspec, mspec, mspec,
                const((1, d)), const((1, d)),
                const((d, d)), const((d, D_FF)), const((D_FF, d)),
                per_seq(8 * SUBLANES), per_seq(t_new), per_seq(t_new),
                const((8 * SUBLANES, LANES)), const((8 * SUBLANES, LANES)),
                const((4, ATT_HEAD_DIM)), const((1, ATT_V_DIM)),
                pl.BlockSpec(memory_space=pl.ANY),
                pl.BlockSpec(memory_space=pl.ANY)],
      out_specs=[tok(d), per_seq(t_new)],
      scratch_shapes=[pltpu.VMEM((CACHE_SLOTS, PAGES_PER_GROUP) + cache_k.shape[1:], F32),
                      pltpu.VMEM((CACHE_SLOTS, PAGES_PER_GROUP) + cache_v.shape[1:], F32),
                      pltpu.SemaphoreType.DMA((2, CACHE_SLOTS)),
                      pltpu.VMEM((8 * SUBLANES, LANES), F32),
                      pltpu.VMEM((8 * SUBLANES, LANES), F32),
                      pltpu.VMEM((8 * SUBLANES, ATT_V_DIM), F32),
                      pltpu.VMEM((PAGE_SIZE, w), BF16),
                      pltpu.VMEM((PAGE_SIZE, w), BF16)])
  return pl.pallas_call(
      functools.partial(_ffn_sample_kernel, n_steps=n_steps, steps_per_seq=steps_per_seq),
      grid_spec=grid_spec,
      out_shape=[jax.ShapeDtypeStruct((nb, tb, d), F32),
                 jax.ShapeDtypeStruct((n_seq, t_new, w), BF16)],
      compiler_params=pltpu.CompilerParams(
          dimension_semantics=("arbitrary",), vmem_limit_bytes=VMEM_LIMIT),
      name="ffn_sample_attention",
  )(page_table, x_p, att_p, ssd_p, ga1, sc2, sh2, ga2, g_ffn.reshape(1, d), g_final.reshape(1, d),
    w_out_b, w_up_b, w_down_b, qb, k_new, v_new, blast, bnew, lam4, subln_g, cache_k, cache_v)


def kernel(x_prompt, x_sample, cache_k, cache_v, state_conv, state_ssm, page_table, c_prompt,
           c_sample, rel_bias, w_ada, b_ada, g_mix, g_ffn, w_in, w_out, lam_q1, lam_k1, lam_q2,
           lam_k2, subln_g, conv_w, conv_b, dt_bias, a_log, d_skip, ssd_norm_g, w_up, w_down,
           g_final):
  assert w_ada.shape[0] == 1, "single-layer step"
  bp, sp, d = x_prompt.shape
  bs, ts, _ = x_sample.shape
  n_pool = cache_k.shape[1]

  w_in_p = jnp.pad(w_in[0], ((0, 0), (0, IN_PAD - IN_WIDTH))).astype(BF16)
  w_kt = w_in[0][:, ATT_WIDTH:2 * ATT_WIDTH].T.astype(BF16)
  w_out_b = w_out[0].astype(BF16)
  w_up_b = w_up[0].astype(BF16)
  w_down_b = w_down[0].astype(BF16)
  lam4 = jnp.concatenate([lam_q1, lam_k1, lam_q2, lam_k2], axis=0)
  subln = subln_g.reshape(1, ATT_V_DIM)
  dtb_p = jnp.pad(dt_bias, ((0, 0), (0, LANES - SSD_HEADS)))
  alog_p = jnp.pad(a_log, ((0, 0), (0, LANES - SSD_HEADS)))
  dsk_e = jnp.repeat(d_skip[0], SSD_HEAD_DIM).reshape(1, SSD_WIDTH)
  ssd_args = (conv_w[0], conv_b, dtb_p, alog_p, dsk_e, ssd_norm_g)

  mods = _modulation(jnp.concatenate([c_prompt, c_sample], axis=0), w_ada[0], b_ada[0])
  mp = [m.reshape(bp, 1, d) for m in jnp.split(mods[:bp], 6, axis=-1)]
  ms = [jnp.repeat(m, ts, axis=0).reshape(1, bs * ts, d) for m in jnp.split(mods[bp:], 6, axis=-1)]
  near, diag, slast, snew = _bias_tiles(rel_bias)

  q_hm, kt_hm, v_hm, kt_p, v_p, z_p, xbc_p, dt_p = _in_proj(
      x_prompt, mp[1], mp[0], g_mix[0], w_in_p, w_kt, tm=512, prompt=True)
  att_p = _prompt_attention(q_hm, kt_hm, v_hm, near, diag, lam4, subln)
  ssd_p, conv_p, h_p = _ssd(
      xbc_p, z_p, dt_p, jnp.zeros((bp, CONV_W - 1, CONV_DIM), F32),
      jnp.zeros((bp, SSD_WIDTH, D_STATE), F32), *ssd_args, tv=SSD_CHUNK)

  n_tok = bs * ts
  xs3 = x_sample.reshape(1, n_tok, d)
  q_s, k_s, v_s, z_s, xbc_s, dt_s = _in_proj(
      xs3, ms[1], ms[0], g_mix[0], w_in_p, w_kt, tm=n_tok, prompt=False)
  qt = q_s.reshape(ATT_HEADS, bs, ts, 2, ATT_HEAD_DIM).transpose(1, 0, 3, 2, 4)
  qt = qt.reshape(bs, 2 * ATT_HEADS, ts, ATT_HEAD_DIM)
  eye = jnp.eye(2 * ATT_HEADS, dtype=BF16)
  qb = (qt[:, :, :, None, :] * eye[None, :, None, :, None]).reshape(bs, 2 * ATT_HEADS * ts, ATT_WIDTH)
  blast = jnp.repeat(slast, 2, axis=0).reshape(2 * ATT_HEADS * SUBLANES, LANES)
  bnew = jnp.repeat(snew, 2, axis=0).reshape(2 * ATT_HEADS * SUBLANES, LANES)
  y_p, att_s = _ffn_with_sample_attention(
      x_prompt, att_p, ssd_p, mp[2], mp[4], mp[3], mp[5], g_ffn[0], g_final,
      w_out_b, w_up_b, w_down_b,
      page_table, qb, k_s.reshape(bs, ts, ATT_WIDTH), v_s.reshape(bs, ts, ATT_WIDTH),
      blast, bnew, lam4, subln,
      jnp.transpose(cache_k[0], (0, 2, 3, 4, 1)).reshape(n_pool, ATT_WIDTH, PAGE_SIZE),
      cache_v[0].reshape(n_pool, PAGE_SIZE * ATT_HEADS, ATT_V_DIM))
  ssd_s, conv_s, h_s = _ssd(
      xbc_s.reshape(bs, ts, CONV_DIM), z_s.reshape(bs, ts, SSD_WIDTH), dt_s.reshape(bs, ts, LANES),
      state_conv[0], state_ssm[0].reshape(bs, SSD_WIDTH, D_STATE), *ssd_args, tv=ts)
  y_s = _out_ffn(xs3, att_s.reshape(1, n_tok, ATT_WIDTH), ssd_s.reshape(1, n_tok, SSD_WIDTH),
                 ms[2], ms[4], ms[3], ms[5], g_ffn[0], g_final, w_out_b, w_up_b, w_down_b, tm=n_tok)

  hshape = (SSD_HEADS, SSD_HEAD_DIM, D_STATE)
  return (y_p, y_s.reshape(bs, ts, d),
          kt_p.reshape(1, bp, ATT_HEADS, 2, ATT_HEAD_DIM, sp).transpose(0, 1, 5, 2, 3, 4),
          v_p.reshape(1, bp, sp, ATT_HEADS, ATT_V_DIM),
          conv_p[None], h_p.reshape(1, bp, *hshape),
          k_s.reshape(1, bs, ts, ATT_HEADS, 2, ATT_HEAD_DIM),
          v_s.reshape(1, bs, ts, ATT_HEADS, ATT_V_DIM),
          conv_s[None], h_s.reshape(1, bs, *hshape))
```

```python
import functools
import math

import numpy as np
import jax
import jax.numpy as jnp
from jax import lax
from jax.experimental import pallas as pl
from jax.experimental.pallas import tpu as pltpu

F32 = jnp.float32
BF16 = jnp.bfloat16

D_MODEL = 1024
PAGE_SIZE = 128
ATT_HEADS = 4
ATT_HEAD_DIM = 64
ATT_V_DIM = 2 * ATT_HEAD_DIM
ATT_WIDTH = ATT_HEADS * ATT_V_DIM
SSD_HEADS = 8
SSD_HEAD_DIM = 64
SSD_WIDTH = SSD_HEADS * SSD_HEAD_DIM
SSD_GROUPS = 2
D_STATE = 128
CONV_W = 4
CONV_DIM = SSD_WIDTH + 2 * SSD_GROUPS * D_STATE
SSD_CHUNK = 128
IN_WIDTH = 3 * ATT_WIDTH + SSD_WIDTH + CONV_DIM + SSD_HEADS
D_FF = 4 * D_MODEL
N_BUCKETS = 32
MAX_DISTANCE = 128
MAX_EXACT = N_BUCKETS // 2
EPS = 1e-6
LAM_INIT = 0.8 - 0.6 * math.exp(-0.3 * 0)

LANES = 128
SUBLANES = 8
IN_PAD = 3 * ATT_WIDTH + SSD_WIDTH + CONV_DIM + LANES
LOG2E = 1.4426950408889634
NEG = -1e30
VMEM_LIMIT = 56 * 1024 * 1024

ATT_T = 256
FFN_TM = 256
PAGES_PER_GROUP = 8
SOFTMAX_CHAINS = 1
ROUNDS_PER_STEP = 8
ROUND_SLOTS = 4


def _bucket_lower_bounds():
  d = np.arange(0, 4 * MAX_DISTANCE)
  nf = np.maximum(d, 1).astype(np.float64)
  large = MAX_EXACT + (np.log(nf / MAX_EXACT) / math.log(MAX_DISTANCE / MAX_EXACT)
                       * (N_BUCKETS - MAX_EXACT)).astype(np.int64)
  large = np.minimum(large, N_BUCKETS - 1)
  bucket = np.where(d < MAX_EXACT, d, large)
  return [int(np.argmax(bucket >= b)) for b in range(N_BUCKETS)]


_LOWER = _bucket_lower_bounds()
FAR_DIST = _LOWER[N_BUCKETS - 1]
assert FAR_DIST <= PAGE_SIZE and FAR_DIST <= ATT_T


def _silu(x):
  h = 0.5 * x
  return h * jnp.tanh(h) + h


def _dot(a, b):
  return jnp.dot(a, b, preferred_element_type=F32)


def _dot_nt(a, b):
  return lax.dot_general(a, b, (((1,), (1,)), ((), ())), preferred_element_type=F32)


def _split3(x):
  hi = x.astype(BF16)
  r1 = x - hi.astype(F32)
  mid = r1.astype(BF16)
  lo = (r1 - mid.astype(F32)).astype(BF16)
  return hi, mid, lo


def _mod_kernel(c_ref, w_ref, b_ref, o_ref):
  s = _silu(c_ref[...]).astype(BF16)
  o_ref[...] = _dot(s, w_ref[...].astype(BF16)) + b_ref[...]


def _modulation(c_all, w_ada, b_ada):
  n = c_all.shape[0]
  tn = D_MODEL
  return pl.pallas_call(
      _mod_kernel,
      grid=(6 * D_MODEL // tn,),
      in_specs=[pl.BlockSpec((n, D_MODEL), lambda j: (0, 0)),
                pl.BlockSpec((D_MODEL, tn), lambda j: (0, j)),
                pl.BlockSpec((1, tn), lambda j: (0, j))],
      out_specs=pl.BlockSpec((n, tn), lambda j: (0, j)),
      out_shape=jax.ShapeDtypeStruct((n, 6 * D_MODEL), F32),
      compiler_params=pltpu.CompilerParams(vmem_limit_bytes=VMEM_LIMIT),
      name="modulation",
  )(c_all, w_ada, b_ada.reshape(1, -1))


def _bias_kernel(tab_ref, near_ref, diag_ref, slast_ref, snew_ref):
  def bias_of(dist, h):
    val = jnp.full(dist.shape, tab_ref[0, h], F32)
    for b in range(1, N_BUCKETS):
      val = jnp.where(dist >= _LOWER[b], tab_ref[b, h], val)
    return (val - tab_ref[N_BUCKETS - 1, h]) * LOG2E

  t = ATT_T
  r = lax.broadcasted_iota(jnp.int32, (t, t), 0)
  c = lax.broadcasted_iota(jnp.int32, (t, t), 1)
  r8 = lax.broadcasted_iota(jnp.int32, (SUBLANES, LANES), 0)
  c8 = lax.broadcasted_iota(jnp.int32, (SUBLANES, LANES), 1)
  for h in range(ATT_HEADS):
    near_ref[h] = bias_of(t + r - c, h)
    d = r - c
    diag_ref[h] = jnp.where(d >= 0, bias_of(d, h), NEG)
    slast_ref[h] = bias_of(PAGE_SIZE + r8 - c8, h)
    dn = r8 - c8
    snew_ref[h] = jnp.where(dn >= 0, bias_of(dn, h), NEG)


def _bias_tiles(rel_bias):
  t = ATT_T
  return pl.pallas_call(
      _bias_kernel,
      in_specs=[pl.BlockSpec(memory_space=pltpu.SMEM)],
      out_shape=(jax.ShapeDtypeStruct((ATT_HEADS, t, t), F32),
                 jax.ShapeDtypeStruct((ATT_HEADS, t, t), F32),
                 jax.ShapeDtypeStruct((ATT_HEADS, SUBLANES, LANES), F32),
                 jax.ShapeDtypeStruct((ATT_HEADS, SUBLANES, LANES), F32)),
      compiler_params=pltpu.CompilerParams(vmem_limit_bytes=VMEM_LIMIT),
      name="bias_tiles",
  )(rel_bias)


def _inproj_kernel(x_ref, sc_ref, sh_ref, g_ref, w_ref, wkt_ref, *out_refs, prompt):
  x = x_ref[0]
  var = jnp.mean(x * x, axis=-1, keepdims=True)
  h = x * lax.rsqrt(var + EPS) * g_ref[...]
  h = h * (1.0 + sc_ref[0]) + sh_ref[0]
  hb = h.astype(BF16)
  aw = ATT_WIDTH
  q = _dot(hb, w_ref[:, 0:aw]) * (ATT_HEAD_DIM ** -0.5 * LOG2E)
  v = _dot(hb, w_ref[:, 2 * aw:3 * aw])
  if prompt:
    q_ref, ktb_ref, vb_ref, kt_ref, v_ref, z_ref, xbc_ref, dt_ref = out_refs
    kt = _dot_nt(wkt_ref[...], hb)
    kt_ref[0] = kt
    t = ATT_T
    for hd in range(ATT_HEADS):
      sl = slice(hd * ATT_V_DIM, (hd + 1) * ATT_V_DIM)
      vb_ref[0, hd] = v[:, sl].astype(BF16)
      v_ref[0, pl.ds(hd, v.shape[0], stride=ATT_HEADS), :] = v[:, sl]
      for cc in range(kt.shape[1] // t):
        ktb_ref[0, hd, cc] = kt[sl, cc * t:(cc + 1) * t].astype(BF16)
  else:
    q_ref, k_ref, v_ref, z_ref, xbc_ref, dt_ref = out_refs
    k_ref[0] = _dot(hb, w_ref[:, aw:2 * aw])
    v_ref[0] = v
  for hd in range(ATT_HEADS):
    sl = slice(hd * ATT_V_DIM, (hd + 1) * ATT_V_DIM)
    q_ref[0, hd] = q[:, sl].astype(BF16)
  o = 3 * aw
  z_ref[0] = _dot(hb, w_ref[:, o:o + SSD_WIDTH])
  o += SSD_WIDTH
  xbc_ref[0] = _dot(hb, w_ref[:, o:o + CONV_DIM])
  o += CONV_DIM
  dt_ref[0] = _dot(hb, w_ref[:, o:o + LANES])


def _in_proj(x3, sc, sh, g_mix, w_in_p, w_kt, tm, prompt):
  nb, tb, d = x3.shape
  nt = tb // tm
  per_row = sc.shape[1] != 1
  if per_row:
    mspec = pl.BlockSpec((1, tm, d), lambda b, i: (b, i, 0))
  else:
    mspec = pl.BlockSpec((1, 1, d), lambda b, i: (b, 0, 0))
  hm = pl.BlockSpec((1, ATT_HEADS, tm, ATT_V_DIM), lambda b, i: (b, 0, i, 0))
  tok = lambda w: pl.BlockSpec((1, tm, w), lambda b, i: (b, i, 0))
  hm_shape = jax.ShapeDtypeStruct((nb, ATT_HEADS, tb, ATT_V_DIM), BF16)
  tok_shape = lambda w: jax.ShapeDtypeStruct((nb, tb, w), F32)
  tail_specs = [tok(SSD_WIDTH), tok(CONV_DIM), tok(LANES)]
  tail_shapes = [tok_shape(SSD_WIDTH), tok_shape(CONV_DIM), tok_shape(LANES)]
  if prompt:
    t = ATT_T
    out_specs = [hm,
                 pl.BlockSpec((1, ATT_HEADS, tm // t, ATT_V_DIM, t), lambda b, i: (b, 0, i, 0, 0)),
                 hm,
                 pl.BlockSpec((1, ATT_WIDTH, tm), lambda b, i: (b, 0, i)),
                 pl.BlockSpec((1, tm * ATT_HEADS, ATT_V_DIM), lambda b, i: (b, i, 0))] + tail_specs
    out_shape = [hm_shape,
                 jax.ShapeDtypeStruct((nb, ATT_HEADS, tb // t, ATT_V_DIM, t), BF16),
                 hm_shape,
                 jax.ShapeDtypeStruct((nb, ATT_WIDTH, tb), F32),
                 jax.ShapeDtypeStruct((nb, tb * ATT_HEADS, ATT_V_DIM), F32)] + tail_shapes
  else:
    out_specs = [hm, tok(ATT_WIDTH), tok(ATT_WIDTH)] + tail_specs
    out_shape = [hm_shape, tok_shape(ATT_WIDTH), tok_shape(ATT_WIDTH)] + tail_shapes
  return pl.pallas_call(
      functools.partial(_inproj_kernel, prompt=prompt),
      grid=(nb, nt),
      in_specs=[tok(d), mspec, mspec,
                pl.BlockSpec((1, d), lambda b, i: (0, 0)),
                pl.BlockSpec((d, IN_PAD), lambda b, i: (0, 0)),
                pl.BlockSpec((ATT_WIDTH, d), lambda b, i: (0, 0))],
      out_specs=out_specs,
      out_shape=out_shape,
      compiler_params=pltpu.CompilerParams(
          dimension_semantics=("arbitrary", "arbitrary"), vmem_limit_bytes=VMEM_LIMIT),
      name="in_proj",
  )(x3, sc, sh, g_mix.reshape(1, d), w_in_p, w_kt)


def _diff_lambda(lam_ref):
  lv = lam_ref[...]
  s1 = jnp.sum(lv[0:1] * lv[1:2], axis=-1, keepdims=True)
  s2 = jnp.sum(lv[2:3] * lv[3:4], axis=-1, keepdims=True)
  return jnp.exp(s1) - jnp.exp(s2) + LAM_INIT


def _diff_combine(o0, o1, lam, g):
  d = o0 - lam * o1
  d = d * lax.rsqrt(jnp.mean(d * d, axis=-1, keepdims=True) + EPS) * g
  return d * (1.0 - LAM_INIT)


def _attn_kernel(q_ref, k_ref, v_ref, near_ref, diag_ref, lam_ref, g_ref, o_ref,
                 m_sc, acc_sc):
  t = ATT_T
  qi = pl.program_id(1)
  lane = lax.broadcasted_iota(jnp.int32, (t, ATT_V_DIM), 1)
  m_sc[...] = jnp.full(m_sc.shape, NEG, F32)
  acc_sc[...] = jnp.zeros(acc_sc.shape, F32)
  ones = jnp.ones((t, LANES), BF16)

  def head_step(h, c, bias_ref):
    q = q_ref[0, h]
    zero = jnp.zeros_like(q)
    q2 = jnp.concatenate([jnp.where(lane < ATT_HEAD_DIM, q, zero),
                          jnp.where(lane >= ATT_HEAD_DIM, q, zero)], axis=0)
    start = pl.multiple_of(c * t, t)
    kc = k_ref[0, h, c]
    vc = v_ref[0, h, pl.ds(start, t), :]
    s = _dot(q2, kc)
    if bias_ref is not None:
      bias = bias_ref[h]
      s = s + jnp.concatenate([bias, bias], axis=0)
    m_prev = m_sc[h]
    m_new = jnp.maximum(m_prev, jnp.max(s, axis=-1, keepdims=True))
    alpha = jnp.exp2(m_prev - m_new)
    p = jnp.exp2(s - jnp.concatenate([m_new] * (t // LANES), axis=1))
    pv = _dot(p.astype(BF16), jnp.concatenate([vc, ones], axis=1))
    acc_sc[h] = acc_sc[h] * jnp.concatenate([alpha, alpha], axis=1) + pv
    m_sc[h] = m_new

  def step(c, bias_ref):
    for h in range(ATT_HEADS):
      head_step(h, c, bias_ref)

  def far_body(c, carry):
    step(c, None)
    return carry

  lax.fori_loop(0, jnp.maximum(qi - 1, 0), far_body, 0)

  @pl.when(qi >= 1)
  def _():
    step(qi - 1, near_ref)

  step(qi, diag_ref)

  lam = _diff_lambda(lam_ref)
  for h in range(ATT_HEADS):
    acc = acc_sc[h]
    o = acc[:, :ATT_V_DIM] / acc[:, ATT_V_DIM:]
    d = _diff_combine(o[:t], o[t:], lam, g_ref[...])
    o_ref[0, :, h * ATT_V_DIM:(h + 1) * ATT_V_DIM] = d.astype(o_ref.dtype)


def _prompt_attention(q_hm, kt_hm, v_hm, near, diag, lam4, subln_g):
  b, nh, s, e = q_hm.shape
  t = ATT_T
  nq = s // t
  return pl.pallas_call(
      _attn_kernel,
      grid=(b, nq),
      in_specs=[pl.BlockSpec((1, nh, t, e), lambda bi, qi: (bi, 0, qi, 0)),
                pl.BlockSpec((1, nh, nq, e, t), lambda bi, qi: (bi, 0, 0, 0, 0)),
                pl.BlockSpec((1, nh, s, e), lambda bi, qi: (bi, 0, 0, 0)),
                pl.BlockSpec((nh, t, t), lambda bi, qi: (0, 0, 0)),
                pl.BlockSpec((nh, t, t), lambda bi, qi: (0, 0, 0)),
                pl.BlockSpec((4, ATT_HEAD_DIM), lambda bi, qi: (0, 0)),
                pl.BlockSpec((1, e), lambda bi, qi: (0, 0))],
      out_specs=pl.BlockSpec((1, t, nh * e), lambda bi, qi: (bi, qi, 0)),
      out_shape=jax.ShapeDtypeStruct((b, s, nh * e), BF16),
      scratch_shapes=[pltpu.VMEM((nh, 2 * t, LANES), F32),
                      pltpu.VMEM((nh, 2 * t, 2 * LANES), F32)],
      compiler_params=pltpu.CompilerParams(
          dimension_semantics=("arbitrary", "arbitrary"), vmem_limit_bytes=VMEM_LIMIT),
      name="prompt_attention",
  )(q_hm, kt_hm, v_hm, near, diag, lam4, subln_g)


def _ssd_kernel(xbc_ref, z_ref, dt_ref, pre_ref, h0_ref, *refs, tv, nc):
  consts, (ssd_ref, conv_ref, h_ref, xp_sc) = refs[:8], refs[8:]
  hp = SUBLANES
  b = pl.program_id(0)
  c = pl.program_id(1)
  n_seq = xbc_ref.shape[0]

  @pl.when((b == 0) & (c == 0))
  def _():
    xp_sc[...] = jnp.zeros(xp_sc.shape, F32)

  @pl.when(c == 0)
  def _():
    for bb in range(n_seq):
      xp_sc[bb, hp - (CONV_W - 1):hp, :] = pre_ref[bb]
      h_ref[bb] = h0_ref[bb]

  for bb in range(n_seq):
    _ssd_chunk(xbc_ref.at[bb], z_ref.at[bb], dt_ref.at[bb], *consts,
               ssd_ref.at[bb], h_ref.at[bb], xp_sc.at[bb], tv=tv)

  @pl.when(c == nc - 1)
  def _():
    for bb in range(n_seq):
      conv_ref[bb] = xp_sc[bb, hp + tv - (CONV_W - 1):hp + tv, :]


def _ssd_chunk(xbc_ref, z_ref, dt_ref, cw_ref, cb_ref, dtb_ref, alog_ref, dsk_ref, ng_ref, e_ref,
               tri_ref, ssd_ref, h_ref, xp_sc, *, tv):
  L = SSD_CHUNK
  hp = SUBLANES
  xp_sc[hp:hp + tv, :] = xbc_ref[...]
  cw = cw_ref[...]
  y = cb_ref[...]
  cur = xp_sc[hp:hp + L, :]
  hist = xp_sc[0:hp, :]
  head_row = lax.broadcasted_iota(jnp.int32, (hp, CONV_DIM), 0)
  for i in range(CONV_W):
    back = CONV_W - 1 - i
    if back == 0:
      tap = cur
    else:
      rolled = pltpu.roll(cur, back, 0)
      head = jnp.where(head_row < back, pltpu.roll(hist, back, 0), rolled[0:hp])
      tap = jnp.concatenate([head, rolled[hp:]], axis=0)
    y = y + tap * cw[i:i + 1]
  xa = _silu(y)
  xp_sc[0:hp, :] = xp_sc[tv:tv + hp, :]

  xs = xa[:, :SSD_WIDTH]
  gw = SSD_GROUPS * D_STATE
  bmat = xa[:, SSD_WIDTH:SSD_WIDTH + gw].astype(BF16)
  cmat = xa[:, SSD_WIDTH + gw:].astype(BF16)

  x = dt_ref[...] + dtb_ref[...]
  dtv = jnp.maximum(x, 0.0) + jnp.log1p(jnp.exp(-jnp.abs(x)))
  if tv < L:
    dtv = jnp.concatenate([dtv, jnp.zeros((L - tv, LANES), F32)], axis=0)
  a = dtv * (-jnp.exp(alog_ref[...]))

  tri = tri_ref[...]
  acs = sum(_dot(tri, part) for part in _split3(a))
  e = e_ref[...]
  aexp = sum(_dot(part, e) for part in _split3(acs))
  dtexp = _dot(dtv.astype(BF16), e)
  xd = xs * dtexp
  a_last = acs[L - 1:L, :]
  xdd_t = (xd * jnp.exp(aexp[L - 1:L, :] - aexp)).T
  acs_t = acs.T

  ri = lax.broadcasted_iota(jnp.int32, (L, L), 0)
  ci = lax.broadcasted_iota(jnp.int32, (L, L), 1)
  causal = ri >= ci
  lane = lax.broadcasted_iota(jnp.int32, (L, LANES), 1)
  heads_per_group = SSD_HEADS // SSD_GROUPS
  hprev = h_ref[...]
  hb = hprev.astype(BF16)

  ydiag, yoff, upd = [], [], []
  for g in range(SSD_GROUPS):
    cg = cmat[:, g * D_STATE:(g + 1) * D_STATE]
    bg = bmat[:, g * D_STATE:(g + 1) * D_STATE]
    scores = _dot_nt(cg, bg)
    for pair in range(heads_per_group // 2):
      h0 = g * heads_per_group + 2 * pair
      ms = []
      for h in (h0, h0 + 1):
        col = jnp.broadcast_to(acs[:, h:h + 1], (L, L))
        row = acs_t[h:h + 1, :]
        lm = jnp.exp(jnp.where(causal, col - row, NEG))
        ms.append((scores * lm).astype(BF16))
      xpair = xd[:, h0 * SSD_HEAD_DIM:(h0 + 2) * SSD_HEAD_DIM]
      xbd = jnp.concatenate([jnp.where(lane < SSD_HEAD_DIM, xpair, 0.0),
                             jnp.where(lane >= SSD_HEAD_DIM, xpair, 0.0)], axis=0)
      ydiag.append(_dot(jnp.concatenate(ms, axis=1), xbd.astype(BF16)))
    rows = slice(g * heads_per_group * SSD_HEAD_DIM, (g + 1) * heads_per_group * SSD_HEAD_DIM)
    yoff.append(_dot_nt(cg, hb[rows, :]))
    upd.append(_dot(xdd_t[rows, :].astype(BF16), bg))

  y = (jnp.concatenate(ydiag, axis=1) + jnp.concatenate(yoff, axis=1) * jnp.exp(aexp)
       + dsk_ref[...] * xs)

  decayed = []
  chunk_decay = jnp.exp(a_last)
  for h in range(SSD_HEADS):
    dec = jnp.broadcast_to(chunk_decay[:, h:h + 1], (SSD_HEAD_DIM, D_STATE))
    decayed.append(hprev[h * SSD_HEAD_DIM:(h + 1) * SSD_HEAD_DIM, :] * dec)
  h_ref[...] = jnp.concatenate(decayed, axis=0) + jnp.concatenate(upd, axis=0)

  yv = y[:tv] * _silu(z_ref[...])
  sq = yv * yv
  half = SSD_WIDTH // SSD_GROUPS
  r0 = lax.rsqrt(jnp.mean(sq[:, :half], axis=-1, keepdims=True) + EPS)
  r1 = lax.rsqrt(jnp.mean(sq[:, half:], axis=-1, keepdims=True) + EPS)
  out = jnp.concatenate([yv[:, :half] * r0, yv[:, half:] * r1], axis=1) * ng_ref[...]
  ssd_ref[...] = out.astype(ssd_ref.dtype)


def _ssd(xbc, z, dt, prefix, h0, conv_w, conv_b, dtb_p, alog_p, dsk_e, norm_g, tv, seqs_per_step):
  nb, tb, _ = xbc.shape
  nc = tb // tv
  ns = seqs_per_step
  assert nb % ns == 0
  e_np = np.zeros((LANES, SSD_WIDTH), np.float32)
  for h in range(SSD_HEADS):
    e_np[h, h * SSD_HEAD_DIM:(h + 1) * SSD_HEAD_DIM] = 1.0
  tri_np = np.tril(np.ones((SSD_CHUNK, SSD_CHUNK), np.float32))
  full = lambda shape: pl.BlockSpec(shape, lambda b, c: (0,) * len(shape))
  tok = lambda w: pl.BlockSpec((ns, tv, w), lambda b, c: (b, c, 0))
  per_b = lambda r, w: pl.BlockSpec((ns, r, w), lambda b, c: (b, 0, 0))
  return pl.pallas_call(
      functools.partial(_ssd_kernel, tv=tv, nc=nc),
      grid=(nb // ns, nc),
      in_specs=[tok(CONV_DIM), tok(SSD_WIDTH), tok(LANES),
                per_b(CONV_W - 1, CONV_DIM), per_b(SSD_WIDTH, D_STATE),
                full((CONV_W, CONV_DIM)), full((1, CONV_DIM)), full((1, LANES)), full((1, LANES)),
                full((1, SSD_WIDTH)), full((1, SSD_WIDTH)),
                full((LANES, SSD_WIDTH)), full((SSD_CHUNK, SSD_CHUNK))],
      out_specs=[tok(SSD_WIDTH), per_b(CONV_W - 1, CONV_DIM), per_b(SSD_WIDTH, D_STATE)],
      out_shape=[jax.ShapeDtypeStruct((nb, tb, SSD_WIDTH), BF16),
                 jax.ShapeDtypeStruct((nb, CONV_W - 1, CONV_DIM), F32),
                 jax.ShapeDtypeStruct((nb, SSD_WIDTH, D_STATE), F32)],
      scratch_shapes=[pltpu.VMEM((ns, SSD_CHUNK + SUBLANES, CONV_DIM), F32)],
      compiler_params=pltpu.CompilerParams(
          dimension_semantics=("arbitrary", "arbitrary"), vmem_limit_bytes=VMEM_LIMIT),
      name="ssd_scan",
  )(xbc, z, dt, prefix, h0, conv_w, conv_b, dtb_p, alog_p, dsk_e, norm_g,
    jnp.asarray(e_np, BF16), jnp.asarray(tri_np, BF16))


def _ffn_pre(x_ref, att_ref, ssd_ref, ga1_ref, sc2_ref, sh2_ref, gf_ref, wo_ref):
  mix = _dot(att_ref[0], wo_ref[0:ATT_WIDTH, :]) + _dot(ssd_ref[0], wo_ref[ATT_WIDTH:, :])
  x1 = x_ref[0] + ga1_ref[0] * mix
  var = jnp.mean(x1 * x1, axis=-1, keepdims=True)
  h2 = x1 * lax.rsqrt(var + EPS) * gf_ref[...]
  return x1, (h2 * (1.0 + sc2_ref[0]) + sh2_ref[0]).astype(BF16)


def _ffn_piece(h2, wu_ref, wd_ref, lo, hi):
  u = jnp.maximum(_dot(h2, wu_ref[:, lo:hi]), 0.0)
  return _dot((u * u).astype(BF16), wd_ref[lo:hi, :])


def _ffn_post(x1, acc, ga2_ref, gl_ref):
  x2 = x1 + ga2_ref[0] * acc
  var2 = jnp.mean(x2 * x2, axis=-1, keepdims=True)
  return x2 * lax.rsqrt(var2 + EPS) * gl_ref[...]


def _ffn_kernel(x_ref, att_ref, ssd_ref, ga1_ref, sc2_ref, sh2_ref, ga2_ref, gf_ref, gl_ref,
                wo_ref, wu_ref, wd_ref, y_ref):
  x1, h2 = _ffn_pre(x_ref, att_ref, ssd_ref, ga1_ref, sc2_ref, sh2_ref, gf_ref, wo_ref)
  fc = D_MODEL
  acc = jnp.zeros(x1.shape, F32)
  for f in range(D_FF // fc):
    acc = acc + _ffn_piece(h2, wu_ref, wd_ref, f * fc, (f + 1) * fc)
  y_ref[0] = _ffn_post(x1, acc, ga2_ref, gl_ref)


def _out_ffn(x3, att, ssd, ga1, sc2, sh2, ga2, g_ffn, g_final, w_out_b, w_up_b, w_down_b, tm):
  nb, tb, d = x3.shape
  nt = tb // tm
  per_row = ga1.shape[1] != 1
  if per_row:
    mspec = pl.BlockSpec((1, tm, d), lambda b, i: (b, i, 0))
  else:
    mspec = pl.BlockSpec((1, 1, d), lambda b, i: (b, 0, 0))
  tok = lambda w: pl.BlockSpec((1, tm, w), lambda b, i: (b, i, 0))
  const = lambda shape: pl.BlockSpec(shape, lambda b, i: (0, 0), pipeline_mode=pl.Buffered(1))
  return pl.pallas_call(
      _ffn_kernel,
      grid=(nb, nt),
      in_specs=[tok(d), tok(ATT_WIDTH), tok(SSD_WIDTH), mspec, mspec, mspec, mspec,
                const((1, d)), const((1, d)),
                const((d, d)), const((d, D_FF)), const((D_FF, d))],
      out_specs=tok(d),
      out_shape=jax.ShapeDtypeStruct((nb, tb, d), F32),
      compiler_params=pltpu.CompilerParams(
          dimension_semantics=("arbitrary", "arbitrary"), vmem_limit_bytes=VMEM_LIMIT),
      name="out_ffn",
  )(x3, att, ssd, ga1, sc2, sh2, ga2, g_ffn.reshape(1, d), g_final.reshape(1, d),
    w_out_b, w_up_b, w_down_b)


def _ffn_sample_kernel(pt_ref, x_ref, att_ref, ssd_ref, ga1_ref, sc2_ref, sh2_ref, ga2_ref,
                       gf_ref, gl_ref, wo_ref, wu_ref, wd_ref,
                       qb_ref, kn_ref, vn_ref, blast_ref, bnew_ref, lam_ref, g_ref, ck_hbm, cv_hbm,
                       y_ref, o_ref,
                       kbuf, vbuf, sem, m_sc, l_sc, acc_sc, kpad, vpad, facc_sc,
                       *, n_steps, steps_per_seq):
  i = pl.program_id(0)
  seq = i // steps_per_seq
  part = i % steps_per_seq
  rps = ROUNDS_PER_STEP
  n_chains = SOFTMAX_CHAINS
  ahead = ROUND_SLOTS - 1
  last_part = part == steps_per_seq - 1

  def ring_slot(r, c):
    return (r % ROUND_SLOTS) * n_chains + c

  def round_copies(r):
    if r < rps:
      sq, rnd = seq, part * rps + r
    else:
      nxt = i + 1
      sq, rnd = nxt // steps_per_seq, (nxt % steps_per_seq) * rps + r - rps
    copies = []
    for c in range(n_chains):
      sl = ring_slot(r, c)
      for p in range(PAGES_PER_GROUP):
        pg = pt_ref[sq, (rnd * n_chains + c) * PAGES_PER_GROUP + p]
        copies.append((pltpu.make_async_copy(ck_hbm.at[pg], kbuf.at[sl, p], sem.at[0, sl]), 0))
        copies.append((pltpu.make_async_copy(cv_hbm.at[pg], vbuf.at[sl, p], sem.at[1, sl]), 1))
    return copies

  def start_round(r):
    for cp, prio in round_copies(r):
      cp.start(priority=prio)

  @pl.when(i == 0)
  def _():
    kpad[...] = jnp.zeros(kpad.shape, kpad.dtype)
    vpad[...] = jnp.zeros(vpad.shape, vpad.dtype)
    for r in range(ahead):
      start_round(r)

  @pl.when(part == 0)
  def _():
    m_sc[...] = jnp.full(m_sc.shape, NEG, F32)
    l_sc[...] = jnp.zeros(l_sc.shape, F32)
    acc_sc[...] = jnp.zeros(acc_sc.shape, F32)

  qb = qb_ref[0]
  rows_per_head = 2 * SUBLANES

  def softmax_probs(c, s):
    m_prev = m_sc[c]
    m_new = jnp.maximum(m_prev, jnp.max(s, axis=-1, keepdims=True))
    alpha = jnp.exp2(m_prev - m_new)
    p = jnp.exp2(s - m_new[:, 0:1])
    l_sc[c] = l_sc[c] * alpha + jnp.sum(p, axis=-1, keepdims=True)
    m_sc[c] = m_new
    return alpha, p.astype(BF16)

  def accumulate(c, alpha, pb, values_of_head):
    pv = [_dot(pb[h * rows_per_head:(h + 1) * rows_per_head, :], values_of_head(h))
          for h in range(ATT_HEADS)]
    acc_sc[c] = acc_sc[c] * alpha + jnp.concatenate(pv, axis=0)

  def group_scores(r, c):
    sl = ring_slot(r, c)
    kt = jnp.concatenate([kbuf[sl, p].astype(BF16) for p in range(PAGES_PER_GROUP)], axis=1)
    s = _dot(qb, kt)
    if r == rps - 1 and c == n_chains - 1:
      tail = s[:, -PAGE_SIZE:] + jnp.where(last_part, blast_ref[...], 0.0)
      s = jnp.concatenate([s[:, :-PAGE_SIZE], tail], axis=1)
    return s

  def group_values(r, c):
    sl = ring_slot(r, c)

    def values_of_head(h):
      v = [vbuf[sl, p, pl.ds(h, PAGE_SIZE, stride=ATT_HEADS), :] for p in range(PAGES_PER_GROUP)]
      return jnp.concatenate(v, axis=0).astype(BF16)

    return values_of_head

  x1, h2 = _ffn_pre(x_ref, att_ref, ssd_ref, ga1_ref, sc2_ref, sh2_ref, gf_ref, wo_ref)
  fc = D_FF // rps
  for r in range(rps):
    for cp, _ in round_copies(r):
      cp.wait()
    if r + ahead < rps:
      start_round(r + ahead)
    else:
      pl.when(i + 1 < n_steps)(functools.partial(start_round, r + ahead))
    s = [group_scores(r, c) for c in range(n_chains)]
    vals = [[group_values(r, c)(h) for h in range(ATT_HEADS)] for c in range(n_chains)]
    u = jnp.maximum(_dot(h2, wu_ref[:, r * fc:(r + 1) * fc]), 0.0)
    ub = (u * u).astype(BF16)
    probs = [softmax_probs(c, s[c]) for c in range(n_chains)]
    for c in range(n_chains):
      accumulate(c, *probs[c], vals[c].__getitem__)
    piece = _dot(ub, wd_ref[r * fc:(r + 1) * fc, :])
    facc_sc[...] = piece if r == 0 else facc_sc[...] + piece
  y_ref[0] = _ffn_post(x1, facc_sc[...], ga2_ref, gl_ref)

  @pl.when(last_part)
  def _():
    for c in range(1, n_chains):
      m0, mc = m_sc[0], m_sc[c]
      m = jnp.maximum(m0, mc)
      a0, ac = jnp.exp2(m0 - m), jnp.exp2(mc - m)
      l_sc[0] = l_sc[0] * a0 + l_sc[c] * ac
      acc_sc[0] = acc_sc[0] * a0 + acc_sc[c] * ac
      m_sc[0] = m
    t_new = kn_ref.shape[1]
    kpad[0:t_new, :] = kn_ref[0].astype(BF16)
    vpad[0:t_new, :] = vn_ref[0].astype(BF16)
    sn = _dot_nt(qb, kpad[...]) + bnew_ref[...]
    accumulate(0, *softmax_probs(0, sn), lambda h: vpad[:, h * ATT_V_DIM:(h + 1) * ATT_V_DIM])
    o = acc_sc[0] / l_sc[0]
    lam = _diff_lambda(lam_ref)
    for h in range(ATT_HEADS):
      o0 = o[(2 * h) * SUBLANES:(2 * h + 1) * SUBLANES, :]
      o1 = o[(2 * h + 1) * SUBLANES:(2 * h + 2) * SUBLANES, :]
      o_ref[0, :, h * ATT_V_DIM:(h + 1) * ATT_V_DIM] = _diff_combine(
          o0, o1, lam, g_ref[...]).astype(o_ref.dtype)


def _ffn_with_sample_attention(x_p, att_p, ssd_p, ga1, sc2, sh2, ga2, g_ffn, g_final,
                               w_out_b, w_up_b, w_down_b,
                               page_table, qb, k_new, v_new, blast, bnew, lam4, subln_g,
                               cache_k, cache_v):
  nb, tb, d = x_p.shape
  tm = FFN_TM
  tiles_per_b = tb // tm
  n_steps = nb * tiles_per_b
  n_seq, n_pages = page_table.shape
  t_new = k_new.shape[1]
  pages_per_step = PAGES_PER_GROUP * SOFTMAX_CHAINS * ROUNDS_PER_STEP
  steps_per_seq = n_pages // pages_per_step
  ring = ROUND_SLOTS * SOFTMAX_CHAINS
  assert t_new == SUBLANES and n_pages % pages_per_step == 0
  assert n_seq * steps_per_seq == n_steps, "one batch of page rounds per FFN tile"
  assert ROUNDS_PER_STEP % ROUND_SLOTS == 0, "ring slots must be static per round"
  w = ATT_WIDTH
  assert cache_k.shape[1:] == (w, PAGE_SIZE) and cache_v.shape[1:] == (PAGE_SIZE * ATT_HEADS, ATT_V_DIM)
  tok = lambda width: pl.BlockSpec((1, tm, width),
                                   lambda i, pt: (i // tiles_per_b, i % tiles_per_b, 0))
  mspec = pl.BlockSpec((1, 1, d), lambda i, pt: (i // tiles_per_b, 0, 0))
  const = lambda shape: pl.BlockSpec(shape, lambda i, pt: (0,) * len(shape),
                                     pipeline_mode=pl.Buffered(1))
  per_seq = lambda r: pl.BlockSpec((1, r, w), lambda i, pt: (i // steps_per_seq, 0, 0))
  grid_spec = pltpu.PrefetchScalarGridSpec(
      num_scalar_prefetch=1,
      grid=(n_steps,),
      in_specs=[tok(d), tok(ATT_WIDTH), tok(SSD_WIDTH), mspec, mspec, mspec, mspec,
                const((1, d)), const((1, d)),
                const((d, d)), const((d, D_FF)), const((D_FF, d)),
                per_seq(8 * SUBLANES), per_seq(t_new), per_seq(t_new),
                const((8 * SUBLANES, LANES)), const((8 * SUBLANES, LANES)),
                const((4, ATT_HEAD_DIM)), const((1, ATT_V_DIM)),
                pl.BlockSpec(memory_space=pl.ANY),
                pl.BlockSpec(memory_space=pl.ANY)],
      out_specs=[tok(d), per_seq(t_new)],
      scratch_shapes=[pltpu.VMEM((ring, PAGES_PER_GROUP) + cache_k.shape[1:], F32),
                      pltpu.VMEM((ring, PAGES_PER_GROUP) + cache_v.shape[1:], F32),
                      pltpu.SemaphoreType.DMA((2, ring)),
                      pltpu.VMEM((SOFTMAX_CHAINS, 8 * SUBLANES, LANES), F32),
                      pltpu.VMEM((SOFTMAX_CHAINS, 8 * SUBLANES, LANES), F32),
                      pltpu.VMEM((SOFTMAX_CHAINS, 8 * SUBLANES, ATT_V_DIM), F32),
                      pltpu.VMEM((PAGE_SIZE, w), BF16),
                      pltpu.VMEM((PAGE_SIZE, w), BF16),
                      pltpu.VMEM((tm, d), F32)])
  return pl.pallas_call(
      functools.partial(_ffn_sample_kernel, n_steps=n_steps, steps_per_seq=steps_per_seq),
      grid_spec=grid_spec,
      out_shape=[jax.ShapeDtypeStruct((nb, tb, d), F32),
                 jax.ShapeDtypeStruct((n_seq, t_new, w), BF16)],
      compiler_params=pltpu.CompilerParams(
          dimension_semantics=("arbitrary",), vmem_limit_bytes=VMEM_LIMIT),
      name="ffn_sample_attention",
  )(page_table, x_p, att_p, ssd_p, ga1, sc2, sh2, ga2, g_ffn.reshape(1, d), g_final.reshape(1, d),
    w_out_b, w_up_b, w_down_b, qb, k_new, v_new, blast, bnew, lam4, subln_g, cache_k, cache_v)


def kernel(x_prompt, x_sample, cache_k, cache_v, state_conv, state_ssm, page_table, c_prompt,
           c_sample, rel_bias, w_ada, b_ada, g_mix, g_ffn, w_in, w_out, lam_q1, lam_k1, lam_q2,
           lam_k2, subln_g, conv_w, conv_b, dt_bias, a_log, d_skip, ssd_norm_g, w_up, w_down,
           g_final):
  assert w_ada.shape[0] == 1, "single-layer step"
  bp, sp, d = x_prompt.shape
  bs, ts, _ = x_sample.shape
  n_pool = cache_k.shape[1]

  w_in_p = jnp.pad(w_in[0], ((0, 0), (0, IN_PAD - IN_WIDTH))).astype(BF16)
  w_kt = w_in[0][:, ATT_WIDTH:2 * ATT_WIDTH].T.astype(BF16)
  w_out_b = w_out[0].astype(BF16)
  w_up_b = w_up[0].astype(BF16)
  w_down_b = w_down[0].astype(BF16)
  lam4 = jnp.concatenate([lam_q1, lam_k1, lam_q2, lam_k2], axis=0)
  subln = subln_g.reshape(1, ATT_V_DIM)
  dtb_p = jnp.pad(dt_bias, ((0, 0), (0, LANES - SSD_HEADS)))
  alog_p = jnp.pad(a_log, ((0, 0), (0, LANES - SSD_HEADS)))
  dsk_e = jnp.repeat(d_skip[0], SSD_HEAD_DIM).reshape(1, SSD_WIDTH)
  ssd_args = (conv_w[0], conv_b, dtb_p, alog_p, dsk_e, ssd_norm_g)

  mods = _modulation(jnp.concatenate([c_prompt, c_sample], axis=0), w_ada[0], b_ada[0])
  mp = [m.reshape(bp, 1, d) for m in jnp.split(mods[:bp], 6, axis=-1)]
  ms = [jnp.repeat(m, ts, axis=0).reshape(1, bs * ts, d) for m in jnp.split(mods[bp:], 6, axis=-1)]
  near, diag, slast, snew = _bias_tiles(rel_bias)

  q_hm, kt_hm, v_hm, kt_p, v_p, z_p, xbc_p, dt_p = _in_proj(
      x_prompt, mp[1], mp[0], g_mix[0], w_in_p, w_kt, tm=512, prompt=True)
  att_p = _prompt_attention(q_hm, kt_hm, v_hm, near, diag, lam4, subln)
  ssd_p, conv_p, h_p = _ssd(
      xbc_p, z_p, dt_p, jnp.zeros((bp, CONV_W - 1, CONV_DIM), F32),
      jnp.zeros((bp, SSD_WIDTH, D_STATE), F32), *ssd_args, tv=SSD_CHUNK, seqs_per_step=2)

  n_tok = bs * ts
  xs3 = x_sample.reshape(1, n_tok, d)
  q_s, k_s, v_s, z_s, xbc_s, dt_s = _in_proj(
      xs3, ms[1], ms[0], g_mix[0], w_in_p, w_kt, tm=n_tok, prompt=False)
  qt = q_s.reshape(ATT_HEADS, bs, ts, 2, ATT_HEAD_DIM).transpose(1, 0, 3, 2, 4)
  qt = qt.reshape(bs, 2 * ATT_HEADS, ts, ATT_HEAD_DIM)
  eye = jnp.eye(2 * ATT_HEADS, dtype=BF16)
  qb = (qt[:, :, :, None, :] * eye[None, :, None, :, None]).reshape(bs, 2 * ATT_HEADS * ts, ATT_WIDTH)
  blast = jnp.repeat(slast, 2, axis=0).reshape(2 * ATT_HEADS * SUBLANES, LANES)
  bnew = jnp.repeat(snew, 2, axis=0).reshape(2 * ATT_HEADS * SUBLANES, LANES)
  y_p, att_s = _ffn_with_sample_attention(
      x_prompt, att_p, ssd_p, mp[2], mp[4], mp[3], mp[5], g_ffn[0], g_final,
      w_out_b, w_up_b, w_down_b,
      page_table, qb, k_s.reshape(bs, ts, ATT_WIDTH), v_s.reshape(bs, ts, ATT_WIDTH),
      blast, bnew, lam4, subln,
      jnp.transpose(cache_k[0], (0, 2, 3, 4, 1)).reshape(n_pool, ATT_WIDTH, PAGE_SIZE),
      cache_v[0].reshape(n_pool, PAGE_SIZE * ATT_HEADS, ATT_V_DIM))
  ssd_s, conv_s, h_s = _ssd(
      xbc_s.reshape(bs, ts, CONV_DIM), z_s.reshape(bs, ts, SSD_WIDTH), dt_s.reshape(bs, ts, LANES),
      state_conv[0], state_ssm[0].reshape(bs, SSD_WIDTH, D_STATE), *ssd_args, tv=ts,
      seqs_per_step=4)
  y_s = _out_ffn(xs3, att_s.reshape(1, n_tok, ATT_WIDTH), ssd_s.reshape(1, n_tok, SSD_WIDTH),
                 ms[2], ms[4], ms[3], ms[5], g_ffn[0], g_final, w_out_b, w_up_b, w_down_b, tm=n_tok)

  hshape = (SSD_HEADS, SSD_HEAD_DIM, D_STATE)
  return (y_p, y_s.reshape(bs, ts, d),
          kt_p.reshape(1, bp, ATT_HEADS, 2, ATT_HEAD_DIM, sp).transpose(0, 1, 5, 2, 3, 4),
          v_p.reshape(1, bp, sp, ATT_HEADS, ATT_V_DIM),
          conv_p[None], h_p.reshape(1, bp, *hshape),
          k_s.reshape(1, bs, ts, ATT_HEADS, 2, ATT_HEAD_DIM),
          v_s.reshape(1, bs, ts, ATT_HEADS, ATT_V_DIM),
          conv_s[None], h_s.reshape(1, bs, *hshape))
```

```python
import functools
import itertools
import math

import numpy as np
import jax
import jax.numpy as jnp
from jax import lax
from jax.experimental import pallas as pl
from jax.experimental.pallas import tpu as pltpu

F32 = jnp.float32
BF16 = jnp.bfloat16

D_MODEL = 1024
PAGE_SIZE = 128
ATT_HEADS = 4
ATT_HEAD_DIM = 64
ATT_V_DIM = 2 * ATT_HEAD_DIM
ATT_WIDTH = ATT_HEADS * ATT_V_DIM
SSD_HEADS = 8
SSD_HEAD_DIM = 64
SSD_WIDTH = SSD_HEADS * SSD_HEAD_DIM
SSD_GROUPS = 2
D_STATE = 128
CONV_W = 4
CONV_DIM = SSD_WIDTH + 2 * SSD_GROUPS * D_STATE
SSD_CHUNK = 128
IN_WIDTH = 3 * ATT_WIDTH + SSD_WIDTH + CONV_DIM + SSD_HEADS
D_FF = 4 * D_MODEL
N_BUCKETS = 32
MAX_DISTANCE = 128
MAX_EXACT = N_BUCKETS // 2
EPS = 1e-6
LAM_INIT = 0.8 - 0.6 * math.exp(-0.3 * 0)

LANES = 128
SUBLANES = 8
IN_PAD = 3 * ATT_WIDTH + SSD_WIDTH + CONV_DIM + LANES
LOG2E = 1.4426950408889634
NEG = -1e30
VMEM_LIMIT = 56 * 1024 * 1024

ATT_T = 256
FFN_TM = 256
PAGES_PER_GROUP = 8
SOFTMAX_CHAINS = 1
ROUNDS_PER_STEP = 8
ROUND_SLOTS = 4


def _bucket_lower_bounds():
  d = np.arange(0, 4 * MAX_DISTANCE)
  nf = np.maximum(d, 1).astype(np.float64)
  large = MAX_EXACT + (np.log(nf / MAX_EXACT) / math.log(MAX_DISTANCE / MAX_EXACT)
                       * (N_BUCKETS - MAX_EXACT)).astype(np.int64)
  large = np.minimum(large, N_BUCKETS - 1)
  bucket = np.where(d < MAX_EXACT, d, large)
  return [int(np.argmax(bucket >= b)) for b in range(N_BUCKETS)]


_LOWER = _bucket_lower_bounds()
FAR_DIST = _LOWER[N_BUCKETS - 1]
assert FAR_DIST <= PAGE_SIZE and FAR_DIST <= ATT_T


def _silu(x):
  h = 0.5 * x
  return h * jnp.tanh(h) + h


def _dot(a, b):
  return jnp.dot(a, b, preferred_element_type=F32)


def _dot_nt(a, b):
  return lax.dot_general(a, b, (((1,), (1,)), ((), ())), preferred_element_type=F32)


def _split3(x):
  hi = x.astype(BF16)
  r1 = x - hi.astype(F32)
  mid = r1.astype(BF16)
  lo = (r1 - mid.astype(F32)).astype(BF16)
  return hi, mid, lo


def _mod_kernel(c_ref, w_ref, b_ref, o_ref):
  s = _silu(c_ref[...]).astype(BF16)
  o_ref[...] = _dot(s, w_ref[...].astype(BF16)) + b_ref[...]


def _modulation(c_all, w_ada, b_ada):
  n = c_all.shape[0]
  tn = D_MODEL
  return pl.pallas_call(
      _mod_kernel,
      grid=(6 * D_MODEL // tn,),
      in_specs=[pl.BlockSpec((n, D_MODEL), lambda j: (0, 0)),
                pl.BlockSpec((D_MODEL, tn), lambda j: (0, j)),
                pl.BlockSpec((1, tn), lambda j: (0, j))],
      out_specs=pl.BlockSpec((n, tn), lambda j: (0, j)),
      out_shape=jax.ShapeDtypeStruct((n, 6 * D_MODEL), F32),
      compiler_params=pltpu.CompilerParams(vmem_limit_bytes=VMEM_LIMIT),
      name="modulation",
  )(c_all, w_ada, b_ada.reshape(1, -1))


def _bias_kernel(tab_ref, near_ref, diag_ref, slast_ref, snew_ref):
  def bias_of(dist, h):
    val = jnp.full(dist.shape, tab_ref[0, h], F32)
    for b in range(1, N_BUCKETS):
      val = jnp.where(dist >= _LOWER[b], tab_ref[b, h], val)
    return (val - tab_ref[N_BUCKETS - 1, h]) * LOG2E

  t = ATT_T
  r = lax.broadcasted_iota(jnp.int32, (t, t), 0)
  c = lax.broadcasted_iota(jnp.int32, (t, t), 1)
  r8 = lax.broadcasted_iota(jnp.int32, (SUBLANES, LANES), 0)
  c8 = lax.broadcasted_iota(jnp.int32, (SUBLANES, LANES), 1)
  for h in range(ATT_HEADS):
    near_ref[h] = bias_of(t + c - r, h)
    d = c - r
    diag_ref[h] = jnp.where(d >= 0, bias_of(d, h), NEG)
    slast_ref[h] = bias_of(PAGE_SIZE + r8 - c8, h)
    dn = r8 - c8
    snew_ref[h] = jnp.where(dn >= 0, bias_of(dn, h), NEG)


def _bias_tiles(rel_bias):
  t = ATT_T
  return pl.pallas_call(
      _bias_kernel,
      in_specs=[pl.BlockSpec(memory_space=pltpu.SMEM)],
      out_shape=(jax.ShapeDtypeStruct((ATT_HEADS, t, t), F32),
                 jax.ShapeDtypeStruct((ATT_HEADS, t, t), F32),
                 jax.ShapeDtypeStruct((ATT_HEADS, SUBLANES, LANES), F32),
                 jax.ShapeDtypeStruct((ATT_HEADS, SUBLANES, LANES), F32)),
      compiler_params=pltpu.CompilerParams(vmem_limit_bytes=VMEM_LIMIT),
      name="bias_tiles",
  )(rel_bias)


def _inproj_kernel(x_ref, sc_ref, sh_ref, g_ref, w_ref, wqkt_ref, *out_refs, prompt):
  x = x_ref[0]
  var = jnp.mean(x * x, axis=-1, keepdims=True)
  h = x * lax.rsqrt(var + EPS) * g_ref[...]
  h = h * (1.0 + sc_ref[0]) + sh_ref[0]
  hb = h.astype(BF16)
  aw = ATT_WIDTH
  q_scale = ATT_HEAD_DIM ** -0.5 * LOG2E
  v = _dot(hb, w_ref[:, 2 * aw:3 * aw])
  if prompt:
    qtb_ref, kb_ref, vtb_ref, kt_ref, v_ref, z_ref, xbc_ref, dt_ref = out_refs
    qt = _dot_nt(wqkt_ref[0:aw, :], hb) * q_scale
    kt = _dot_nt(wqkt_ref[aw:2 * aw, :], hb)
    kt_ref[0] = kt
    k_tok = kt.T
    vt = v.T
    t = ATT_T
    for hd in range(ATT_HEADS):
      sl = slice(hd * ATT_V_DIM, (hd + 1) * ATT_V_DIM)
      kb_ref[0, hd] = k_tok[:, sl].astype(BF16)
      v_ref[0, pl.ds(hd, v.shape[0], stride=ATT_HEADS), :] = v[:, sl]
      for cc in range(kt.shape[1] // t):
        cols = slice(cc * t, (cc + 1) * t)
        qtb_ref[0, hd, cc] = qt[sl, cols].astype(BF16)
        vtb_ref[0, hd, cc] = vt[sl, cols].astype(BF16)
  else:
    q_ref, k_ref, v_ref, z_ref, xbc_ref, dt_ref = out_refs
    q = _dot(hb, w_ref[:, 0:aw]) * q_scale
    k_ref[0] = _dot(hb, w_ref[:, aw:2 * aw])
    v_ref[0] = v
    for hd in range(ATT_HEADS):
      sl = slice(hd * ATT_V_DIM, (hd + 1) * ATT_V_DIM)
      q_ref[0, hd] = q[:, sl].astype(BF16)
  o = 3 * aw
  z_ref[0] = _dot(hb, w_ref[:, o:o + SSD_WIDTH])
  o += SSD_WIDTH
  xbc_ref[0] = _dot(hb, w_ref[:, o:o + CONV_DIM])
  o += CONV_DIM
  dt_ref[0] = _dot(hb, w_ref[:, o:o + LANES])


def _in_proj(x3, sc, sh, g_mix, w_in_p, w_kt, tm, prompt):
  nb, tb, d = x3.shape
  nt = tb // tm
  per_row = sc.shape[1] != 1
  if per_row:
    mspec = pl.BlockSpec((1, tm, d), lambda b, i: (b, i, 0))
  else:
    mspec = pl.BlockSpec((1, 1, d), lambda b, i: (b, 0, 0))
  hm = pl.BlockSpec((1, ATT_HEADS, tm, ATT_V_DIM), lambda b, i: (b, 0, i, 0))
  tok = lambda w: pl.BlockSpec((1, tm, w), lambda b, i: (b, i, 0))
  hm_shape = jax.ShapeDtypeStruct((nb, ATT_HEADS, tb, ATT_V_DIM), BF16)
  tok_shape = lambda w: jax.ShapeDtypeStruct((nb, tb, w), F32)
  tail_specs = [tok(SSD_WIDTH), tok(CONV_DIM), tok(LANES)]
  tail_shapes = [tok_shape(SSD_WIDTH), tok_shape(CONV_DIM), tok_shape(LANES)]
  if prompt:
    t = ATT_T
    fm = pl.BlockSpec((1, ATT_HEADS, tm // t, ATT_V_DIM, t), lambda b, i: (b, 0, i, 0, 0))
    fm_shape = jax.ShapeDtypeStruct((nb, ATT_HEADS, tb // t, ATT_V_DIM, t), BF16)
    out_specs = [fm, hm, fm,
                 pl.BlockSpec((1, ATT_WIDTH, tm), lambda b, i: (b, 0, i)),
                 pl.BlockSpec((1, tm * ATT_HEADS, ATT_V_DIM), lambda b, i: (b, i, 0))] + tail_specs
    out_shape = [fm_shape, hm_shape, fm_shape,
                 jax.ShapeDtypeStruct((nb, ATT_WIDTH, tb), F32),
                 jax.ShapeDtypeStruct((nb, tb * ATT_HEADS, ATT_V_DIM), F32)] + tail_shapes
  else:
    out_specs = [hm, tok(ATT_WIDTH), tok(ATT_WIDTH)] + tail_specs
    out_shape = [hm_shape, tok_shape(ATT_WIDTH), tok_shape(ATT_WIDTH)] + tail_shapes
  return pl.pallas_call(
      functools.partial(_inproj_kernel, prompt=prompt),
      grid=(nb, nt),
      in_specs=[tok(d), mspec, mspec,
                pl.BlockSpec((1, d), lambda b, i: (0, 0)),
                pl.BlockSpec((d, IN_PAD), lambda b, i: (0, 0)),
                pl.BlockSpec((2 * ATT_WIDTH, d), lambda b, i: (0, 0))],
      out_specs=out_specs,
      out_shape=out_shape,
      compiler_params=pltpu.CompilerParams(
          dimension_semantics=("arbitrary", "arbitrary"), vmem_limit_bytes=VMEM_LIMIT),
      name="in_proj",
  )(x3, sc, sh, g_mix.reshape(1, d), w_in_p, w_kt)


def _diff_lambda(lam_ref):
  lv = lam_ref[...]
  s1 = jnp.sum(lv[0:1] * lv[1:2], axis=-1, keepdims=True)
  s2 = jnp.sum(lv[2:3] * lv[3:4], axis=-1, keepdims=True)
  return jnp.exp(s1) - jnp.exp(s2) + LAM_INIT


def _diff_combine(o0, o1, lam, g):
  d = o0 - lam * o1
  d = d * lax.rsqrt(jnp.mean(d * d, axis=-1, keepdims=True) + EPS) * g
  return d * (1.0 - LAM_INIT)


def _attn_kernel(qt_ref, k_ref, vt_ref, near_ref, diag_ref, lam_ref, g_ref, o_ref,
                 m_sc, l_sc, acc_sc):
  t = ATT_T
  qi = pl.program_id(1)
  feat = lax.broadcasted_iota(jnp.int32, (ATT_V_DIM, t), 0)
  m_sc[...] = jnp.full(m_sc.shape, NEG, F32)
  l_sc[...] = jnp.zeros(l_sc.shape, F32)
  acc_sc[...] = jnp.zeros(acc_sc.shape, F32)

  def scores(h, c, bias_ref):
    qt = qt_ref[0, h, 0]
    zero = jnp.zeros_like(qt)
    q2t = jnp.concatenate([jnp.where(feat < ATT_HEAD_DIM, qt, zero),
                           jnp.where(feat >= ATT_HEAD_DIM, qt, zero)], axis=1)
    start = pl.multiple_of(c * t, t)
    kc = k_ref[0, h, pl.ds(start, t), :]
    s = _dot(kc, q2t)
    if bias_ref is not None:
      bias = bias_ref[h]
      s = s + jnp.concatenate([bias, bias], axis=1)
    return s

  def softmax_probs(h, s):
    m_prev = m_sc[h]
    m_new = jnp.maximum(m_prev, jnp.max(s, axis=0, keepdims=True))
    alpha = jnp.exp2(m_prev - m_new)
    p = jnp.exp2(s - m_new)
    l_sc[h] = l_sc[h] * alpha + jnp.sum(p, axis=0, keepdims=True)
    m_sc[h] = m_new
    return alpha, p.astype(BF16)

  def step(c, bias_ref):
    s = [scores(h, c, bias_ref) for h in range(ATT_HEADS)]
    probs = [softmax_probs(h, s[h]) for h in range(ATT_HEADS)]
    for h in range(ATT_HEADS):
      alpha, pb = probs[h]
      acc_sc[h] = acc_sc[h] * alpha + _dot(vt_ref[0, h, c], pb)

  def far_body(c, carry):
    step(c, None)
    return carry

  lax.fori_loop(0, jnp.maximum(qi - 1, 0), far_body, 0)

  @pl.when(qi >= 1)
  def _():
    step(qi - 1, near_ref)

  step(qi, diag_ref)

  lam = _diff_lambda(lam_ref)
  for h in range(ATT_HEADS):
    o = acc_sc[h] / l_sc[h]
    d = o[:, :t] - lam * o[:, t:]
    d = d * lax.rsqrt(jnp.mean(d * d, axis=0, keepdims=True) + EPS) * g_ref[...]
    d = d * (1.0 - LAM_INIT)
    o_ref[0, :, h * ATT_V_DIM:(h + 1) * ATT_V_DIM] = d.T.astype(o_ref.dtype)


def _prompt_attention(qt_fm, k_hm, vt_fm, near, diag, lam4, subln_g):
  b, nh, s, e = k_hm.shape
  t = ATT_T
  nq = s // t
  g_cols = jnp.broadcast_to(subln_g.reshape(e, 1), (e, t))
  return pl.pallas_call(
      _attn_kernel,
      grid=(b, nq),
      in_specs=[pl.BlockSpec((1, nh, 1, e, t), lambda bi, qi: (bi, 0, qi, 0, 0)),
                pl.BlockSpec((1, nh, s, e), lambda bi, qi: (bi, 0, 0, 0)),
                pl.BlockSpec((1, nh, nq, e, t), lambda bi, qi: (bi, 0, 0, 0, 0)),
                pl.BlockSpec((nh, t, t), lambda bi, qi: (0, 0, 0)),
                pl.BlockSpec((nh, t, t), lambda bi, qi: (0, 0, 0)),
                pl.BlockSpec((4, ATT_HEAD_DIM), lambda bi, qi: (0, 0)),
                pl.BlockSpec((e, t), lambda bi, qi: (0, 0))],
      out_specs=pl.BlockSpec((1, t, nh * e), lambda bi, qi: (bi, qi, 0)),
      out_shape=jax.ShapeDtypeStruct((b, s, nh * e), BF16),
      scratch_shapes=[pltpu.VMEM((nh, 1, 2 * t), F32),
                      pltpu.VMEM((nh, 1, 2 * t), F32),
                      pltpu.VMEM((nh, e, 2 * t), F32)],
      compiler_params=pltpu.CompilerParams(
          dimension_semantics=("arbitrary", "arbitrary"), vmem_limit_bytes=VMEM_LIMIT),
      name="prompt_attention",
  )(qt_fm, k_hm, vt_fm, near, diag, lam4, g_cols)


def _ssd_kernel(xbc_ref, z_ref, dt_ref, pre_ref, h0_ref, *refs, tv, nc, alternate):
  consts, (ssd_ref, conv_ref, h_ref, xp_sc) = refs[:8], refs[8:]
  hp = SUBLANES
  b = pl.program_id(0)
  c = pl.program_id(1)
  n_seq = xbc_ref.shape[0]

  @pl.when((b == 0) & (c == 0))
  def _():
    xp_sc[...] = jnp.zeros(xp_sc.shape, F32)

  @pl.when(c == 0)
  def _():
    for bb in range(n_seq):
      xp_sc[bb, hp - (CONV_W - 1):hp, :] = pre_ref[bb]
      h_ref[bb] = h0_ref[bb]

  stages = [_ssd_chunk(xbc_ref.at[bb], z_ref.at[bb], dt_ref.at[bb], *consts,
                       ssd_ref.at[bb], h_ref.at[bb], xp_sc.at[bb], tv=tv) for bb in range(n_seq)]
  for _ in (itertools.zip_longest(*stages) if alternate else itertools.chain(*stages)):
    pass

  @pl.when(c == nc - 1)
  def _():
    for bb in range(n_seq):
      conv_ref[bb] = xp_sc[bb, hp + tv - (CONV_W - 1):hp + tv, :]


def _ssd_chunk(xbc_ref, z_ref, dt_ref, cw_ref, cb_ref, dtb_ref, alog_ref, dsk_ref, ng_ref, e_ref,
               tri_ref, ssd_ref, h_ref, xp_sc, *, tv):
  L = SSD_CHUNK
  hp = SUBLANES
  xp_sc[hp:hp + tv, :] = xbc_ref[...]
  cw = cw_ref[...]
  y = cb_ref[...]
  cur = xp_sc[hp:hp + L, :]
  hist = xp_sc[0:hp, :]
  head_row = lax.broadcasted_iota(jnp.int32, (hp, CONV_DIM), 0)
  for i in range(CONV_W):
    back = CONV_W - 1 - i
    if back == 0:
      tap = cur
    else:
      rolled = pltpu.roll(cur, back, 0)
      head = jnp.where(head_row < back, pltpu.roll(hist, back, 0), rolled[0:hp])
      tap = jnp.concatenate([head, rolled[hp:]], axis=0)
    y = y + tap * cw[i:i + 1]
  xa = _silu(y)
  xp_sc[0:hp, :] = xp_sc[tv:tv + hp, :]
  yield

  xs = xa[:, :SSD_WIDTH]
  gw = SSD_GROUPS * D_STATE
  bmat = xa[:, SSD_WIDTH:SSD_WIDTH + gw].astype(BF16)
  cmat = xa[:, SSD_WIDTH + gw:].astype(BF16)

  x = dt_ref[...] + dtb_ref[...]
  dtv = jnp.maximum(x, 0.0) + jnp.log1p(jnp.exp(-jnp.abs(x)))
  if tv < L:
    dtv = jnp.concatenate([dtv, jnp.zeros((L - tv, LANES), F32)], axis=0)
  a = dtv * (-jnp.exp(alog_ref[...]))

  tri = tri_ref[...]
  acs = sum(_dot(tri, part) for part in _split3(a))
  yield
  e = e_ref[...]
  aexp = sum(_dot(part, e) for part in _split3(acs))
  dtexp = _dot(dtv.astype(BF16), e)
  xd = xs * dtexp
  a_last = acs[L - 1:L, :]
  xdd_t = (xd * jnp.exp(aexp[L - 1:L, :] - aexp)).T
  acs_t = acs.T
  yield

  ri = lax.broadcasted_iota(jnp.int32, (L, L), 0)
  ci = lax.broadcasted_iota(jnp.int32, (L, L), 1)
  causal = ri >= ci
  lane = lax.broadcasted_iota(jnp.int32, (L, LANES), 1)
  heads_per_group = SSD_HEADS // SSD_GROUPS
  hprev = h_ref[...]
  hb = hprev.astype(BF16)

  ydiag, yoff, upd = [], [], []
  for g in range(SSD_GROUPS):
    cg = cmat[:, g * D_STATE:(g + 1) * D_STATE]
    bg = bmat[:, g * D_STATE:(g + 1) * D_STATE]
    scores = _dot_nt(cg, bg)
    for pair in range(heads_per_group // 2):
      h0 = g * heads_per_group + 2 * pair
      ms = []
      for h in (h0, h0 + 1):
        col = jnp.broadcast_to(acs[:, h:h + 1], (L, L))
        row = acs_t[h:h + 1, :]
        lm = jnp.exp(jnp.where(causal, col - row, NEG))
        ms.append((scores * lm).astype(BF16))
      xpair = xd[:, h0 * SSD_HEAD_DIM:(h0 + 2) * SSD_HEAD_DIM]
      xbd = jnp.concatenate([jnp.where(lane < SSD_HEAD_DIM, xpair, 0.0),
                             jnp.where(lane >= SSD_HEAD_DIM, xpair, 0.0)], axis=0)
      ydiag.append(_dot(jnp.concatenate(ms, axis=1), xbd.astype(BF16)))
      yield
    rows = slice(g * heads_per_group * SSD_HEAD_DIM, (g + 1) * heads_per_group * SSD_HEAD_DIM)
    yoff.append(_dot_nt(cg, hb[rows, :]))
    upd.append(_dot(xdd_t[rows, :].astype(BF16), bg))
    yield

  y = (jnp.concatenate(ydiag, axis=1) + jnp.concatenate(yoff, axis=1) * jnp.exp(aexp)
       + dsk_ref[...] * xs)
  yield

  decayed = []
  chunk_decay = jnp.exp(a_last)
  for h in range(SSD_HEADS):
    dec = jnp.broadcast_to(chunk_decay[:, h:h + 1], (SSD_HEAD_DIM, D_STATE))
    decayed.append(hprev[h * SSD_HEAD_DIM:(h + 1) * SSD_HEAD_DIM, :] * dec)
  h_ref[...] = jnp.concatenate(decayed, axis=0) + jnp.concatenate(upd, axis=0)
  yield

  yv = y[:tv] * _silu(z_ref[...])
  sq = yv * yv
  half = SSD_WIDTH // SSD_GROUPS
  r0 = lax.rsqrt(jnp.mean(sq[:, :half], axis=-1, keepdims=True) + EPS)
  r1 = lax.rsqrt(jnp.mean(sq[:, half:], axis=-1, keepdims=True) + EPS)
  out = jnp.concatenate([yv[:, :half] * r0, yv[:, half:] * r1], axis=1) * ng_ref[...]
  ssd_ref[...] = out.astype(ssd_ref.dtype)


def _ssd(xbc, z, dt, prefix, h0, conv_w, conv_b, dtb_p, alog_p, dsk_e, norm_g, tv, seqs_per_step):
  nb, tb, _ = xbc.shape
  nc = tb // tv
  ns = seqs_per_step
  assert nb % ns == 0
  e_np = np.zeros((LANES, SSD_WIDTH), np.float32)
  for h in range(SSD_HEADS):
    e_np[h, h * SSD_HEAD_DIM:(h + 1) * SSD_HEAD_DIM] = 1.0
  tri_np = np.tril(np.ones((SSD_CHUNK, SSD_CHUNK), np.float32))
  full = lambda shape: pl.BlockSpec(shape, lambda b, c: (0,) * len(shape))
  tok = lambda w: pl.BlockSpec((ns, tv, w), lambda b, c: (b, c, 0))
  per_b = lambda r, w: pl.BlockSpec((ns, r, w), lambda b, c: (b, 0, 0))
  return pl.pallas_call(
      functools.partial(_ssd_kernel, tv=tv, nc=nc, alternate=tv < SSD_CHUNK),
      grid=(nb // ns, nc),
      in_specs=[tok(CONV_DIM), tok(SSD_WIDTH), tok(LANES),
                per_b(CONV_W - 1, CONV_DIM), per_b(SSD_WIDTH, D_STATE),
                full((CONV_W, CONV_DIM)), full((1, CONV_DIM)), full((1, LANES)), full((1, LANES)),
                full((1, SSD_WIDTH)), full((1, SSD_WIDTH)),
                full((LANES, SSD_WIDTH)), full((SSD_CHUNK, SSD_CHUNK))],
      out_specs=[tok(SSD_WIDTH), per_b(CONV_W - 1, CONV_DIM), per_b(SSD_WIDTH, D_STATE)],
      out_shape=[jax.ShapeDtypeStruct((nb, tb, SSD_WIDTH), BF16),
                 jax.ShapeDtypeStruct((nb, CONV_W - 1, CONV_DIM), F32),
                 jax.ShapeDtypeStruct((nb, SSD_WIDTH, D_STATE), F32)],
      scratch_shapes=[pltpu.VMEM((ns, SSD_CHUNK + SUBLANES, CONV_DIM), F32)],
      compiler_params=pltpu.CompilerParams(
          dimension_semantics=("arbitrary", "arbitrary"), vmem_limit_bytes=VMEM_LIMIT),
      name="ssd_scan",
  )(xbc, z, dt, prefix, h0, conv_w, conv_b, dtb_p, alog_p, dsk_e, norm_g,
    jnp.asarray(e_np, BF16), jnp.asarray(tri_np, BF16))


def _ffn_pre(x_ref, att_ref, ssd_ref, ga1_ref, sc2_ref, sh2_ref, gf_ref, wo_ref):
  mix = _dot(att_ref[0], wo_ref[0:ATT_WIDTH, :]) + _dot(ssd_ref[0], wo_ref[ATT_WIDTH:, :])
  x1 = x_ref[0] + ga1_ref[0] * mix
  var = jnp.mean(x1 * x1, axis=-1, keepdims=True)
  h2 = x1 * lax.rsqrt(var + EPS) * gf_ref[...]
  return x1, (h2 * (1.0 + sc2_ref[0]) + sh2_ref[0]).astype(BF16)


def _ffn_piece(h2, wu_ref, wd_ref, lo, hi):
  u = jnp.maximum(_dot(h2, wu_ref[:, lo:hi]), 0.0)
  return _dot((u * u).astype(BF16), wd_ref[lo:hi, :])


def _ffn_post(x1, acc, ga2_ref, gl_ref):
  x2 = x1 + ga2_ref[0] * acc
  var2 = jnp.mean(x2 * x2, axis=-1, keepdims=True)
  return x2 * lax.rsqrt(var2 + EPS) * gl_ref[...]


def _ffn_kernel(x_ref, att_ref, ssd_ref, ga1_ref, sc2_ref, sh2_ref, ga2_ref, gf_ref, gl_ref,
                wo_ref, wu_ref, wd_ref, y_ref):
  x1, h2 = _ffn_pre(x_ref, att_ref, ssd_ref, ga1_ref, sc2_ref, sh2_ref, gf_ref, wo_ref)
  fc = D_MODEL
  acc = jnp.zeros(x1.shape, F32)
  for f in range(D_FF // fc):
    acc = acc + _ffn_piece(h2, wu_ref, wd_ref, f * fc, (f + 1) * fc)
  y_ref[0] = _ffn_post(x1, acc, ga2_ref, gl_ref)


def _out_ffn(x3, att, ssd, ga1, sc2, sh2, ga2, g_ffn, g_final, w_out_b, w_up_b, w_down_b, tm):
  nb, tb, d = x3.shape
  nt = tb // tm
  per_row = ga1.shape[1] != 1
  if per_row:
    mspec = pl.BlockSpec((1, tm, d), lambda b, i: (b, i, 0))
  else:
    mspec = pl.BlockSpec((1, 1, d), lambda b, i: (b, 0, 0))
  tok = lambda w: pl.BlockSpec((1, tm, w), lambda b, i: (b, i, 0))
  const = lambda shape: pl.BlockSpec(shape, lambda b, i: (0, 0), pipeline_mode=pl.Buffered(1))
  return pl.pallas_call(
      _ffn_kernel,
      grid=(nb, nt),
      in_specs=[tok(d), tok(ATT_WIDTH), tok(SSD_WIDTH), mspec, mspec, mspec, mspec,
                const((1, d)), const((1, d)),
                const((d, d)), const((d, D_FF)), const((D_FF, d))],
      out_specs=tok(d),
      out_shape=jax.ShapeDtypeStruct((nb, tb, d), F32),
      compiler_params=pltpu.CompilerParams(
          dimension_semantics=("arbitrary", "arbitrary"), vmem_limit_bytes=VMEM_LIMIT),
      name="out_ffn",
  )(x3, att, ssd, ga1, sc2, sh2, ga2, g_ffn.reshape(1, d), g_final.reshape(1, d),
    w_out_b, w_up_b, w_down_b)


def _ffn_sample_kernel(pt_ref, x_ref, att_ref, ssd_ref, ga1_ref, sc2_ref, sh2_ref, ga2_ref,
                       gf_ref, gl_ref, wo_ref, wu_ref, wd_ref,
                       qb_ref, kn_ref, vn_ref, blast_ref, bnew_ref, lam_ref, g_ref, ck_hbm, cv_hbm,
                       y_ref, o_ref,
                       kbuf, vbuf, sem, m_sc, l_sc, acc_sc, kpad, vpad, facc_sc,
                       *, n_steps, steps_per_seq):
  i = pl.program_id(0)
  seq = i // steps_per_seq
  part = i % steps_per_seq
  rps = ROUNDS_PER_STEP
  n_chains = SOFTMAX_CHAINS
  ahead = ROUND_SLOTS - 1
  last_part = part == steps_per_seq - 1

  def ring_slot(r, c):
    return (r % ROUND_SLOTS) * n_chains + c

  def round_copies(r):
    if r < rps:
      sq, rnd = seq, part * rps + r
    else:
      nxt = i + 1
      sq, rnd = nxt // steps_per_seq, (nxt % steps_per_seq) * rps + r - rps
    copies = []
    for c in range(n_chains):
      sl = ring_slot(r, c)
      for p in range(PAGES_PER_GROUP):
        pg = pt_ref[sq, (rnd * n_chains + c) * PAGES_PER_GROUP + p]
        copies.append((pltpu.make_async_copy(ck_hbm.at[pg], kbuf.at[sl, p], sem.at[0, sl]), 0))
        copies.append((pltpu.make_async_copy(cv_hbm.at[pg], vbuf.at[sl, p], sem.at[1, sl]), 1))
    return copies

  def start_round(r):
    for cp, prio in round_copies(r):
      cp.start(priority=prio)

  @pl.when(i == 0)
  def _():
    kpad[...] = jnp.zeros(kpad.shape, kpad.dtype)
    vpad[...] = jnp.zeros(vpad.shape, vpad.dtype)
    for r in range(ahead):
      start_round(r)

  @pl.when(part == 0)
  def _():
    m_sc[...] = jnp.full(m_sc.shape, NEG, F32)
    l_sc[...] = jnp.zeros(l_sc.shape, F32)
    acc_sc[...] = jnp.zeros(acc_sc.shape, F32)

  qb = qb_ref[0]
  rows_per_head = 2 * SUBLANES

  def softmax_probs(c, s):
    m_prev = m_sc[c]
    m_new = jnp.maximum(m_prev, jnp.max(s, axis=-1, keepdims=True))
    alpha = jnp.exp2(m_prev - m_new)
    p = jnp.exp2(s - m_new[:, 0:1])
    l_sc[c] = l_sc[c] * alpha + jnp.sum(p, axis=-1, keepdims=True)
    m_sc[c] = m_new
    return alpha, p.astype(BF16)

  def accumulate(c, alpha, pb, values_of_head):
    pv = [_dot(pb[h * rows_per_head:(h + 1) * rows_per_head, :], values_of_head(h))
          for h in range(ATT_HEADS)]
    acc_sc[c] = acc_sc[c] * alpha + jnp.concatenate(pv, axis=0)

  def group_scores(r, c):
    sl = ring_slot(r, c)
    kt = jnp.concatenate([kbuf[sl, p].astype(BF16) for p in range(PAGES_PER_GROUP)], axis=1)
    s = _dot(qb, kt)
    if r == rps - 1 and c == n_chains - 1:
      tail = s[:, -PAGE_SIZE:] + jnp.where(last_part, blast_ref[...], 0.0)
      s = jnp.concatenate([s[:, :-PAGE_SIZE], tail], axis=1)
    return s

  def group_values(r, c):
    sl = ring_slot(r, c)

    def values_of_head(h):
      v = [vbuf[sl, p, pl.ds(h, PAGE_SIZE, stride=ATT_HEADS), :] for p in range(PAGES_PER_GROUP)]
      return jnp.concatenate(v, axis=0).astype(BF16)

    return values_of_head

  x1, h2 = _ffn_pre(x_ref, att_ref, ssd_ref, ga1_ref, sc2_ref, sh2_ref, gf_ref, wo_ref)
  fc = D_FF // rps
  for r in range(rps):
    for cp, _ in round_copies(r):
      cp.wait()
    if r + ahead < rps:
      start_round(r + ahead)
    else:
      pl.when(i + 1 < n_steps)(functools.partial(start_round, r + ahead))
    s = [group_scores(r, c) for c in range(n_chains)]
    vals = [[group_values(r, c)(h) for h in range(ATT_HEADS)] for c in range(n_chains)]
    u = jnp.maximum(_dot(h2, wu_ref[:, r * fc:(r + 1) * fc]), 0.0)
    ub = (u * u).astype(BF16)
    probs = [softmax_probs(c, s[c]) for c in range(n_chains)]
    for c in range(n_chains):
      accumulate(c, *probs[c], vals[c].__getitem__)
    piece = _dot(ub, wd_ref[r * fc:(r + 1) * fc, :])
    facc_sc[...] = piece if r == 0 else facc_sc[...] + piece
  y_ref[0] = _ffn_post(x1, facc_sc[...], ga2_ref, gl_ref)

  @pl.when(last_part)
  def _():
    for c in range(1, n_chains):
      m0, mc = m_sc[0], m_sc[c]
      m = jnp.maximum(m0, mc)
      a0, ac = jnp.exp2(m0 - m), jnp.exp2(mc - m)
      l_sc[0] = l_sc[0] * a0 + l_sc[c] * ac
      acc_sc[0] = acc_sc[0] * a0 + acc_sc[c] * ac
      m_sc[0] = m
    t_new = kn_ref.shape[1]
    kpad[0:t_new, :] = kn_ref[0].astype(BF16)
    vpad[0:t_new, :] = vn_ref[0].astype(BF16)
    sn = _dot_nt(qb, kpad[...]) + bnew_ref[...]
    accumulate(0, *softmax_probs(0, sn), lambda h: vpad[:, h * ATT_V_DIM:(h + 1) * ATT_V_DIM])
    o = acc_sc[0] / l_sc[0]
    lam = _diff_lambda(lam_ref)
    for h in range(ATT_HEADS):
      o0 = o[(2 * h) * SUBLANES:(2 * h + 1) * SUBLANES, :]
      o1 = o[(2 * h + 1) * SUBLANES:(2 * h + 2) * SUBLANES, :]
      o_ref[0, :, h * ATT_V_DIM:(h + 1) * ATT_V_DIM] = _diff_combine(
          o0, o1, lam, g_ref[...]).astype(o_ref.dtype)


def _ffn_with_sample_attention(x_p, att_p, ssd_p, ga1, sc2, sh2, ga2, g_ffn, g_final,
                               w_out_b, w_up_b, w_down_b,
                               page_table, qb, k_new, v_new, blast, bnew, lam4, subln_g,
                               cache_k, cache_v):
  nb, tb, d = x_p.shape
  tm = FFN_TM
  tiles_per_b = tb // tm
  n_steps = nb * tiles_per_b
  n_seq, n_pages = page_table.shape
  t_new = k_new.shape[1]
  pages_per_step = PAGES_PER_GROUP * SOFTMAX_CHAINS * ROUNDS_PER_STEP
  steps_per_seq = n_pages // pages_per_step
  ring = ROUND_SLOTS * SOFTMAX_CHAINS
  assert t_new == SUBLANES and n_pages % pages_per_step == 0
  assert n_seq * steps_per_seq == n_steps, "one batch of page rounds per FFN tile"
  assert ROUNDS_PER_STEP % ROUND_SLOTS == 0, "ring slots must be static per round"
  w = ATT_WIDTH
  assert cache_k.shape[1:] == (w, PAGE_SIZE) and cache_v.shape[1:] == (PAGE_SIZE * ATT_HEADS, ATT_V_DIM)
  tok = lambda width: pl.BlockSpec((1, tm, width),
                                   lambda i, pt: (i // tiles_per_b, i % tiles_per_b, 0))
  mspec = pl.BlockSpec((1, 1, d), lambda i, pt: (i // tiles_per_b, 0, 0))
  const = lambda shape: pl.BlockSpec(shape, lambda i, pt: (0,) * len(shape),
                                     pipeline_mode=pl.Buffered(1))
  per_seq = lambda r: pl.BlockSpec((1, r, w), lambda i, pt: (i // steps_per_seq, 0, 0))
  grid_spec = pltpu.PrefetchScalarGridSpec(
      num_scalar_prefetch=1,
      grid=(n_steps,),
      in_specs=[tok(d), tok(ATT_WIDTH), tok(SSD_WIDTH), mspec, mspec, mspec, mspec,
                const((1, d)), const((1, d)),
                const((d, d)), const((d, D_FF)), const((D_FF, d)),
                per_seq(8 * SUBLANES), per_seq(t_new), per_seq(t_new),
                const((8 * SUBLANES, LANES)), const((8 * SUBLANES, LANES)),
                const((4, ATT_HEAD_DIM)), const((1, ATT_V_DIM)),
                pl.BlockSpec(memory_space=pl.ANY),
                pl.BlockSpec(memory_space=pl.ANY)],
      out_specs=[tok(d), per_seq(t_new)],
      scratch_shapes=[pltpu.VMEM((ring, PAGES_PER_GROUP) + cache_k.shape[1:], F32),
                      pltpu.VMEM((ring, PAGES_PER_GROUP) + cache_v.shape[1:], F32),
                      pltpu.SemaphoreType.DMA((2, ring)),
                      pltpu.VMEM((SOFTMAX_CHAINS, 8 * SUBLANES, LANES), F32),
                      pltpu.VMEM((SOFTMAX_CHAINS, 8 * SUBLANES, LANES), F32),
                      pltpu.VMEM((SOFTMAX_CHAINS, 8 * SUBLANES, ATT_V_DIM), F32),
                      pltpu.VMEM((PAGE_SIZE, w), BF16),
                      pltpu.VMEM((PAGE_SIZE, w), BF16),
                      pltpu.VMEM((tm, d), F32)])
  return pl.pallas_call(
      functools.partial(_ffn_sample_kernel, n_steps=n_steps, steps_per_seq=steps_per_seq),
      grid_spec=grid_spec,
      out_shape=[jax.ShapeDtypeStruct((nb, tb, d), F32),
                 jax.ShapeDtypeStruct((n_seq, t_new, w), BF16)],
      compiler_params=pltpu.CompilerParams(
          dimension_semantics=("arbitrary",), vmem_limit_bytes=VMEM_LIMIT),
      name="ffn_sample_attention",
  )(page_table, x_p, att_p, ssd_p, ga1, sc2, sh2, ga2, g_ffn.reshape(1, d), g_final.reshape(1, d),
    w_out_b, w_up_b, w_down_b, qb, k_new, v_new, blast, bnew, lam4, subln_g, cache_k, cache_v)


def kernel(x_prompt, x_sample, cache_k, cache_v, state_conv, state_ssm, page_table, c_prompt,
           c_sample, rel_bias, w_ada, b_ada, g_mix, g_ffn, w_in, w_out, lam_q1, lam_k1, lam_q2,
           lam_k2, subln_g, conv_w, conv_b, dt_bias, a_log, d_skip, ssd_norm_g, w_up, w_down,
           g_final):
  assert w_ada.shape[0] == 1, "single-layer step"
  bp, sp, d = x_prompt.shape
  bs, ts, _ = x_sample.shape
  n_pool = cache_k.shape[1]

  w_in_p = jnp.pad(w_in[0], ((0, 0), (0, IN_PAD - IN_WIDTH))).astype(BF16)
  w_kt = w_in[0][:, 0:2 * ATT_WIDTH].T.astype(BF16)
  w_out_b = w_out[0].astype(BF16)
  w_up_b = w_up[0].astype(BF16)
  w_down_b = w_down[0].astype(BF16)
  lam4 = jnp.concatenate([lam_q1, lam_k1, lam_q2, lam_k2], axis=0)
  subln = subln_g.reshape(1, ATT_V_DIM)
  dtb_p = jnp.pad(dt_bias, ((0, 0), (0, LANES - SSD_HEADS)))
  alog_p = jnp.pad(a_log, ((0, 0), (0, LANES - SSD_HEADS)))
  dsk_e = jnp.repeat(d_skip[0], SSD_HEAD_DIM).reshape(1, SSD_WIDTH)
  ssd_args = (conv_w[0], conv_b, dtb_p, alog_p, dsk_e, ssd_norm_g)

  mods = _modulation(jnp.concatenate([c_prompt, c_sample], axis=0), w_ada[0], b_ada[0])
  mp = [m.reshape(bp, 1, d) for m in jnp.split(mods[:bp], 6, axis=-1)]
  ms = [jnp.repeat(m, ts, axis=0).reshape(1, bs * ts, d) for m in jnp.split(mods[bp:], 6, axis=-1)]
  near, diag, slast, snew = _bias_tiles(rel_bias)

  qt_fm, k_hm, vt_fm, kt_p, v_p, z_p, xbc_p, dt_p = _in_proj(
      x_prompt, mp[1], mp[0], g_mix[0], w_in_p, w_kt, tm=512, prompt=True)
  att_p = _prompt_attention(qt_fm, k_hm, vt_fm, near, diag, lam4, subln)
  ssd_p, conv_p, h_p = _ssd(
      xbc_p, z_p, dt_p, jnp.zeros((bp, CONV_W - 1, CONV_DIM), F32),
      jnp.zeros((bp, SSD_WIDTH, D_STATE), F32), *ssd_args, tv=SSD_CHUNK, seqs_per_step=4)

  n_tok = bs * ts
  xs3 = x_sample.reshape(1, n_tok, d)
  q_s, k_s, v_s, z_s, xbc_s, dt_s = _in_proj(
      xs3, ms[1], ms[0], g_mix[0], w_in_p, w_kt, tm=n_tok, prompt=False)
  qt = q_s.reshape(ATT_HEADS, bs, ts, 2, ATT_HEAD_DIM).transpose(1, 0, 3, 2, 4)
  qt = qt.reshape(bs, 2 * ATT_HEADS, ts, ATT_HEAD_DIM)
  eye = jnp.eye(2 * ATT_HEADS, dtype=BF16)
  qb = (qt[:, :, :, None, :] * eye[None, :, None, :, None]).reshape(bs, 2 * ATT_HEADS * ts, ATT_WIDTH)
  blast = jnp.repeat(slast, 2, axis=0).reshape(2 * ATT_HEADS * SUBLANES, LANES)
  bnew = jnp.repeat(snew, 2, axis=0).reshape(2 * ATT_HEADS * SUBLANES, LANES)
  y_p, att_s = _ffn_with_sample_attention(
      x_prompt, att_p, ssd_p, mp[2], mp[4], mp[3], mp[5], g_ffn[0], g_final,
      w_out_b, w_up_b, w_down_b,
      page_table, qb, k_s.reshape(bs, ts, ATT_WIDTH), v_s.reshape(bs, ts, ATT_WIDTH),
      blast, bnew, lam4, subln,
      jnp.transpose(cache_k[0], (0, 2, 3, 4, 1)).reshape(n_pool, ATT_WIDTH, PAGE_SIZE),
      cache_v[0].reshape(n_pool, PAGE_SIZE * ATT_HEADS, ATT_V_DIM))
  ssd_s, conv_s, h_s = _ssd(
      xbc_s.reshape(bs, ts, CONV_DIM), z_s.reshape(bs, ts, SSD_WIDTH), dt_s.reshape(bs, ts, LANES),
      state_conv[0], state_ssm[0].reshape(bs, SSD_WIDTH, D_STATE), *ssd_args, tv=ts,
      seqs_per_step=4)
  y_s = _out_ffn(xs3, att_s.reshape(1, n_tok, ATT_WIDTH), ssd_s.reshape(1, n_tok, SSD_WIDTH),
                 ms[2], ms[4], ms[3], ms[5], g_ffn[0], g_final, w_out_b, w_up_b, w_down_b, tm=n_tok)

  hshape = (SSD_HEADS, SSD_HEAD_DIM, D_STATE)
  return (y_p, y_s.reshape(bs, ts, d),
          kt_p.reshape(1, bp, ATT_HEADS, 2, ATT_HEAD_DIM, sp).transpose(0, 1, 5, 2, 3, 4),
          v_p.reshape(1, bp, sp, ATT_HEADS, ATT_V_DIM),
          conv_p[None], h_p.reshape(1, bp, *hshape),
          k_s.reshape(1, bs, ts, ATT_HEADS, 2, ATT_HEAD_DIM),
          v_s.reshape(1, bs, ts, ATT_HEADS, ATT_V_DIM),
          conv_s[None], h_s.reshape(1, bs, *hshape))
```

```python
import functools
import itertools
import math

import numpy as np
import jax
import jax.numpy as jnp
from jax import lax
from jax.experimental import pallas as pl
from jax.experimental.pallas import tpu as pltpu

F32 = jnp.float32
BF16 = jnp.bfloat16

D_MODEL = 1024
PAGE_SIZE = 128
ATT_HEADS = 4
ATT_HEAD_DIM = 64
ATT_V_DIM = 2 * ATT_HEAD_DIM
ATT_WIDTH = ATT_HEADS * ATT_V_DIM
SSD_HEADS = 8
SSD_HEAD_DIM = 64
SSD_WIDTH = SSD_HEADS * SSD_HEAD_DIM
SSD_GROUPS = 2
D_STATE = 128
CONV_W = 4
CONV_DIM = SSD_WIDTH + 2 * SSD_GROUPS * D_STATE
SSD_CHUNK = 128
IN_WIDTH = 3 * ATT_WIDTH + SSD_WIDTH + CONV_DIM + SSD_HEADS
D_FF = 4 * D_MODEL
N_BUCKETS = 32
MAX_DISTANCE = 128
MAX_EXACT = N_BUCKETS // 2
EPS = 1e-6
LAM_INIT = 0.8 - 0.6 * math.exp(-0.3 * 0)

LANES = 128
SUBLANES = 8
IN_PAD = 3 * ATT_WIDTH + SSD_WIDTH + CONV_DIM + LANES
LOG2E = 1.4426950408889634
NEG = -1e30
VMEM_LIMIT = 56 * 1024 * 1024

ATT_T = 256
FFN_TM = 256
PAGES_PER_GROUP = 8
SOFTMAX_CHAINS = 1
ROUNDS_PER_STEP = 8
ROUND_SLOTS = 4


def _bucket_lower_bounds():
  d = np.arange(0, 4 * MAX_DISTANCE)
  nf = np.maximum(d, 1).astype(np.float64)
  large = MAX_EXACT + (np.log(nf / MAX_EXACT) / math.log(MAX_DISTANCE / MAX_EXACT)
                       * (N_BUCKETS - MAX_EXACT)).astype(np.int64)
  large = np.minimum(large, N_BUCKETS - 1)
  bucket = np.where(d < MAX_EXACT, d, large)
  return [int(np.argmax(bucket >= b)) for b in range(N_BUCKETS)]


_LOWER = _bucket_lower_bounds()
FAR_DIST = _LOWER[N_BUCKETS - 1]
assert FAR_DIST <= PAGE_SIZE and FAR_DIST <= ATT_T


def _silu(x):
  h = 0.5 * x
  return h * jnp.tanh(h) + h


def _dot(a, b):
  return jnp.dot(a, b, preferred_element_type=F32)


def _dot_nt(a, b):
  return lax.dot_general(a, b, (((1,), (1,)), ((), ())), preferred_element_type=F32)


def _split3(x):
  hi = x.astype(BF16)
  r1 = x - hi.astype(F32)
  mid = r1.astype(BF16)
  lo = (r1 - mid.astype(F32)).astype(BF16)
  return hi, mid, lo


def _mod_kernel(c_ref, w_ref, b_ref, o_ref):
  s = _silu(c_ref[...]).astype(BF16)
  o_ref[...] = _dot(s, w_ref[...].astype(BF16)) + b_ref[...]


def _modulation(c_all, w_ada, b_ada):
  n = c_all.shape[0]
  tn = D_MODEL
  return pl.pallas_call(
      _mod_kernel,
      grid=(6 * D_MODEL // tn,),
      in_specs=[pl.BlockSpec((n, D_MODEL), lambda j: (0, 0)),
                pl.BlockSpec((D_MODEL, tn), lambda j: (0, j)),
                pl.BlockSpec((1, tn), lambda j: (0, j))],
      out_specs=pl.BlockSpec((n, tn), lambda j: (0, j)),
      out_shape=jax.ShapeDtypeStruct((n, 6 * D_MODEL), F32),
      compiler_params=pltpu.CompilerParams(vmem_limit_bytes=VMEM_LIMIT),
      name="modulation",
  )(c_all, w_ada, b_ada.reshape(1, -1))


def _bias_kernel(tab_ref, near_ref, diag_ref, slast_ref, snew_ref):
  def bias_of(dist, h):
    val = jnp.full(dist.shape, tab_ref[0, h], F32)
    for b in range(1, N_BUCKETS):
      val = jnp.where(dist >= _LOWER[b], tab_ref[b, h], val)
    return (val - tab_ref[N_BUCKETS - 1, h]) * LOG2E

  t = ATT_T
  r = lax.broadcasted_iota(jnp.int32, (t, t), 0)
  c = lax.broadcasted_iota(jnp.int32, (t, t), 1)
  r8 = lax.broadcasted_iota(jnp.int32, (SUBLANES, LANES), 0)
  c8 = lax.broadcasted_iota(jnp.int32, (SUBLANES, LANES), 1)
  for h in range(ATT_HEADS):
    near_ref[h] = bias_of(t + r - c, h)
    d = r - c
    diag_ref[h] = jnp.where(d >= 0, bias_of(d, h), NEG)
    slast_ref[h] = bias_of(PAGE_SIZE + r8 - c8, h)
    dn = r8 - c8
    snew_ref[h] = jnp.where(dn >= 0, bias_of(dn, h), NEG)


def _bias_tiles(rel_bias):
  t = ATT_T
  return pl.pallas_call(
      _bias_kernel,
      in_specs=[pl.BlockSpec(memory_space=pltpu.SMEM)],
      out_shape=(jax.ShapeDtypeStruct((ATT_HEADS, t, t), F32),
                 jax.ShapeDtypeStruct((ATT_HEADS, t, t), F32),
                 jax.ShapeDtypeStruct((ATT_HEADS, SUBLANES, LANES), F32),
                 jax.ShapeDtypeStruct((ATT_HEADS, SUBLANES, LANES), F32)),
      compiler_params=pltpu.CompilerParams(vmem_limit_bytes=VMEM_LIMIT),
      name="bias_tiles",
  )(rel_bias)


def _inproj_kernel(x_ref, sc_ref, sh_ref, g_ref, w_ref, wkt_ref, *out_refs, prompt):
  x = x_ref[0]
  var = jnp.mean(x * x, axis=-1, keepdims=True)
  h = x * lax.rsqrt(var + EPS) * g_ref[...]
  h = h * (1.0 + sc_ref[0]) + sh_ref[0]
  hb = h.astype(BF16)
  aw = ATT_WIDTH
  q = _dot(hb, w_ref[:, 0:aw]) * (ATT_HEAD_DIM ** -0.5 * LOG2E)
  v = _dot(hb, w_ref[:, 2 * aw:3 * aw])
  if prompt:
    q_ref, ktb_ref, vb_ref, kt_ref, v_ref, z_ref, xbc_ref, dt_ref = out_refs
    kt = _dot_nt(wkt_ref[...], hb)
    kt_ref[0] = kt
    t = ATT_T
    for hd in range(ATT_HEADS):
      sl = slice(hd * ATT_V_DIM, (hd + 1) * ATT_V_DIM)
      vb_ref[0, hd] = v[:, sl].astype(BF16)
      v_ref[0, pl.ds(hd, v.shape[0], stride=ATT_HEADS), :] = v[:, sl]
      for cc in range(kt.shape[1] // t):
        ktb_ref[0, hd, cc] = kt[sl, cc * t:(cc + 1) * t].astype(BF16)
  else:
    q_ref, k_ref, v_ref, z_ref, xbc_ref, dt_ref = out_refs
    k_ref[0] = _dot(hb, w_ref[:, aw:2 * aw])
    v_ref[0] = v
  for hd in range(ATT_HEADS):
    sl = slice(hd * ATT_V_DIM, (hd + 1) * ATT_V_DIM)
    q_ref[0, hd] = q[:, sl].astype(BF16)
  o = 3 * aw
  z_ref[0] = _dot(hb, w_ref[:, o:o + SSD_WIDTH])
  o += SSD_WIDTH
  xbc_ref[0] = _dot(hb, w_ref[:, o:o + CONV_DIM])
  o += CONV_DIM
  dt_ref[0] = _dot(hb, w_ref[:, o:o + LANES])


def _in_proj(x3, sc, sh, g_mix, w_in_p, w_kt, tm, prompt):
  nb, tb, d = x3.shape
  nt = tb // tm
  per_row = sc.shape[1] != 1
  if per_row:
    mspec = pl.BlockSpec((1, tm, d), lambda b, i: (b, i, 0))
  else:
    mspec = pl.BlockSpec((1, 1, d), lambda b, i: (b, 0, 0))
  hm = pl.BlockSpec((1, ATT_HEADS, tm, ATT_V_DIM), lambda b, i: (b, 0, i, 0))
  tok = lambda w: pl.BlockSpec((1, tm, w), lambda b, i: (b, i, 0))
  hm_shape = jax.ShapeDtypeStruct((nb, ATT_HEADS, tb, ATT_V_DIM), BF16)
  tok_shape = lambda w: jax.ShapeDtypeStruct((nb, tb, w), F32)
  tail_specs = [tok(SSD_WIDTH), tok(CONV_DIM), tok(LANES)]
  tail_shapes = [tok_shape(SSD_WIDTH), tok_shape(CONV_DIM), tok_shape(LANES)]
  if prompt:
    t = ATT_T
    fm = pl.BlockSpec((1, ATT_HEADS, tm // t, ATT_V_DIM, t), lambda b, i: (b, 0, i, 0, 0))
    fm_shape = jax.ShapeDtypeStruct((nb, ATT_HEADS, tb // t, ATT_V_DIM, t), BF16)
    out_specs = [hm, fm, hm,
                 pl.BlockSpec((1, ATT_WIDTH, tm), lambda b, i: (b, 0, i)),
                 pl.BlockSpec((1, tm * ATT_HEADS, ATT_V_DIM), lambda b, i: (b, i, 0))] + tail_specs
    out_shape = [hm_shape, fm_shape, hm_shape,
                 jax.ShapeDtypeStruct((nb, ATT_WIDTH, tb), F32),
                 jax.ShapeDtypeStruct((nb, tb * ATT_HEADS, ATT_V_DIM), F32)] + tail_shapes
  else:
    out_specs = [hm, tok(ATT_WIDTH), tok(ATT_WIDTH)] + tail_specs
    out_shape = [hm_shape, tok_shape(ATT_WIDTH), tok_shape(ATT_WIDTH)] + tail_shapes
  return pl.pallas_call(
      functools.partial(_inproj_kernel, prompt=prompt),
      grid=(nb, nt),
      in_specs=[tok(d), mspec, mspec,
                pl.BlockSpec((1, d), lambda b, i: (0, 0)),
                pl.BlockSpec((d, IN_PAD), lambda b, i: (0, 0)),
                pl.BlockSpec((ATT_WIDTH, d), lambda b, i: (0, 0))],
      out_specs=out_specs,
      out_shape=out_shape,
      compiler_params=pltpu.CompilerParams(
          dimension_semantics=("arbitrary", "arbitrary"), vmem_limit_bytes=VMEM_LIMIT),
      name="in_proj",
  )(x3, sc, sh, g_mix.reshape(1, d), w_in_p, w_kt)


def _diff_lambda(lam_ref):
  lv = lam_ref[...]
  s1 = jnp.sum(lv[0:1] * lv[1:2], axis=-1, keepdims=True)
  s2 = jnp.sum(lv[2:3] * lv[3:4], axis=-1, keepdims=True)
  return jnp.exp(s1) - jnp.exp(s2) + LAM_INIT


def _diff_combine(o0, o1, lam, g):
  d = o0 - lam * o1
  d = d * lax.rsqrt(jnp.mean(d * d, axis=-1, keepdims=True) + EPS) * g
  return d * (1.0 - LAM_INIT)


def _attn_kernel(q_ref, k_ref, v_ref, near_ref, diag_ref, lam_ref, g_ref, o_ref,
                 m_sc, acc_sc):
  t = ATT_T
  qi = pl.program_id(1)
  lane = lax.broadcasted_iota(jnp.int32, (t, ATT_V_DIM), 1)
  m_sc[...] = jnp.full(m_sc.shape, NEG, F32)
  acc_sc[...] = jnp.zeros(acc_sc.shape, F32)

  def scores(h, c, bias_refs):
    q = q_ref[0, h]
    zero = jnp.zeros_like(q)
    q2 = jnp.concatenate([jnp.where(lane < ATT_HEAD_DIM, q, zero),
                          jnp.where(lane >= ATT_HEAD_DIM, q, zero)], axis=0)
    kc = jnp.concatenate([k_ref[0, h, c + j] for j in range(len(bias_refs))], axis=1)
    s = _dot(q2, kc)
    if any(b is not None for b in bias_refs):
      bias = jnp.concatenate([jnp.zeros((t, t), F32) if b is None else b[h] for b in bias_refs],
                             axis=1)
      s = s + jnp.concatenate([bias, bias], axis=0)
    return s

  def softmax_probs(h, s):
    m_prev = m_sc[h]
    m_new = jnp.maximum(m_prev, jnp.max(s, axis=-1, keepdims=True))
    alpha = jnp.exp2(m_prev - m_new)
    p = jnp.exp2(s - jnp.concatenate([m_new] * (s.shape[1] // LANES), axis=1))
    m_sc[h] = m_new
    return alpha, p.astype(BF16)

  def step(c, bias_refs):
    keys = len(bias_refs) * t
    start = pl.multiple_of(c * t, t)
    ones = jnp.ones((keys, LANES), BF16)
    for h in range(ATT_HEADS):
      alpha, pb = softmax_probs(h, scores(h, c, bias_refs))
      vaug = jnp.concatenate([v_ref[0, h, pl.ds(start, keys), :], ones], axis=1)
      acc_sc[h] = acc_sc[h] * jnp.concatenate([alpha, alpha], axis=1) + _dot(pb, vaug)

  n_far = jnp.maximum(qi - 1, 0)

  def far_pair(k, carry):
    step(2 * k, (None, None))
    return carry

  lax.fori_loop(0, n_far // 2, far_pair, 0)

  @pl.when(n_far % 2 == 1)
  def _():
    step(n_far - 1, (None,))

  @pl.when(qi >= 1)
  def _():
    step(qi - 1, (near_ref, diag_ref))

  @pl.when(qi == 0)
  def _():
    step(qi, (diag_ref,))

  lam = _diff_lambda(lam_ref)
  for h in range(ATT_HEADS):
    acc = acc_sc[h]
    o = acc[:, :ATT_V_DIM] / acc[:, ATT_V_DIM:]
    d = _diff_combine(o[:t], o[t:], lam, g_ref[...])
    o_ref[0, :, h * ATT_V_DIM:(h + 1) * ATT_V_DIM] = d.astype(o_ref.dtype)


def _prompt_attention(q_hm, kt_hm, v_hm, near, diag, lam4, subln_g):
  b, nh, s, e = q_hm.shape
  t = ATT_T
  nq = s // t
  return pl.pallas_call(
      _attn_kernel,
      grid=(b, nq),
      in_specs=[pl.BlockSpec((1, nh, t, e), lambda bi, qi: (bi, 0, qi, 0)),
                pl.BlockSpec((1, nh, nq, e, t), lambda bi, qi: (bi, 0, 0, 0, 0)),
                pl.BlockSpec((1, nh, s, e), lambda bi, qi: (bi, 0, 0, 0)),
                pl.BlockSpec((nh, t, t), lambda bi, qi: (0, 0, 0)),
                pl.BlockSpec((nh, t, t), lambda bi, qi: (0, 0, 0)),
                pl.BlockSpec((4, ATT_HEAD_DIM), lambda bi, qi: (0, 0)),
                pl.BlockSpec((1, e), lambda bi, qi: (0, 0))],
      out_specs=pl.BlockSpec((1, t, nh * e), lambda bi, qi: (bi, qi, 0)),
      out_shape=jax.ShapeDtypeStruct((b, s, nh * e), BF16),
      scratch_shapes=[pltpu.VMEM((nh, 2 * t, LANES), F32),
                      pltpu.VMEM((nh, 2 * t, 2 * LANES), F32)],
      compiler_params=pltpu.CompilerParams(
          dimension_semantics=("arbitrary", "arbitrary"), vmem_limit_bytes=VMEM_LIMIT),
      name="prompt_attention",
  )(q_hm, kt_hm, v_hm, near, diag, lam4, subln_g)


def _ssd_kernel(xbc_ref, z_ref, dt_ref, pre_ref, h0_ref, *refs, tv, nc, alternate):
  consts, (ssd_ref, conv_ref, h_ref, xp_sc) = refs[:8], refs[8:]
  hp = SUBLANES
  b = pl.program_id(0)
  c = pl.program_id(1)
  n_seq = xbc_ref.shape[0]

  @pl.when((b == 0) & (c == 0))
  def _():
    xp_sc[...] = jnp.zeros(xp_sc.shape, F32)

  @pl.when(c == 0)
  def _():
    for bb in range(n_seq):
      xp_sc[bb, hp - (CONV_W - 1):hp, :] = pre_ref[bb]
      h_ref[bb] = h0_ref[bb]

  stages = [_ssd_chunk(xbc_ref.at[bb], z_ref.at[bb], dt_ref.at[bb], *consts,
                       ssd_ref.at[bb], h_ref.at[bb], xp_sc.at[bb], tv=tv) for bb in range(n_seq)]
  for _ in (itertools.zip_longest(*stages) if alternate else itertools.chain(*stages)):
    pass

  @pl.when(c == nc - 1)
  def _():
    for bb in range(n_seq):
      conv_ref[bb] = xp_sc[bb, hp + tv - (CONV_W - 1):hp + tv, :]


def _ssd_chunk(xbc_ref, z_ref, dt_ref, cw_ref, cb_ref, dtb_ref, alog_ref, dsk_ref, ng_ref, e_ref,
               tri_ref, ssd_ref, h_ref, xp_sc, *, tv):
  L = SSD_CHUNK
  hp = SUBLANES
  xp_sc[hp:hp + tv, :] = xbc_ref[...]
  cw = cw_ref[...]
  y = cb_ref[...]
  cur = xp_sc[hp:hp + L, :]
  hist = xp_sc[0:hp, :]
  head_row = lax.broadcasted_iota(jnp.int32, (hp, CONV_DIM), 0)
  for i in range(CONV_W):
    back = CONV_W - 1 - i
    if back == 0:
      tap = cur
    else:
      rolled = pltpu.roll(cur, back, 0)
      head = jnp.where(head_row < back, pltpu.roll(hist, back, 0), rolled[0:hp])
      tap = jnp.concatenate([head, rolled[hp:]], axis=0)
    y = y + tap * cw[i:i + 1]
  xa = _silu(y)
  xp_sc[0:hp, :] = xp_sc[tv:tv + hp, :]
  yield

  xs = xa[:, :SSD_WIDTH]
  gw = SSD_GROUPS * D_STATE
  bmat = xa[:, SSD_WIDTH:SSD_WIDTH + gw].astype(BF16)
  cmat = xa[:, SSD_WIDTH + gw:].astype(BF16)

  x = dt_ref[...] + dtb_ref[...]
  dtv = jnp.maximum(x, 0.0) + jnp.log1p(jnp.exp(-jnp.abs(x)))
  if tv < L:
    dtv = jnp.concatenate([dtv, jnp.zeros((L - tv, LANES), F32)], axis=0)
  a = dtv * (-jnp.exp(alog_ref[...]))

  tri = tri_ref[...]
  acs = sum(_dot(tri, part) for part in _split3(a))
  yield
  e = e_ref[...]
  aexp = sum(_dot(part, e) for part in _split3(acs))
  dtexp = _dot(dtv.astype(BF16), e)
  xd = xs * dtexp
  a_last = acs[L - 1:L, :]
  xdd_t = (xd * jnp.exp(aexp[L - 1:L, :] - aexp)).T
  acs_t = acs.T
  yield

  ri = lax.broadcasted_iota(jnp.int32, (L, L), 0)
  ci = lax.broadcasted_iota(jnp.int32, (L, L), 1)
  causal = ri >= ci
  lane = lax.broadcasted_iota(jnp.int32, (L, LANES), 1)
  heads_per_group = SSD_HEADS // SSD_GROUPS
  hprev = h_ref[...]
  hb = hprev.astype(BF16)

  ydiag, yoff, upd = [], [], []
  for g in range(SSD_GROUPS):
    cg = cmat[:, g * D_STATE:(g + 1) * D_STATE]
    bg = bmat[:, g * D_STATE:(g + 1) * D_STATE]
    scores = _dot_nt(cg, bg)
    for pair in range(heads_per_group // 2):
      h0 = g * heads_per_group + 2 * pair
      ms = []
      for h in (h0, h0 + 1):
        col = jnp.broadcast_to(acs[:, h:h + 1], (L, L))
        row = acs_t[h:h + 1, :]
        lm = jnp.exp(jnp.where(causal, col - row, NEG))
        ms.append((scores * lm).astype(BF16))
      xpair = xd[:, h0 * SSD_HEAD_DIM:(h0 + 2) * SSD_HEAD_DIM]
      xbd = jnp.concatenate([jnp.where(lane < SSD_HEAD_DIM, xpair, 0.0),
                             jnp.where(lane >= SSD_HEAD_DIM, xpair, 0.0)], axis=0)
      ydiag.append(_dot(jnp.concatenate(ms, axis=1), xbd.astype(BF16)))
      yield
    rows = slice(g * heads_per_group * SSD_HEAD_DIM, (g + 1) * heads_per_group * SSD_HEAD_DIM)
    yoff.append(_dot_nt(cg, hb[rows, :]))
    upd.append(_dot(xdd_t[rows, :].astype(BF16), bg))
    yield

  y = (jnp.concatenate(ydiag, axis=1) + jnp.concatenate(yoff, axis=1) * jnp.exp(aexp)
       + dsk_ref[...] * xs)
  yield

  decayed = []
  chunk_decay = jnp.exp(a_last)
  for h in range(SSD_HEADS):
    dec = jnp.broadcast_to(chunk_decay[:, h:h + 1], (SSD_HEAD_DIM, D_STATE))
    decayed.append(hprev[h * SSD_HEAD_DIM:(h + 1) * SSD_HEAD_DIM, :] * dec)
  h_ref[...] = jnp.concatenate(decayed, axis=0) + jnp.concatenate(upd, axis=0)
  yield

  yv = y[:tv] * _silu(z_ref[...])
  sq = yv * yv
  half = SSD_WIDTH // SSD_GROUPS
  r0 = lax.rsqrt(jnp.mean(sq[:, :half], axis=-1, keepdims=True) + EPS)
  r1 = lax.rsqrt(jnp.mean(sq[:, half:], axis=-1, keepdims=True) + EPS)
  out = jnp.concatenate([yv[:, :half] * r0, yv[:, half:] * r1], axis=1) * ng_ref[...]
  ssd_ref[...] = out.astype(ssd_ref.dtype)


def _ssd(xbc, z, dt, prefix, h0, conv_w, conv_b, dtb_p, alog_p, dsk_e, norm_g, tv, seqs_per_step):
  nb, tb, _ = xbc.shape
  nc = tb // tv
  ns = seqs_per_step
  assert nb % ns == 0
  e_np = np.zeros((LANES, SSD_WIDTH), np.float32)
  for h in range(SSD_HEADS):
    e_np[h, h * SSD_HEAD_DIM:(h + 1) * SSD_HEAD_DIM] = 1.0
  tri_np = np.tril(np.ones((SSD_CHUNK, SSD_CHUNK), np.float32))
  full = lambda shape: pl.BlockSpec(shape, lambda b, c: (0,) * len(shape))
  tok = lambda w: pl.BlockSpec((ns, tv, w), lambda b, c: (b, c, 0))
  per_b = lambda r, w: pl.BlockSpec((ns, r, w), lambda b, c: (b, 0, 0))
  return pl.pallas_call(
      functools.partial(_ssd_kernel, tv=tv, nc=nc, alternate=tv < SSD_CHUNK),
      grid=(nb // ns, nc),
      in_specs=[tok(CONV_DIM), tok(SSD_WIDTH), tok(LANES),
                per_b(CONV_W - 1, CONV_DIM), per_b(SSD_WIDTH, D_STATE),
                full((CONV_W, CONV_DIM)), full((1, CONV_DIM)), full((1, LANES)), full((1, LANES)),
                full((1, SSD_WIDTH)), full((1, SSD_WIDTH)),
                full((LANES, SSD_WIDTH)), full((SSD_CHUNK, SSD_CHUNK))],
      out_specs=[tok(SSD_WIDTH), per_b(CONV_W - 1, CONV_DIM), per_b(SSD_WIDTH, D_STATE)],
      out_shape=[jax.ShapeDtypeStruct((nb, tb, SSD_WIDTH), BF16),
                 jax.ShapeDtypeStruct((nb, CONV_W - 1, CONV_DIM), F32),
                 jax.ShapeDtypeStruct((nb, SSD_WIDTH, D_STATE), F32)],
      scratch_shapes=[pltpu.VMEM((ns, SSD_CHUNK + SUBLANES, CONV_DIM), F32)],
      compiler_params=pltpu.CompilerParams(
          dimension_semantics=("arbitrary", "arbitrary"), vmem_limit_bytes=VMEM_LIMIT),
      name="ssd_scan",
  )(xbc, z, dt, prefix, h0, conv_w, conv_b, dtb_p, alog_p, dsk_e, norm_g,
    jnp.asarray(e_np, BF16), jnp.asarray(tri_np, BF16))


def _ffn_pre(x_ref, att_ref, ssd_ref, ga1_ref, sc2_ref, sh2_ref, gf_ref, wo_ref):
  mix = _dot(att_ref[0], wo_ref[0:ATT_WIDTH, :]) + _dot(ssd_ref[0], wo_ref[ATT_WIDTH:, :])
  x1 = x_ref[0] + ga1_ref[0] * mix
  var = jnp.mean(x1 * x1, axis=-1, keepdims=True)
  h2 = x1 * lax.rsqrt(var + EPS) * gf_ref[...]
  return x1, (h2 * (1.0 + sc2_ref[0]) + sh2_ref[0]).astype(BF16)


def _ffn_piece(h2, wu_ref, wd_ref, lo, hi):
  u = jnp.maximum(_dot(h2, wu_ref[:, lo:hi]), 0.0)
  return _dot((u * u).astype(BF16), wd_ref[lo:hi, :])


def _ffn_post(x1, acc, ga2_ref, gl_ref):
  x2 = x1 + ga2_ref[0] * acc
  var2 = jnp.mean(x2 * x2, axis=-1, keepdims=True)
  return x2 * lax.rsqrt(var2 + EPS) * gl_ref[...]


def _ffn_kernel(x_ref, att_ref, ssd_ref, ga1_ref, sc2_ref, sh2_ref, ga2_ref, gf_ref, gl_ref,
                wo_ref, wu_ref, wd_ref, y_ref):
  x1, h2 = _ffn_pre(x_ref, att_ref, ssd_ref, ga1_ref, sc2_ref, sh2_ref, gf_ref, wo_ref)
  fc = D_MODEL
  acc = jnp.zeros(x1.shape, F32)
  for f in range(D_FF // fc):
    acc = acc + _ffn_piece(h2, wu_ref, wd_ref, f * fc, (f + 1) * fc)
  y_ref[0] = _ffn_post(x1, acc, ga2_ref, gl_ref)


def _out_ffn(x3, att, ssd, ga1, sc2, sh2, ga2, g_ffn, g_final, w_out_b, w_up_b, w_down_b, tm):
  nb, tb, d = x3.shape
  nt = tb // tm
  per_row = ga1.shape[1] != 1
  if per_row:
    mspec = pl.BlockSpec((1, tm, d), lambda b, i: (b, i, 0))
  else:
    mspec = pl.BlockSpec((1, 1, d), lambda b, i: (b, 0, 0))
  tok = lambda w: pl.BlockSpec((1, tm, w), lambda b, i: (b, i, 0))
  const = lambda shape: pl.BlockSpec(shape, lambda b, i: (0, 0), pipeline_mode=pl.Buffered(1))
  return pl.pallas_call(
      _ffn_kernel,
      grid=(nb, nt),
      in_specs=[tok(d), tok(ATT_WIDTH), tok(SSD_WIDTH), mspec, mspec, mspec, mspec,
                const((1, d)), const((1, d)),
                const((d, d)), const((d, D_FF)), const((D_FF, d))],
      out_specs=tok(d),
      out_shape=jax.ShapeDtypeStruct((nb, tb, d), F32),
      compiler_params=pltpu.CompilerParams(
          dimension_semantics=("arbitrary", "arbitrary"), vmem_limit_bytes=VMEM_LIMIT),
      name="out_ffn",
  )(x3, att, ssd, ga1, sc2, sh2, ga2, g_ffn.reshape(1, d), g_final.reshape(1, d),
    w_out_b, w_up_b, w_down_b)


def _ffn_sample_kernel(pt_ref, x_ref, att_ref, ssd_ref, ga1_ref, sc2_ref, sh2_ref, ga2_ref,
                       gf_ref, gl_ref, wo_ref, wu_ref, wd_ref,
                       qb_ref, kn_ref, vn_ref, blast_ref, bnew_ref, lam_ref, g_ref, ck_hbm, cv_hbm,
                       y_ref, o_ref,
                       kbuf, vbuf, sem, m_sc, l_sc, acc_sc, kpad, vpad, facc_sc,
                       *, n_steps, steps_per_seq):
  i = pl.program_id(0)
  seq = i // steps_per_seq
  part = i % steps_per_seq
  rps = ROUNDS_PER_STEP
  n_chains = SOFTMAX_CHAINS
  ahead = ROUND_SLOTS - 1
  last_part = part == steps_per_seq - 1

  def ring_slot(r, c):
    return (r % ROUND_SLOTS) * n_chains + c

  def round_copies(r):
    if r < rps:
      sq, rnd = seq, part * rps + r
    else:
      nxt = i + 1
      sq, rnd = nxt // steps_per_seq, (nxt % steps_per_seq) * rps + r - rps
    copies = []
    for c in range(n_chains):
      sl = ring_slot(r, c)
      for p in range(PAGES_PER_GROUP):
        pg = pt_ref[sq, (rnd * n_chains + c) * PAGES_PER_GROUP + p]
        copies.append((pltpu.make_async_copy(ck_hbm.at[pg], kbuf.at[sl, p], sem.at[0, sl]), 0))
        copies.append((pltpu.make_async_copy(cv_hbm.at[pg], vbuf.at[sl, p], sem.at[1, sl]), 1))
    return copies

  def start_round(r):
    for cp, prio in round_copies(r):
      cp.start(priority=prio)

  @pl.when(i == 0)
  def _():
    kpad[...] = jnp.zeros(kpad.shape, kpad.dtype)
    vpad[...] = jnp.zeros(vpad.shape, vpad.dtype)
    for r in range(ahead):
      start_round(r)

  @pl.when(part == 0)
  def _():
    m_sc[...] = jnp.full(m_sc.shape, NEG, F32)
    l_sc[...] = jnp.zeros(l_sc.shape, F32)
    acc_sc[...] = jnp.zeros(acc_sc.shape, F32)

  qb = qb_ref[0]
  rows_per_head = 2 * SUBLANES

  def softmax_probs(c, s):
    m_prev = m_sc[c]
    m_new = jnp.maximum(m_prev, jnp.max(s, axis=-1, keepdims=True))
    alpha = jnp.exp2(m_prev - m_new)
    p = jnp.exp2(s - m_new[:, 0:1])
    l_sc[c] = l_sc[c] * alpha + jnp.sum(p, axis=-1, keepdims=True)
    m_sc[c] = m_new
    return alpha, p.astype(BF16)

  def accumulate(c, alpha, pb, values_of_head):
    pv = [_dot(pb[h * rows_per_head:(h + 1) * rows_per_head, :], values_of_head(h))
          for h in range(ATT_HEADS)]
    acc_sc[c] = acc_sc[c] * alpha + jnp.concatenate(pv, axis=0)

  def group_scores(r, c):
    sl = ring_slot(r, c)
    kt = jnp.concatenate([kbuf[sl, p].astype(BF16) for p in range(PAGES_PER_GROUP)], axis=1)
    s = _dot(qb, kt)
    if r == rps - 1 and c == n_chains - 1:
      tail = s[:, -PAGE_SIZE:] + jnp.where(last_part, blast_ref[...], 0.0)
      s = jnp.concatenate([s[:, :-PAGE_SIZE], tail], axis=1)
    return s

  def group_values(r, c):
    sl = ring_slot(r, c)

    def values_of_head(h):
      v = [vbuf[sl, p, pl.ds(h, PAGE_SIZE, stride=ATT_HEADS), :] for p in range(PAGES_PER_GROUP)]
      return jnp.concatenate(v, axis=0).astype(BF16)

    return values_of_head

  x1, h2 = _ffn_pre(x_ref, att_ref, ssd_ref, ga1_ref, sc2_ref, sh2_ref, gf_ref, wo_ref)
  fc = D_FF // rps
  for r in range(rps):
    for cp, _ in round_copies(r):
      cp.wait()
    if r + ahead < rps:
      start_round(r + ahead)
    else:
      pl.when(i + 1 < n_steps)(functools.partial(start_round, r + ahead))
    s = [group_scores(r, c) for c in range(n_chains)]
    vals = [[group_values(r, c)(h) for h in range(ATT_HEADS)] for c in range(n_chains)]
    u = jnp.maximum(_dot(h2, wu_ref[:, r * fc:(r + 1) * fc]), 0.0)
    ub = (u * u).astype(BF16)
    probs = [softmax_probs(c, s[c]) for c in range(n_chains)]
    for c in range(n_chains):
      accumulate(c, *probs[c], vals[c].__getitem__)
    piece = _dot(ub, wd_ref[r * fc:(r + 1) * fc, :])
    facc_sc[...] = piece if r == 0 else facc_sc[...] + piece
  y_ref[0] = _ffn_post(x1, facc_sc[...], ga2_ref, gl_ref)

  @pl.when(last_part)
  def _():
    for c in range(1, n_chains):
      m0, mc = m_sc[0], m_sc[c]
      m = jnp.maximum(m0, mc)
      a0, ac = jnp.exp2(m0 - m), jnp.exp2(mc - m)
      l_sc[0] = l_sc[0] * a0 + l_sc[c] * ac
      acc_sc[0] = acc_sc[0] * a0 + acc_sc[c] * ac
      m_sc[0] = m
    t_new = kn_ref.shape[1]
    kpad[0:t_new, :] = kn_ref[0].astype(BF16)
    vpad[0:t_new, :] = vn_ref[0].astype(BF16)
    sn = _dot_nt(qb, kpad[...]) + bnew_ref[...]
    accumulate(0, *softmax_probs(0, sn), lambda h: vpad[:, h * ATT_V_DIM:(h + 1) * ATT_V_DIM])
    o = acc_sc[0] / l_sc[0]
    lam = _diff_lambda(lam_ref)
    for h in range(ATT_HEADS):
      o0 = o[(2 * h) * SUBLANES:(2 * h + 1) * SUBLANES, :]
      o1 = o[(2 * h + 1) * SUBLANES:(2 * h + 2) * SUBLANES, :]
      o_ref[0, :, h * ATT_V_DIM:(h + 1) * ATT_V_DIM] = _diff_combine(
          o0, o1, lam, g_ref[...]).astype(o_ref.dtype)


def _ffn_with_sample_attention(x_p, att_p, ssd_p, ga1, sc2, sh2, ga2, g_ffn, g_final,
                               w_out_b, w_up_b, w_down_b,
                               page_table, qb, k_new, v_new, blast, bnew, lam4, subln_g,
                               cache_k, cache_v):
  nb, tb, d = x_p.shape
  tm = FFN_TM
  tiles_per_b = tb // tm
  n_steps = nb * tiles_per_b
  n_seq, n_pages = page_table.shape
  t_new = k_new.shape[1]
  pages_per_step = PAGES_PER_GROUP * SOFTMAX_CHAINS * ROUNDS_PER_STEP
  steps_per_seq = n_pages // pages_per_step
  ring = ROUND_SLOTS * SOFTMAX_CHAINS
  assert t_new == SUBLANES and n_pages % pages_per_step == 0
  assert n_seq * steps_per_seq == n_steps, "one batch of page rounds per FFN tile"
  assert ROUNDS_PER_STEP % ROUND_SLOTS == 0, "ring slots must be static per round"
  w = ATT_WIDTH
  assert cache_k.shape[1:] == (w, PAGE_SIZE) and cache_v.shape[1:] == (PAGE_SIZE * ATT_HEADS, ATT_V_DIM)
  tok = lambda width: pl.BlockSpec((1, tm, width),
                                   lambda i, pt: (i // tiles_per_b, i % tiles_per_b, 0))
  mspec = pl.BlockSpec((1, 1, d), lambda i, pt: (i // tiles_per_b, 0, 0))
  const = lambda shape: pl.BlockSpec(shape, lambda i, pt: (0,) * len(shape),
                                     pipeline_mode=pl.Buffered(1))
  per_seq = lambda r: pl.BlockSpec((1, r, w), lambda i, pt: (i // steps_per_seq, 0, 0))
  grid_spec = pltpu.PrefetchScalarGridSpec(
      num_scalar_prefetch=1,
      grid=(n_steps,),
      in_specs=[tok(d), tok(ATT_WIDTH), tok(SSD_WIDTH), mspec, mspec, mspec, mspec,
                const((1, d)), const((1, d)),
                const((d, d)), const((d, D_FF)), const((D_FF, d)),
                per_seq(8 * SUBLANES), per_seq(t_new), per_seq(t_new),
                const((8 * SUBLANES, LANES)), const((8 * SUBLANES, LANES)),
                const((4, ATT_HEAD_DIM)), const((1, ATT_V_DIM)),
                pl.BlockSpec(memory_space=pl.ANY),
                pl.BlockSpec(memory_space=pl.ANY)],
      out_specs=[tok(d), per_seq(t_new)],
      scratch_shapes=[pltpu.VMEM((ring, PAGES_PER_GROUP) + cache_k.shape[1:], F32),
                      pltpu.VMEM((ring, PAGES_PER_GROUP) + cache_v.shape[1:], F32),
                      pltpu.SemaphoreType.DMA((2, ring)),
                      pltpu.VMEM((SOFTMAX_CHAINS, 8 * SUBLANES, LANES), F32),
                      pltpu.VMEM((SOFTMAX_CHAINS, 8 * SUBLANES, LANES), F32),
                      pltpu.VMEM((SOFTMAX_CHAINS, 8 * SUBLANES, ATT_V_DIM), F32),
                      pltpu.VMEM((PAGE_SIZE, w), BF16),
                      pltpu.VMEM((PAGE_SIZE, w), BF16),
                      pltpu.VMEM((tm, d), F32)])
  return pl.pallas_call(
      functools.partial(_ffn_sample_kernel, n_steps=n_steps, steps_per_seq=steps_per_seq),
      grid_spec=grid_spec,
      out_shape=[jax.ShapeDtypeStruct((nb, tb, d), F32),
                 jax.ShapeDtypeStruct((n_seq, t_new, w), BF16)],
      compiler_params=pltpu.CompilerParams(
          dimension_semantics=("arbitrary",), vmem_limit_bytes=VMEM_LIMIT),
      name="ffn_sample_attention",
  )(page_table, x_p, att_p, ssd_p, ga1, sc2, sh2, ga2, g_ffn.reshape(1, d), g_final.reshape(1, d),
    w_out_b, w_up_b, w_down_b, qb, k_new, v_new, blast, bnew, lam4, subln_g, cache_k, cache_v)


def kernel(x_prompt, x_sample, cache_k, cache_v, state_conv, state_ssm, page_table, c_prompt,
           c_sample, rel_bias, w_ada, b_ada, g_mix, g_ffn, w_in, w_out, lam_q1, lam_k1, lam_q2,
           lam_k2, subln_g, conv_w, conv_b, dt_bias, a_log, d_skip, ssd_norm_g, w_up, w_down,
           g_final):
  assert w_ada.shape[0] == 1, "single-layer step"
  bp, sp, d = x_prompt.shape
  bs, ts, _ = x_sample.shape
  n_pool = cache_k.shape[1]

  w_in_p = jnp.pad(w_in[0], ((0, 0), (0, IN_PAD - IN_WIDTH))).astype(BF16)
  w_kt = w_in[0][:, ATT_WIDTH:2 * ATT_WIDTH].T.astype(BF16)
  w_out_b = w_out[0].astype(BF16)
  w_up_b = w_up[0].astype(BF16)
  w_down_b = w_down[0].astype(BF16)
  lam4 = jnp.concatenate([lam_q1, lam_k1, lam_q2, lam_k2], axis=0)
  subln = subln_g.reshape(1, ATT_V_DIM)
  dtb_p = jnp.pad(dt_bias, ((0, 0), (0, LANES - SSD_HEADS)))
  alog_p = jnp.pad(a_log, ((0, 0), (0, LANES - SSD_HEADS)))
  dsk_e = jnp.repeat(d_skip[0], SSD_HEAD_DIM).reshape(1, SSD_WIDTH)
  ssd_args = (conv_w[0], conv_b, dtb_p, alog_p, dsk_e, ssd_norm_g)

  mods = _modulation(jnp.concatenate([c_prompt, c_sample], axis=0), w_ada[0], b_ada[0])
  mp = [m.reshape(bp, 1, d) for m in jnp.split(mods[:bp], 6, axis=-1)]
  ms = [jnp.repeat(m, ts, axis=0).reshape(1, bs * ts, d) for m in jnp.split(mods[bp:], 6, axis=-1)]
  near, diag, slast, snew = _bias_tiles(rel_bias)

  q_hm, kt_hm, v_hm, kt_p, v_p, z_p, xbc_p, dt_p = _in_proj(
      x_prompt, mp[1], mp[0], g_mix[0], w_in_p, w_kt, tm=512, prompt=True)
  att_p = _prompt_attention(q_hm, kt_hm, v_hm, near, diag, lam4, subln)
  ssd_p, conv_p, h_p = _ssd(
      xbc_p, z_p, dt_p, jnp.zeros((bp, CONV_W - 1, CONV_DIM), F32),
      jnp.zeros((bp, SSD_WIDTH, D_STATE), F32), *ssd_args, tv=SSD_CHUNK, seqs_per_step=4)

  n_tok = bs * ts
  xs3 = x_sample.reshape(1, n_tok, d)
  q_s, k_s, v_s, z_s, xbc_s, dt_s = _in_proj(
      xs3, ms[1], ms[0], g_mix[0], w_in_p, w_kt, tm=n_tok, prompt=False)
  qt = q_s.reshape(ATT_HEADS, bs, ts, 2, ATT_HEAD_DIM).transpose(1, 0, 3, 2, 4)
  qt = qt.reshape(bs, 2 * ATT_HEADS, ts, ATT_HEAD_DIM)
  eye = jnp.eye(2 * ATT_HEADS, dtype=BF16)
  qb = (qt[:, :, :, None, :] * eye[None, :, None, :, None]).reshape(bs, 2 * ATT_HEADS * ts, ATT_WIDTH)
  blast = jnp.repeat(slast, 2, axis=0).reshape(2 * ATT_HEADS * SUBLANES, LANES)
  bnew = jnp.repeat(snew, 2, axis=0).reshape(2 * ATT_HEADS * SUBLANES, LANES)
  y_p, att_s = _ffn_with_sample_attention(
      x_prompt, att_p, ssd_p, mp[2], mp[4], mp[3], mp[5], g_ffn[0], g_final,
      w_out_b, w_up_b, w_down_b,
      page_table, qb, k_s.reshape(bs, ts, ATT_WIDTH), v_s.reshape(bs, ts, ATT_WIDTH),
      blast, bnew, lam4, subln,
      jnp.transpose(cache_k[0], (0, 2, 3, 4, 1)).reshape(n_pool, ATT_WIDTH, PAGE_SIZE),
      cache_v[0].reshape(n_pool, PAGE_SIZE * ATT_HEADS, ATT_V_DIM))
  ssd_s, conv_s, h_s = _ssd(
      xbc_s.reshape(bs, ts, CONV_DIM), z_s.reshape(bs, ts, SSD_WIDTH), dt_s.reshape(bs, ts, LANES),
      state_conv[0], state_ssm[0].reshape(bs, SSD_WIDTH, D_STATE), *ssd_args, tv=ts,
      seqs_per_step=4)
  y_s = _out_ffn(xs3, att_s.reshape(1, n_tok, ATT_WIDTH), ssd_s.reshape(1, n_tok, SSD_WIDTH),
                 ms[2], ms[4], ms[3], ms[5], g_ffn[0], g_final, w_out_b, w_up_b, w_down_b, tm=n_tok)

  hshape = (SSD_HEADS, SSD_HEAD_DIM, D_STATE)
  return (y_p, y_s.reshape(bs, ts, d),
          kt_p.reshape(1, bp, ATT_HEADS, 2, ATT_HEAD_DIM, sp).transpose(0, 1, 5, 2, 3, 4),
          v_p.reshape(1, bp, sp, ATT_HEADS, ATT_V_DIM),
          conv_p[None], h_p.reshape(1, bp, *hshape),
          k_s.reshape(1, bs, ts, ATT_HEADS, 2, ATT_HEAD_DIM),
          v_s.reshape(1, bs, ts, ATT_HEADS, ATT_V_DIM),
          conv_s[None], h_s.reshape(1, bs, *hshape))
```

```python
import functools
import itertools
import math

import numpy as np
import jax
import jax.numpy as jnp
from jax import lax
from jax.experimental import pallas as pl
from jax.experimental.pallas import tpu as pltpu

F32 = jnp.float32
BF16 = jnp.bfloat16

D_MODEL = 1024
PAGE_SIZE = 128
ATT_HEADS = 4
ATT_HEAD_DIM = 64
ATT_V_DIM = 2 * ATT_HEAD_DIM
ATT_WIDTH = ATT_HEADS * ATT_V_DIM
SSD_HEADS = 8
SSD_HEAD_DIM = 64
SSD_WIDTH = SSD_HEADS * SSD_HEAD_DIM
SSD_GROUPS = 2
D_STATE = 128
CONV_W = 4
CONV_DIM = SSD_WIDTH + 2 * SSD_GROUPS * D_STATE
SSD_CHUNK = 128
IN_WIDTH = 3 * ATT_WIDTH + SSD_WIDTH + CONV_DIM + SSD_HEADS
D_FF = 4 * D_MODEL
N_BUCKETS = 32
MAX_DISTANCE = 128
MAX_EXACT = N_BUCKETS // 2
EPS = 1e-6
LAM_INIT = 0.8 - 0.6 * math.exp(-0.3 * 0)
MOD_SH1, MOD_SC1, MOD_GA1, MOD_SH2, MOD_SC2, MOD_GA2 = range(6)

LANES = 128
SUBLANES = 8
IN_PAD = 3 * ATT_WIDTH + SSD_WIDTH + CONV_DIM + LANES
LOG2E = 1.4426950408889634
NEG = -1e30
VMEM_LIMIT = 56 * 1024 * 1024

ATT_T = 256
FFN_TM = 256
PAGES_PER_GROUP = 8
ROUNDS_PER_STEP = 8
ROUND_SLOTS = 4


def _bucket_lower_bounds():
  d = np.arange(0, 4 * MAX_DISTANCE)
  nf = np.maximum(d, 1).astype(np.float64)
  large = MAX_EXACT + (np.log(nf / MAX_EXACT) / math.log(MAX_DISTANCE / MAX_EXACT)
                       * (N_BUCKETS - MAX_EXACT)).astype(np.int64)
  large = np.minimum(large, N_BUCKETS - 1)
  bucket = np.where(d < MAX_EXACT, d, large)
  return [int(np.argmax(bucket >= b)) for b in range(N_BUCKETS)]


_LOWER = _bucket_lower_bounds()
FAR_DIST = _LOWER[N_BUCKETS - 1]
assert FAR_DIST <= PAGE_SIZE and FAR_DIST <= ATT_T


def _silu(x):
  h = 0.5 * x
  return h * jnp.tanh(h) + h


def _dot(a, b):
  return jnp.dot(a, b, preferred_element_type=F32)


def _dot_nt(a, b):
  return lax.dot_general(a, b, (((1,), (1,)), ((), ())), preferred_element_type=F32)


def _split3(x):
  hi = x.astype(BF16)
  r1 = x - hi.astype(F32)
  mid = r1.astype(BF16)
  lo = (r1 - mid.astype(F32)).astype(BF16)
  return hi, mid, lo


def _mod_kernel(c_ref, w_ref, b_ref, o_ref):
  s = _silu(c_ref[...]).astype(BF16)
  o_ref[...] = _dot(s, w_ref[...].astype(BF16)) + b_ref[...]


def _modulation(c_all, w_ada, b_ada):
  n = c_all.shape[0]
  tn = D_MODEL
  return pl.pallas_call(
      _mod_kernel,
      grid=(6 * D_MODEL // tn,),
      in_specs=[pl.BlockSpec((n, D_MODEL), lambda j: (0, 0)),
                pl.BlockSpec((D_MODEL, tn), lambda j: (0, j)),
                pl.BlockSpec((1, tn), lambda j: (0, j))],
      out_specs=pl.BlockSpec((n, tn), lambda j: (0, j)),
      out_shape=jax.ShapeDtypeStruct((n, 6 * D_MODEL), F32),
      compiler_params=pltpu.CompilerParams(vmem_limit_bytes=VMEM_LIMIT),
      name="modulation",
  )(c_all, w_ada, b_ada.reshape(1, -1))


def _bias_kernel(tab_ref, near_ref, diag_ref, slast_ref, snew_ref):
  def bias_of(dist, h):
    val = jnp.full(dist.shape, tab_ref[0, h], F32)
    for b in range(1, N_BUCKETS):
      val = jnp.where(dist >= _LOWER[b], tab_ref[b, h], val)
    return (val - tab_ref[N_BUCKETS - 1, h]) * LOG2E

  t = ATT_T
  r = lax.broadcasted_iota(jnp.int32, (t, t), 0)
  c = lax.broadcasted_iota(jnp.int32, (t, t), 1)
  r8 = lax.broadcasted_iota(jnp.int32, (SUBLANES, LANES), 0)
  c8 = lax.broadcasted_iota(jnp.int32, (SUBLANES, LANES), 1)
  for h in range(ATT_HEADS):
    near_ref[h] = bias_of(t + r - c, h)
    d = r - c
    diag_ref[h] = jnp.where(d >= 0, bias_of(d, h), NEG)
    slast_ref[h] = bias_of(PAGE_SIZE + r8 - c8, h)
    dn = r8 - c8
    snew_ref[h] = jnp.where(dn >= 0, bias_of(dn, h), NEG)


def _bias_tiles(rel_bias):
  t = ATT_T
  return pl.pallas_call(
      _bias_kernel,
      in_specs=[pl.BlockSpec(memory_space=pltpu.SMEM)],
      out_shape=(jax.ShapeDtypeStruct((ATT_HEADS, t, t), F32),
                 jax.ShapeDtypeStruct((ATT_HEADS, t, t), F32),
                 jax.ShapeDtypeStruct((ATT_HEADS, SUBLANES, LANES), F32),
                 jax.ShapeDtypeStruct((ATT_HEADS, SUBLANES, LANES), F32)),
      compiler_params=pltpu.CompilerParams(vmem_limit_bytes=VMEM_LIMIT),
      name="bias_tiles",
  )(rel_bias)


def _inproj_kernel(x_ref, sc_ref, sh_ref, g_ref, w_ref, wkt_ref, *out_refs, prompt):
  x = x_ref[0]
  var = jnp.mean(x * x, axis=-1, keepdims=True)
  h = x * lax.rsqrt(var + EPS) * g_ref[...]
  h = h * (1.0 + sc_ref[0]) + sh_ref[0]
  hb = h.astype(BF16)
  aw = ATT_WIDTH
  q = _dot(hb, w_ref[:, 0:aw]) * (ATT_HEAD_DIM ** -0.5 * LOG2E)
  v = _dot(hb, w_ref[:, 2 * aw:3 * aw])
  if prompt:
    q_ref, ktb_ref, vb_ref, kt_ref, v_ref, z_ref, xbc_ref, dt_ref = out_refs
    kt = _dot_nt(wkt_ref[...], hb)
    kt_ref[0] = kt
    t = ATT_T
    for hd in range(ATT_HEADS):
      sl = slice(hd * ATT_V_DIM, (hd + 1) * ATT_V_DIM)
      vb_ref[0, hd] = v[:, sl].astype(BF16)
      v_ref[0, pl.ds(hd, v.shape[0], stride=ATT_HEADS), :] = v[:, sl]
      q_ref[0, hd] = q[:, sl].astype(BF16)
      for cc in range(kt.shape[1] // t):
        ktb_ref[0, hd, cc] = kt[sl, cc * t:(cc + 1) * t].astype(BF16)
  else:
    q_ref, k_ref, v_ref, z_ref, xbc_ref, dt_ref = out_refs
    q_ref[0] = q
    k_ref[0] = _dot(hb, w_ref[:, aw:2 * aw])
    v_ref[0] = v
  o = 3 * aw
  z_ref[0] = _dot(hb, w_ref[:, o:o + SSD_WIDTH])
  o += SSD_WIDTH
  xbc_ref[0] = _dot(hb, w_ref[:, o:o + CONV_DIM])
  o += CONV_DIM
  dt_ref[0] = _dot(hb, w_ref[:, o:o + LANES])


def _mod_spec(mods, tm, col, b_of, i_of):
  d = mods.shape[2] // 6
  if mods.shape[1] != 1:
    return pl.BlockSpec((1, tm, d), lambda *g: (b_of(*g), i_of(*g), col))
  return pl.BlockSpec((1, 1, d), lambda *g: (b_of(*g), 0, col))


def _in_proj(x3, mods, g_mix, w_in_p, w_kt, tm, prompt):
  nb, tb, d = x3.shape
  nt = tb // tm
  mspec = lambda col: _mod_spec(mods, tm, col, lambda b, i: b, lambda b, i: i)
  hm = pl.BlockSpec((1, ATT_HEADS, tm, ATT_V_DIM), lambda b, i: (b, 0, i, 0))
  tok = lambda w: pl.BlockSpec((1, tm, w), lambda b, i: (b, i, 0))
  hm_shape = jax.ShapeDtypeStruct((nb, ATT_HEADS, tb, ATT_V_DIM), BF16)
  tok_shape = lambda w: jax.ShapeDtypeStruct((nb, tb, w), F32)
  tail_specs = [tok(SSD_WIDTH), tok(CONV_DIM), tok(LANES)]
  tail_shapes = [tok_shape(SSD_WIDTH), tok_shape(CONV_DIM), tok_shape(LANES)]
  if prompt:
    t = ATT_T
    fm = pl.BlockSpec((1, ATT_HEADS, tm // t, ATT_V_DIM, t), lambda b, i: (b, 0, i, 0, 0))
    fm_shape = jax.ShapeDtypeStruct((nb, ATT_HEADS, tb // t, ATT_V_DIM, t), BF16)
    out_specs = [hm, fm, hm,
                 pl.BlockSpec((1, ATT_WIDTH, tm), lambda b, i: (b, 0, i)),
                 pl.BlockSpec((1, tm * ATT_HEADS, ATT_V_DIM), lambda b, i: (b, i, 0))] + tail_specs
    out_shape = [hm_shape, fm_shape, hm_shape,
                 jax.ShapeDtypeStruct((nb, ATT_WIDTH, tb), F32),
                 jax.ShapeDtypeStruct((nb, tb * ATT_HEADS, ATT_V_DIM), F32)] + tail_shapes
  else:
    out_specs = [tok(ATT_WIDTH)] * 3 + tail_specs
    out_shape = [tok_shape(ATT_WIDTH)] * 3 + tail_shapes
  return pl.pallas_call(
      functools.partial(_inproj_kernel, prompt=prompt),
      grid=(nb, nt),
      in_specs=[tok(d), mspec(MOD_SC1), mspec(MOD_SH1),
                pl.BlockSpec((1, d), lambda b, i: (0, 0)),
                pl.BlockSpec((d, IN_PAD), lambda b, i: (0, 0)),
                pl.BlockSpec((ATT_WIDTH, d), lambda b, i: (0, 0))],
      out_specs=out_specs,
      out_shape=out_shape,
      compiler_params=pltpu.CompilerParams(
          dimension_semantics=("arbitrary", "arbitrary"), vmem_limit_bytes=VMEM_LIMIT),
      name="in_proj",
  )(x3, mods, mods, g_mix.reshape(1, d), w_in_p, w_kt)


def _diff_lambda(lam_ref):
  lv = lam_ref[...]
  s1 = jnp.sum(lv[0:1] * lv[1:2], axis=-1, keepdims=True)
  s2 = jnp.sum(lv[2:3] * lv[3:4], axis=-1, keepdims=True)
  return jnp.exp(s1) - jnp.exp(s2) + LAM_INIT


def _diff_combine(o0, o1, lam, g):
  d = o0 - lam * o1
  d = d * lax.rsqrt(jnp.mean(d * d, axis=-1, keepdims=True) + EPS) * g
  return d * (1.0 - LAM_INIT)


def _attn_kernel(q_ref, k_ref, v_ref, near_ref, diag_ref, lam_ref, g_ref, o_ref,
                 m_sc, acc_sc):
  t = ATT_T
  qi = pl.program_id(1)
  lane = lax.broadcasted_iota(jnp.int32, (t, ATT_V_DIM), 1)
  m_sc[...] = jnp.full(m_sc.shape, NEG, F32)
  acc_sc[...] = jnp.zeros(acc_sc.shape, F32)

  def scores(h, c, bias_refs):
    q = q_ref[0, h]
    zero = jnp.zeros_like(q)
    q2 = jnp.concatenate([jnp.where(lane < ATT_HEAD_DIM, q, zero),
                          jnp.where(lane >= ATT_HEAD_DIM, q, zero)], axis=0)
    kc = jnp.concatenate([k_ref[0, h, c + j] for j in range(len(bias_refs))], axis=1)
    s = _dot(q2, kc)
    if any(b is not None for b in bias_refs):
      bias = jnp.concatenate([jnp.zeros((t, t), F32) if b is None else b[h] for b in bias_refs],
                             axis=1)
      s = s + jnp.concatenate([bias, bias], axis=0)
    return s

  def softmax_probs(h, s):
    m_prev = m_sc[h]
    m_new = jnp.maximum(m_prev, jnp.max(s, axis=-1, keepdims=True))
    alpha = jnp.exp2(m_prev - m_new)
    p = jnp.exp2(s - jnp.concatenate([m_new] * (s.shape[1] // LANES), axis=1))
    m_sc[h] = m_new
    return alpha, p.astype(BF16)

  def step(c, bias_refs):
    keys = len(bias_refs) * t
    start = pl.multiple_of(c * t, t)
    ones = jnp.ones((keys, LANES), BF16)
    for h in range(ATT_HEADS):
      alpha, pb = softmax_probs(h, scores(h, c, bias_refs))
      vaug = jnp.concatenate([v_ref[0, h, pl.ds(start, keys), :], ones], axis=1)
      acc_sc[h] = acc_sc[h] * jnp.concatenate([alpha, alpha], axis=1) + _dot(pb, vaug)

  n_far = jnp.maximum(qi - 1, 0)

  def far_pair(k, carry):
    step(2 * k, (None, None))
    return carry

  lax.fori_loop(0, n_far // 2, far_pair, 0)

  @pl.when(n_far % 2 == 1)
  def _():
    step(n_far - 1, (None,))

  @pl.when(qi >= 1)
  def _():
    step(qi - 1, (near_ref, diag_ref))

  @pl.when(qi == 0)
  def _():
    step(qi, (diag_ref,))

  lam = _diff_lambda(lam_ref)
  for h in range(ATT_HEADS):
    acc = acc_sc[h]
    o = acc[:, :ATT_V_DIM] / acc[:, ATT_V_DIM:]
    d = _diff_combine(o[:t], o[t:], lam, g_ref[...])
    o_ref[0, :, h * ATT_V_DIM:(h + 1) * ATT_V_DIM] = d.astype(o_ref.dtype)


def _prompt_attention(q_hm, kt_hm, v_hm, near, diag, lam4, subln_g):
  b, nh, s, e = q_hm.shape
  t = ATT_T
  nq = s // t
  return pl.pallas_call(
      _attn_kernel,
      grid=(b, nq),
      in_specs=[pl.BlockSpec((1, nh, t, e), lambda bi, qi: (bi, 0, qi, 0)),
                pl.BlockSpec((1, nh, nq, e, t), lambda bi, qi: (bi, 0, 0, 0, 0)),
                pl.BlockSpec((1, nh, s, e), lambda bi, qi: (bi, 0, 0, 0)),
                pl.BlockSpec((nh, t, t), lambda bi, qi: (0, 0, 0)),
                pl.BlockSpec((nh, t, t), lambda bi, qi: (0, 0, 0)),
                pl.BlockSpec((4, ATT_HEAD_DIM), lambda bi, qi: (0, 0)),
                pl.BlockSpec((1, e), lambda bi, qi: (0, 0))],
      out_specs=pl.BlockSpec((1, t, nh * e), lambda bi, qi: (bi, qi, 0)),
      out_shape=jax.ShapeDtypeStruct((b, s, nh * e), BF16),
      scratch_shapes=[pltpu.VMEM((nh, 2 * t, LANES), F32),
                      pltpu.VMEM((nh, 2 * t, 2 * LANES), F32)],
      compiler_params=pltpu.CompilerParams(
          dimension_semantics=("arbitrary", "arbitrary"), vmem_limit_bytes=VMEM_LIMIT),
      name="prompt_attention",
  )(q_hm, kt_hm, v_hm, near, diag, lam4, subln_g)


def _ssd_kernel(xbc_ref, z_ref, dt_ref, pre_ref, h0_ref, *refs, tv, nc, alternate):
  consts, (ssd_ref, conv_ref, h_ref, xp_sc) = refs[:8], refs[8:]
  hp = SUBLANES
  b = pl.program_id(0)
  c = pl.program_id(1)
  n_seq = xbc_ref.shape[0]

  @pl.when((b == 0) & (c == 0))
  def _():
    xp_sc[...] = jnp.zeros(xp_sc.shape, F32)

  @pl.when(c == 0)
  def _():
    for bb in range(n_seq):
      xp_sc[bb, hp - (CONV_W - 1):hp, :] = pre_ref[bb]
      h_ref[bb] = h0_ref[bb]

  stages = [_ssd_chunk(xbc_ref.at[bb], z_ref.at[bb], dt_ref.at[bb], *consts,
                       ssd_ref.at[bb], h_ref.at[bb], xp_sc.at[bb], tv=tv) for bb in range(n_seq)]
  for _ in (itertools.zip_longest(*stages) if alternate else itertools.chain(*stages)):
    pass

  @pl.when(c == nc - 1)
  def _():
    for bb in range(n_seq):
      conv_ref[bb] = xp_sc[bb, hp + tv - (CONV_W - 1):hp + tv, :]


def _ssd_chunk(xbc_ref, z_ref, dt_ref, cw_ref, cb_ref, dtb_ref, alog_ref, dsk_ref, ng_ref, e_ref,
               tri_ref, ssd_ref, h_ref, xp_sc, *, tv):
  L = SSD_CHUNK
  hp = SUBLANES
  xp_sc[hp:hp + tv, :] = xbc_ref[...]
  cw = cw_ref[...]
  y = cb_ref[...]
  cur = xp_sc[hp:hp + L, :]
  hist = xp_sc[0:hp, :]
  head_row = lax.broadcasted_iota(jnp.int32, (hp, CONV_DIM), 0)
  for i in range(CONV_W):
    back = CONV_W - 1 - i
    if back == 0:
      tap = cur
    else:
      rolled = pltpu.roll(cur, back, 0)
      head = jnp.where(head_row < back, pltpu.roll(hist, back, 0), rolled[0:hp])
      tap = jnp.concatenate([head, rolled[hp:]], axis=0)
    y = y + tap * cw[i:i + 1]
  xa = _silu(y)
  xp_sc[0:hp, :] = xp_sc[tv:tv + hp, :]
  yield

  xs = xa[:, :SSD_WIDTH]
  gw = SSD_GROUPS * D_STATE
  bmat = xa[:, SSD_WIDTH:SSD_WIDTH + gw].astype(BF16)
  cmat = xa[:, SSD_WIDTH + gw:].astype(BF16)

  x = dt_ref[...] + dtb_ref[...]
  dtv = jnp.maximum(x, 0.0) + jnp.log1p(jnp.exp(-jnp.abs(x)))
  if tv < L:
    dtv = jnp.concatenate([dtv, jnp.zeros((L - tv, LANES), F32)], axis=0)
  a = dtv * (-jnp.exp(alog_ref[...]))

  tri = tri_ref[...]
  acs = sum(_dot(tri, part) for part in _split3(a))
  yield
  e = e_ref[...]
  aexp = sum(_dot(part, e) for part in _split3(acs))
  dtexp = _dot(dtv.astype(BF16), e)
  xd = xs * dtexp
  a_last = acs[L - 1:L, :]
  xdd_t = (xd * jnp.exp(aexp[L - 1:L, :] - aexp)).T
  acs_t = acs.T
  yield

  ri = lax.broadcasted_iota(jnp.int32, (L, L), 0)
  ci = lax.broadcasted_iota(jnp.int32, (L, L), 1)
  causal = ri >= ci
  lane = lax.broadcasted_iota(jnp.int32, (L, LANES), 1)
  heads_per_group = SSD_HEADS // SSD_GROUPS
  hprev = h_ref[...]
  hb = hprev.astype(BF16)

  ydiag, yoff, upd = [], [], []
  for g in range(SSD_GROUPS):
    cg = cmat[:, g * D_STATE:(g + 1) * D_STATE]
    bg = bmat[:, g * D_STATE:(g + 1) * D_STATE]
    scores = _dot_nt(cg, bg)
    for pair in range(heads_per_group // 2):
      h0 = g * heads_per_group + 2 * pair
      ms = []
      for h in (h0, h0 + 1):
        col = jnp.broadcast_to(acs[:, h:h + 1], (L, L))
        row = acs_t[h:h + 1, :]
        lm = jnp.exp(jnp.where(causal, col - row, NEG))
        ms.append((scores * lm).astype(BF16))
      xpair = xd[:, h0 * SSD_HEAD_DIM:(h0 + 2) * SSD_HEAD_DIM]
      xbd = jnp.concatenate([jnp.where(lane < SSD_HEAD_DIM, xpair, 0.0),
                             jnp.where(lane >= SSD_HEAD_DIM, xpair, 0.0)], axis=0)
      ydiag.append(_dot(jnp.concatenate(ms, axis=1), xbd.astype(BF16)))
      yield
    rows = slice(g * heads_per_group * SSD_HEAD_DIM, (g + 1) * heads_per_group * SSD_HEAD_DIM)
    yoff.append(_dot_nt(cg, hb[rows, :]))
    upd.append(_dot(xdd_t[rows, :].astype(BF16), bg))
    yield

  y = (jnp.concatenate(ydiag, axis=1) + jnp.concatenate(yoff, axis=1) * jnp.exp(aexp)
       + dsk_ref[...] * xs)
  yield

  decayed = []
  chunk_decay = jnp.exp(a_last)
  for h in range(SSD_HEADS):
    dec = jnp.broadcast_to(chunk_decay[:, h:h + 1], (SSD_HEAD_DIM, D_STATE))
    decayed.append(hprev[h * SSD_HEAD_DIM:(h + 1) * SSD_HEAD_DIM, :] * dec)
  h_ref[...] = jnp.concatenate(decayed, axis=0) + jnp.concatenate(upd, axis=0)
  yield

  yv = y[:tv] * _silu(z_ref[...])
  sq = yv * yv
  half = SSD_WIDTH // SSD_GROUPS
  r0 = lax.rsqrt(jnp.mean(sq[:, :half], axis=-1, keepdims=True) + EPS)
  r1 = lax.rsqrt(jnp.mean(sq[:, half:], axis=-1, keepdims=True) + EPS)
  out = jnp.concatenate([yv[:, :half] * r0, yv[:, half:] * r1], axis=1) * ng_ref[...]
  ssd_ref[...] = out.astype(ssd_ref.dtype)


def _ssd(xbc, z, dt, prefix, h0, conv_w, conv_b, dtb_p, alog_p, dsk_e, norm_g, tv, seqs_per_step):
  nb, tb, _ = xbc.shape
  nc = tb // tv
  ns = seqs_per_step
  assert nb % ns == 0
  e_np = np.zeros((LANES, SSD_WIDTH), np.float32)
  for h in range(SSD_HEADS):
    e_np[h, h * SSD_HEAD_DIM:(h + 1) * SSD_HEAD_DIM] = 1.0
  tri_np = np.tril(np.ones((SSD_CHUNK, SSD_CHUNK), np.float32))
  full = lambda shape: pl.BlockSpec(shape, lambda b, c: (0,) * len(shape))
  tok = lambda w: pl.BlockSpec((ns, tv, w), lambda b, c: (b, c, 0))
  per_b = lambda r, w: pl.BlockSpec((ns, r, w), lambda b, c: (b, 0, 0))
  return pl.pallas_call(
      functools.partial(_ssd_kernel, tv=tv, nc=nc, alternate=tv < SSD_CHUNK),
      grid=(nb // ns, nc),
      in_specs=[tok(CONV_DIM), tok(SSD_WIDTH), tok(LANES),
                per_b(CONV_W - 1, CONV_DIM), per_b(SSD_WIDTH, D_STATE),
                full((CONV_W, CONV_DIM)), full((1, CONV_DIM)), full((1, LANES)), full((1, LANES)),
                full((1, SSD_WIDTH)), full((1, SSD_WIDTH)),
                full((LANES, SSD_WIDTH)), full((SSD_CHUNK, SSD_CHUNK))],
      out_specs=[tok(SSD_WIDTH), per_b(CONV_W - 1, CONV_DIM), per_b(SSD_WIDTH, D_STATE)],
      out_shape=[jax.ShapeDtypeStruct((nb, tb, SSD_WIDTH), BF16),
                 jax.ShapeDtypeStruct((nb, CONV_W - 1, CONV_DIM), F32),
                 jax.ShapeDtypeStruct((nb, SSD_WIDTH, D_STATE), F32)],
      scratch_shapes=[pltpu.VMEM((ns, SSD_CHUNK + SUBLANES, CONV_DIM), F32)],
      compiler_params=pltpu.CompilerParams(
          dimension_semantics=("arbitrary", "arbitrary"), vmem_limit_bytes=VMEM_LIMIT),
      name="ssd_scan",
  )(xbc, z, dt, prefix, h0, conv_w, conv_b, dtb_p, alog_p, dsk_e, norm_g,
    jnp.asarray(e_np, BF16), jnp.asarray(tri_np, BF16))


def _ffn_pre(x_ref, att_ref, ssd_ref, ga1_ref, sc2_ref, sh2_ref, gf_ref, wo_ref):
  mix = _dot(att_ref[0], wo_ref[0:ATT_WIDTH, :]) + _dot(ssd_ref[0], wo_ref[ATT_WIDTH:, :])
  x1 = x_ref[0] + ga1_ref[0] * mix
  var = jnp.mean(x1 * x1, axis=-1, keepdims=True)
  h2 = x1 * lax.rsqrt(var + EPS) * gf_ref[...]
  return x1, (h2 * (1.0 + sc2_ref[0]) + sh2_ref[0]).astype(BF16)


def _ffn_piece(h2, wu_ref, wd_ref, lo, hi):
  u = jnp.maximum(_dot(h2, wu_ref[:, lo:hi]), 0.0)
  return _dot((u * u).astype(BF16), wd_ref[lo:hi, :])


def _ffn_post(x1, acc, ga2_ref, gl_ref):
  x2 = x1 + ga2_ref[0] * acc
  var2 = jnp.mean(x2 * x2, axis=-1, keepdims=True)
  return x2 * lax.rsqrt(var2 + EPS) * gl_ref[...]


def _ffn_kernel(x_ref, att_ref, ssd_ref, ga1_ref, sc2_ref, sh2_ref, ga2_ref, gf_ref, gl_ref,
                wo_ref, wu_ref, wd_ref, y_ref):
  x1, h2 = _ffn_pre(x_ref, att_ref, ssd_ref, ga1_ref, sc2_ref, sh2_ref, gf_ref, wo_ref)
  fc = D_MODEL
  acc = jnp.zeros(x1.shape, F32)
  for f in range(D_FF // fc):
    acc = acc + _ffn_piece(h2, wu_ref, wd_ref, f * fc, (f + 1) * fc)
  y_ref[0] = _ffn_post(x1, acc, ga2_ref, gl_ref)


def _out_ffn(x3, att, ssd, mods, g_ffn, g_final, w_out_b, w_up_b, w_down_b, tm):
  nb, tb, d = x3.shape
  nt = tb // tm
  mspec = lambda col: _mod_spec(mods, tm, col, lambda b, i: b, lambda b, i: i)
  tok = lambda w: pl.BlockSpec((1, tm, w), lambda b, i: (b, i, 0))
  const = lambda shape: pl.BlockSpec(shape, lambda b, i: (0, 0), pipeline_mode=pl.Buffered(1))
  return pl.pallas_call(
      _ffn_kernel,
      grid=(nb, nt),
      in_specs=[tok(d), tok(ATT_WIDTH), tok(SSD_WIDTH),
                mspec(MOD_GA1), mspec(MOD_SC2), mspec(MOD_SH2), mspec(MOD_GA2),
                const((1, d)), const((1, d)),
                const((d, d)), const((d, D_FF)), const((D_FF, d))],
      out_specs=tok(d),
      out_shape=jax.ShapeDtypeStruct((nb, tb, d), F32),
      compiler_params=pltpu.CompilerParams(
          dimension_semantics=("arbitrary", "arbitrary"), vmem_limit_bytes=VMEM_LIMIT),
      name="out_ffn",
  )(x3, att, ssd, mods, mods, mods, mods, g_ffn.reshape(1, d), g_final.reshape(1, d),
    w_out_b, w_up_b, w_down_b)


def _ffn_sample_kernel(pt_ref, x_ref, att_ref, ssd_ref, ga1_ref, sc2_ref, sh2_ref, ga2_ref,
                       gf_ref, gl_ref, wo_ref, wu_ref, wd_ref,
                       qn_ref, kn_ref, vn_ref, blast_ref, bnew_ref, lam_ref, g_ref, ck_hbm, cv_hbm,
                       y_ref, o_ref,
                       kbuf, vbuf, sem, m_sc, l_sc, acc_sc, kpad, vpad, facc_sc,
                       *, n_steps, steps_per_seq):
  i = pl.program_id(0)
  seq = i // steps_per_seq
  part = i % steps_per_seq
  rps = ROUNDS_PER_STEP
  ahead = ROUND_SLOTS - 1
  last_part = part == steps_per_seq - 1

  def round_copies(r):
    if r < rps:
      sq, rnd = seq, part * rps + r
    else:
      nxt = i + 1
      sq, rnd = nxt // steps_per_seq, (nxt % steps_per_seq) * rps + r - rps
    sl = r % ROUND_SLOTS
    copies = []
    for p in range(PAGES_PER_GROUP):
      pg = pt_ref[sq, rnd * PAGES_PER_GROUP + p]
      copies.append((pltpu.make_async_copy(ck_hbm.at[pg], kbuf.at[sl, p], sem.at[0, sl]), 0))
      copies.append((pltpu.make_async_copy(cv_hbm.at[pg], vbuf.at[sl, p], sem.at[1, sl]), 1))
    return copies

  def start_round(r):
    for cp, prio in round_copies(r):
      cp.start(priority=prio)

  @pl.when(i == 0)
  def _():
    kpad[...] = jnp.zeros(kpad.shape, kpad.dtype)
    vpad[...] = jnp.zeros(vpad.shape, vpad.dtype)
    for r in range(ahead):
      start_round(r)

  @pl.when(part == 0)
  def _():
    m_sc[...] = jnp.full(m_sc.shape, NEG, F32)
    l_sc[...] = jnp.zeros(l_sc.shape, F32)
    acc_sc[...] = jnp.zeros(acc_sc.shape, F32)

  qn = qn_ref[0].astype(BF16)
  half_of_lane = lax.broadcasted_iota(jnp.int32, qn.shape, 1) // ATT_HEAD_DIM
  qb = jnp.concatenate([jnp.where(half_of_lane == hj, qn, jnp.zeros_like(qn))
                        for hj in range(2 * ATT_HEADS)], axis=0)
  rows_per_head = 2 * SUBLANES

  def softmax_probs(s):
    m_prev = m_sc[...]
    m_new = jnp.maximum(m_prev, jnp.max(s, axis=-1, keepdims=True))
    alpha = jnp.exp2(m_prev - m_new)
    p = jnp.exp2(s - m_new[:, 0:1])
    l_sc[...] = l_sc[...] * alpha + jnp.sum(p, axis=-1, keepdims=True)
    m_sc[...] = m_new
    return alpha, p.astype(BF16)

  def accumulate(alpha, pb, values_of_head):
    pv = [_dot(pb[h * rows_per_head:(h + 1) * rows_per_head, :], values_of_head(h))
          for h in range(ATT_HEADS)]
    acc_sc[...] = acc_sc[...] * alpha + jnp.concatenate(pv, axis=0)

  def round_scores(r):
    sl = r % ROUND_SLOTS
    kt = jnp.concatenate([kbuf[sl, p].astype(BF16) for p in range(PAGES_PER_GROUP)], axis=1)
    hr, hf = qb.shape[0] // 2, qb.shape[1] // 2
    s = jnp.concatenate([_dot(qb[:hr, :hf], kt[:hf]), _dot(qb[hr:, hf:], kt[hf:])], axis=0)
    if r == rps - 1:
      tail = s[:, -PAGE_SIZE:] + jnp.where(last_part, blast_ref[...], 0.0)
      s = jnp.concatenate([s[:, :-PAGE_SIZE], tail], axis=1)
    return s

  def round_values(r):
    sl = r % ROUND_SLOTS

    def values_of_head(h):
      v = [vbuf[sl, p, pl.ds(h, PAGE_SIZE, stride=ATT_HEADS), :] for p in range(PAGES_PER_GROUP)]
      return jnp.concatenate(v, axis=0).astype(BF16)

    return values_of_head

  x1, h2 = _ffn_pre(x_ref, att_ref, ssd_ref, ga1_ref, sc2_ref, sh2_ref, gf_ref, wo_ref)
  fc = D_FF // rps
  for r in range(rps):
    for cp, _ in round_copies(r):
      cp.wait()
    if r + ahead < rps:
      start_round(r + ahead)
    else:
      pl.when(i + 1 < n_steps)(functools.partial(start_round, r + ahead))
    s = round_scores(r)
    u = jnp.maximum(_dot(h2, wu_ref[:, r * fc:(r + 1) * fc]), 0.0)
    ub = (u * u).astype(BF16)
    alpha, pb = softmax_probs(s)
    accumulate(alpha, pb, round_values(r))
    piece = _dot(ub, wd_ref[r * fc:(r + 1) * fc, :])
    facc_sc[...] = piece if r == 0 else facc_sc[...] + piece
  y_ref[0] = _ffn_post(x1, facc_sc[...], ga2_ref, gl_ref)

  @pl.when(last_part)
  def _():
    t_new = kn_ref.shape[1]
    kpad[0:t_new, :] = kn_ref[0].astype(BF16)
    vpad[0:t_new, :] = vn_ref[0].astype(BF16)
    sn = _dot_nt(qb, kpad[...]) + bnew_ref[...]
    accumulate(*softmax_probs(sn), lambda h: vpad[:, h * ATT_V_DIM:(h + 1) * ATT_V_DIM])
    o = acc_sc[...] / l_sc[...]
    lam = _diff_lambda(lam_ref)
    for h in range(ATT_HEADS):
      o0 = o[(2 * h) * SUBLANES:(2 * h + 1) * SUBLANES, :]
      o1 = o[(2 * h + 1) * SUBLANES:(2 * h + 2) * SUBLANES, :]
      o_ref[0, :, h * ATT_V_DIM:(h + 1) * ATT_V_DIM] = _diff_combine(
          o0, o1, lam, g_ref[...]).astype(o_ref.dtype)


def _ffn_with_sample_attention(x_p, att_p, ssd_p, mods, g_ffn, g_final,
                               w_out_b, w_up_b, w_down_b,
                               page_table, q_new, k_new, v_new, blast, bnew, lam4, subln_g,
                               cache_k, cache_v):
  nb, tb, d = x_p.shape
  tm = FFN_TM
  tiles_per_b = tb // tm
  n_steps = nb * tiles_per_b
  n_seq, n_pages = page_table.shape
  t_new = k_new.shape[1]
  pages_per_step = PAGES_PER_GROUP * ROUNDS_PER_STEP
  steps_per_seq = n_pages // pages_per_step
  ring = ROUND_SLOTS
  assert t_new == SUBLANES and n_pages % pages_per_step == 0
  assert n_seq * steps_per_seq == n_steps, "one batch of page rounds per FFN tile"
  assert ROUNDS_PER_STEP % ROUND_SLOTS == 0, "ring slots must be static per round"
  w = ATT_WIDTH
  assert cache_k.shape[1:] == (w, PAGE_SIZE) and cache_v.shape[1:] == (PAGE_SIZE * ATT_HEADS, ATT_V_DIM)
  tok = lambda width: pl.BlockSpec((1, tm, width),
                                   lambda i, pt: (i // tiles_per_b, i % tiles_per_b, 0))
  mspec = lambda col: _mod_spec(mods, tm, col, lambda i, pt: i // tiles_per_b,
                                lambda i, pt: i % tiles_per_b)
  const = lambda shape: pl.BlockSpec(shape, lambda i, pt: (0,) * len(shape),
                                     pipeline_mode=pl.Buffered(1))
  per_seq = lambda r: pl.BlockSpec((1, r, w), lambda i, pt: (i // steps_per_seq, 0, 0))
  grid_spec = pltpu.PrefetchScalarGridSpec(
      num_scalar_prefetch=1,
      grid=(n_steps,),
      in_specs=[tok(d), tok(ATT_WIDTH), tok(SSD_WIDTH),
                mspec(MOD_GA1), mspec(MOD_SC2), mspec(MOD_SH2), mspec(MOD_GA2),
                const((1, d)), const((1, d)),
                const((d, d)), const((d, D_FF)), const((D_FF, d)),
                per_seq(t_new), per_seq(t_new), per_seq(t_new),
                const((8 * SUBLANES, LANES)), const((8 * SUBLANES, LANES)),
                const((4, ATT_HEAD_DIM)), const((1, ATT_V_DIM)),
                pl.BlockSpec(memory_space=pl.ANY),
                pl.BlockSpec(memory_space=pl.ANY)],
      out_specs=[tok(d), per_seq(t_new)],
      scratch_shapes=[pltpu.VMEM((ring, PAGES_PER_GROUP) + cache_k.shape[1:], F32),
                      pltpu.VMEM((ring, PAGES_PER_GROUP) + cache_v.shape[1:], F32),
                      pltpu.SemaphoreType.DMA((2, ring)),
                      pltpu.VMEM((8 * SUBLANES, LANES), F32),
                      pltpu.VMEM((8 * SUBLANES, LANES), F32),
                      pltpu.VMEM((8 * SUBLANES, ATT_V_DIM), F32),
                      pltpu.VMEM((PAGE_SIZE, w), BF16),
                      pltpu.VMEM((PAGE_SIZE, w), BF16),
                      pltpu.VMEM((tm, d), F32)])
  return pl.pallas_call(
      functools.partial(_ffn_sample_kernel, n_steps=n_steps, steps_per_seq=steps_per_seq),
      grid_spec=grid_spec,
      out_shape=[jax.ShapeDtypeStruct((nb, tb, d), F32),
                 jax.ShapeDtypeStruct((n_seq, t_new, w), BF16)],
      compiler_params=pltpu.CompilerParams(
          dimension_semantics=("arbitrary",), vmem_limit_bytes=VMEM_LIMIT),
      name="ffn_sample_attention",
  )(page_table, x_p, att_p, ssd_p, mods, mods, mods, mods,
    g_ffn.reshape(1, d), g_final.reshape(1, d), w_out_b, w_up_b, w_down_b,
    q_new, k_new, v_new, blast, bnew, lam4, subln_g, cache_k, cache_v)


def kernel(x_prompt, x_sample, cache_k, cache_v, state_conv, state_ssm, page_table, c_prompt,
           c_sample, rel_bias, w_ada, b_ada, g_mix, g_ffn, w_in, w_out, lam_q1, lam_k1, lam_q2,
           lam_k2, subln_g, conv_w, conv_b, dt_bias, a_log, d_skip, ssd_norm_g, w_up, w_down,
           g_final):
  assert w_ada.shape[0] == 1, "single-layer step"
  bp, sp, d = x_prompt.shape
  bs, ts, _ = x_sample.shape
  n_pool = cache_k.shape[1]

  w_in_p = jnp.pad(w_in[0], ((0, 0), (0, IN_PAD - IN_WIDTH))).astype(BF16)
  w_kt = w_in[0][:, ATT_WIDTH:2 * ATT_WIDTH].T.astype(BF16)
  w_out_b = w_out[0].astype(BF16)
  w_up_b = w_up[0].astype(BF16)
  w_down_b = w_down[0].astype(BF16)
  lam4 = jnp.concatenate([lam_q1, lam_k1, lam_q2, lam_k2], axis=0)
  subln = subln_g.reshape(1, ATT_V_DIM)
  dtb_p = jnp.pad(dt_bias, ((0, 0), (0, LANES - SSD_HEADS)))
  alog_p = jnp.pad(a_log, ((0, 0), (0, LANES - SSD_HEADS)))
  dsk_e = jnp.repeat(d_skip[0], SSD_HEAD_DIM).reshape(1, SSD_WIDTH)
  ssd_args = (conv_w[0], conv_b, dtb_p, alog_p, dsk_e, ssd_norm_g)

  mods = _modulation(jnp.concatenate([c_prompt, c_sample], axis=0), w_ada[0], b_ada[0])
  mods_p = mods[:bp].reshape(bp, 1, 6 * d)
  mods_s = jnp.repeat(mods[bp:], ts, axis=0).reshape(1, bs * ts, 6 * d)
  near, diag, slast, snew = _bias_tiles(rel_bias)

  q_hm, kt_hm, v_hm, kt_p, v_p, z_p, xbc_p, dt_p = _in_proj(
      x_prompt, mods_p, g_mix[0], w_in_p, w_kt, tm=512, prompt=True)
  att_p = _prompt_attention(q_hm, kt_hm, v_hm, near, diag, lam4, subln)
  ssd_p, conv_p, h_p = _ssd(
      xbc_p, z_p, dt_p, jnp.zeros((bp, CONV_W - 1, CONV_DIM), F32),
      jnp.zeros((bp, SSD_WIDTH, D_STATE), F32), *ssd_args, tv=SSD_CHUNK, seqs_per_step=4)

  n_tok = bs * ts
  xs3 = x_sample.reshape(1, n_tok, d)
  q_s, k_s, v_s, z_s, xbc_s, dt_s = _in_proj(
      xs3, mods_s, g_mix[0], w_in_p, w_kt, tm=n_tok, prompt=False)
  blast = jnp.repeat(slast, 2, axis=0).reshape(2 * ATT_HEADS * SUBLANES, LANES)
  bnew = jnp.repeat(snew, 2, axis=0).reshape(2 * ATT_HEADS * SUBLANES, LANES)
  per_seq = lambda a: a.reshape(bs, ts, ATT_WIDTH)
  y_p, att_s = _ffn_with_sample_attention(
      x_prompt, att_p, ssd_p, mods_p, g_ffn[0], g_final, w_out_b, w_up_b, w_down_b,
      page_table, per_seq(q_s), per_seq(k_s), per_seq(v_s), blast, bnew, lam4, subln,
      jnp.transpose(cache_k[0], (0, 2, 3, 4, 1)).reshape(n_pool, ATT_WIDTH, PAGE_SIZE),
      cache_v[0].reshape(n_pool, PAGE_SIZE * ATT_HEADS, ATT_V_DIM))
  ssd_s, conv_s, h_s = _ssd(
      xbc_s.reshape(bs, ts, CONV_DIM), z_s.reshape(bs, ts, SSD_WIDTH), dt_s.reshape(bs, ts, LANES),
      state_conv[0], state_ssm[0].reshape(bs, SSD_WIDTH, D_STATE), *ssd_args, tv=ts,
      seqs_per_step=4)
  y_s = _out_ffn(xs3, att_s.reshape(1, n_tok, ATT_WIDTH), ssd_s.reshape(1, n_tok, SSD_WIDTH),
                 mods_s, g_ffn[0], g_final, w_out_b, w_up_b, w_down_b, tm=n_tok)

  hshape = (SSD_HEADS, SSD_HEAD_DIM, D_STATE)
  return (y_p, y_s.reshape(bs, ts, d),
          kt_p.reshape(1, bp, ATT_HEADS, 2, ATT_HEAD_DIM, sp).transpose(0, 1, 5, 2, 3, 4),
          v_p.reshape(1, bp, sp, ATT_HEADS, ATT_V_DIM),
          conv_p[None], h_p.reshape(1, bp, *hshape),
          k_s.reshape(1, bs, ts, ATT_HEADS, 2, ATT_HEAD_DIM),
          v_s.reshape(1, bs, ts, ATT_HEADS, ATT_V_DIM),
          conv_s[None], h_s.reshape(1, bs, *hshape))
```

```python
import functools
import itertools
import math

import numpy as np
import jax
import jax.numpy as jnp
from jax import lax
from jax.experimental import pallas as pl
from jax.experimental.pallas import tpu as pltpu

F32 = jnp.float32
BF16 = jnp.bfloat16

D_MODEL = 1024
PAGE_SIZE = 128
ATT_HEADS = 4
ATT_HEAD_DIM = 64
ATT_V_DIM = 2 * ATT_HEAD_DIM
ATT_WIDTH = ATT_HEADS * ATT_V_DIM
SSD_HEADS = 8
SSD_HEAD_DIM = 64
SSD_WIDTH = SSD_HEADS * SSD_HEAD_DIM
SSD_GROUPS = 2
D_STATE = 128
CONV_W = 4
CONV_DIM = SSD_WIDTH + 2 * SSD_GROUPS * D_STATE
SSD_CHUNK = 128
IN_WIDTH = 3 * ATT_WIDTH + SSD_WIDTH + CONV_DIM + SSD_HEADS
D_FF = 4 * D_MODEL
N_BUCKETS = 32
MAX_DISTANCE = 128
MAX_EXACT = N_BUCKETS // 2
EPS = 1e-6
LAM_INIT = 0.8 - 0.6 * math.exp(-0.3 * 0)
MOD_SH1, MOD_SC1, MOD_GA1, MOD_SH2, MOD_SC2, MOD_GA2 = range(6)

LANES = 128
SUBLANES = 8
IN_MAIN = IN_WIDTH - SSD_HEADS
DT_ROWS = 2 * SUBLANES
LOG2E = 1.4426950408889634
NEG = -1e30
VMEM_LIMIT = 56 * 1024 * 1024

ATT_T = 256
FFN_TM = 256
PAGES_PER_GROUP = 8
ROUNDS_PER_STEP = 8
ROUND_SLOTS = 4


def _bucket_lower_bounds():
  d = np.arange(0, 4 * MAX_DISTANCE)
  nf = np.maximum(d, 1).astype(np.float64)
  large = MAX_EXACT + (np.log(nf / MAX_EXACT) / math.log(MAX_DISTANCE / MAX_EXACT)
                       * (N_BUCKETS - MAX_EXACT)).astype(np.int64)
  large = np.minimum(large, N_BUCKETS - 1)
  bucket = np.where(d < MAX_EXACT, d, large)
  return [int(np.argmax(bucket >= b)) for b in range(N_BUCKETS)]


_LOWER = _bucket_lower_bounds()
FAR_DIST = _LOWER[N_BUCKETS - 1]
assert FAR_DIST <= PAGE_SIZE and FAR_DIST <= ATT_T


def _silu(x):
  h = 0.5 * x
  return h * jnp.tanh(h) + h


def _dot(a, b):
  return jnp.dot(a, b, preferred_element_type=F32)


def _dot_nt(a, b):
  return lax.dot_general(a, b, (((1,), (1,)), ((), ())), preferred_element_type=F32)


def _split3(x):
  hi = x.astype(BF16)
  r1 = x - hi.astype(F32)
  mid = r1.astype(BF16)
  lo = (r1 - mid.astype(F32)).astype(BF16)
  return hi, mid, lo


def _mod_kernel(c_ref, w_ref, b_ref, o_ref):
  s = _silu(c_ref[...]).astype(BF16)
  o_ref[...] = _dot(s, w_ref[...].astype(BF16)) + b_ref[...]


def _modulation(c_all, w_ada, b_ada):
  n = c_all.shape[0]
  tn = D_MODEL
  return pl.pallas_call(
      _mod_kernel,
      grid=(6 * D_MODEL // tn,),
      in_specs=[pl.BlockSpec((n, D_MODEL), lambda j: (0, 0)),
                pl.BlockSpec((D_MODEL, tn), lambda j: (0, j)),
                pl.BlockSpec((1, tn), lambda j: (0, j))],
      out_specs=pl.BlockSpec((n, tn), lambda j: (0, j)),
      out_shape=jax.ShapeDtypeStruct((n, 6 * D_MODEL), F32),
      compiler_params=pltpu.CompilerParams(vmem_limit_bytes=VMEM_LIMIT),
      name="modulation",
  )(c_all, w_ada, b_ada.reshape(1, -1))


def _bias_kernel(tab_ref, near_ref, diag_ref, slast_ref, snew_ref):
  def bias_of(dist, h):
    val = jnp.full(dist.shape, tab_ref[0, h], F32)
    for b in range(1, N_BUCKETS):
      val = jnp.where(dist >= _LOWER[b], tab_ref[b, h], val)
    return (val - tab_ref[N_BUCKETS - 1, h]) * LOG2E

  t = ATT_T
  r = lax.broadcasted_iota(jnp.int32, (t, t), 0)
  c = lax.broadcasted_iota(jnp.int32, (t, t), 1)
  r8 = lax.broadcasted_iota(jnp.int32, (SUBLANES, LANES), 0)
  c8 = lax.broadcasted_iota(jnp.int32, (SUBLANES, LANES), 1)
  for h in range(ATT_HEADS):
    near_ref[h] = bias_of(t + r - c, h)
    d = r - c
    diag_ref[h] = jnp.where(d >= 0, bias_of(d, h), NEG)
    slast_ref[h] = bias_of(PAGE_SIZE + r8 - c8, h)
    dn = r8 - c8
    snew_ref[h] = jnp.where(dn >= 0, bias_of(dn, h), NEG)


def _bias_tiles(rel_bias):
  t = ATT_T
  return pl.pallas_call(
      _bias_kernel,
      in_specs=[pl.BlockSpec(memory_space=pltpu.SMEM)],
      out_shape=(jax.ShapeDtypeStruct((ATT_HEADS, t, t), F32),
                 jax.ShapeDtypeStruct((ATT_HEADS, t, t), F32),
                 jax.ShapeDtypeStruct((ATT_HEADS, SUBLANES, LANES), F32),
                 jax.ShapeDtypeStruct((ATT_HEADS, SUBLANES, LANES), F32)),
      compiler_params=pltpu.CompilerParams(vmem_limit_bytes=VMEM_LIMIT),
      name="bias_tiles",
  )(rel_bias)


def _inproj_kernel(x_ref, sc_ref, sh_ref, g_ref, w_ref, wkt_ref, wdt_ref, *out_refs, prompt):
  x = x_ref[0]
  var = jnp.mean(x * x, axis=-1, keepdims=True)
  h = x * lax.rsqrt(var + EPS) * g_ref[...]
  h = h * (1.0 + sc_ref[0]) + sh_ref[0]
  hb = h.astype(BF16)
  aw = ATT_WIDTH
  q = _dot(hb, w_ref[:, 0:aw]) * (ATT_HEAD_DIM ** -0.5 * LOG2E)
  v = _dot(hb, w_ref[:, 2 * aw:3 * aw])
  if prompt:
    q_ref, ktb_ref, vb_ref, kt_ref, v_ref, z_ref, xbc_ref, dt_ref = out_refs
    kt = _dot_nt(wkt_ref[...], hb)
    kt_ref[0] = kt
    t = ATT_T
    for hd in range(ATT_HEADS):
      sl = slice(hd * ATT_V_DIM, (hd + 1) * ATT_V_DIM)
      vb_ref[0, hd] = v[:, sl].astype(BF16)
      v_ref[0, pl.ds(hd, v.shape[0], stride=ATT_HEADS), :] = v[:, sl]
      q_ref[0, hd] = q[:, sl].astype(BF16)
      for cc in range(kt.shape[1] // t):
        ktb_ref[0, hd, cc] = kt[sl, cc * t:(cc + 1) * t].astype(BF16)
  else:
    q_ref, k_ref, v_ref, z_ref, xbc_ref, dt_ref = out_refs
    q_ref[0] = q
    k_ref[0] = _dot(hb, w_ref[:, aw:2 * aw])
    v_ref[0] = v
  o = 3 * aw
  z_ref[0] = _dot(hb, w_ref[:, o:o + SSD_WIDTH])
  o += SSD_WIDTH
  xbc_ref[0] = _dot(hb, w_ref[:, o:o + CONV_DIM])
  dt_ref[0] = _dot_nt(wdt_ref[...], hb) if prompt else _dot(hb, wdt_ref[...])


def _mod_spec(mods, tm, col, b_of, i_of):
  d = mods.shape[2] // 6
  if mods.shape[1] != 1:
    return pl.BlockSpec((1, tm, d), lambda *g: (b_of(*g), i_of(*g), col))
  return pl.BlockSpec((1, 1, d), lambda *g: (b_of(*g), 0, col))


def _in_proj(x3, mods, g_mix, w_main, w_kt, w_dt, tm, prompt):
  nb, tb, d = x3.shape
  nt = tb // tm
  mspec = lambda col: _mod_spec(mods, tm, col, lambda b, i: b, lambda b, i: i)
  hm = pl.BlockSpec((1, ATT_HEADS, tm, ATT_V_DIM), lambda b, i: (b, 0, i, 0))
  tok = lambda w: pl.BlockSpec((1, tm, w), lambda b, i: (b, i, 0))
  hm_shape = jax.ShapeDtypeStruct((nb, ATT_HEADS, tb, ATT_V_DIM), BF16)
  tok_shape = lambda w: jax.ShapeDtypeStruct((nb, tb, w), F32)
  if prompt:
    dt_spec = pl.BlockSpec((1, DT_ROWS, tm), lambda b, i: (b, 0, i))
    dt_shape = jax.ShapeDtypeStruct((nb, DT_ROWS, tb), F32)
  else:
    dt_spec, dt_shape = tok(LANES), tok_shape(LANES)
  tail_specs = [tok(SSD_WIDTH), tok(CONV_DIM), dt_spec]
  tail_shapes = [tok_shape(SSD_WIDTH), tok_shape(CONV_DIM), dt_shape]
  if prompt:
    t = ATT_T
    fm = pl.BlockSpec((1, ATT_HEADS, tm // t, ATT_V_DIM, t), lambda b, i: (b, 0, i, 0, 0))
    fm_shape = jax.ShapeDtypeStruct((nb, ATT_HEADS, tb // t, ATT_V_DIM, t), BF16)
    out_specs = [hm, fm, hm,
                 pl.BlockSpec((1, ATT_WIDTH, tm), lambda b, i: (b, 0, i)),
                 pl.BlockSpec((1, tm * ATT_HEADS, ATT_V_DIM), lambda b, i: (b, i, 0))] + tail_specs
    out_shape = [hm_shape, fm_shape, hm_shape,
                 jax.ShapeDtypeStruct((nb, ATT_WIDTH, tb), F32),
                 jax.ShapeDtypeStruct((nb, tb * ATT_HEADS, ATT_V_DIM), F32)] + tail_shapes
  else:
    out_specs = [tok(ATT_WIDTH)] * 3 + tail_specs
    out_shape = [tok_shape(ATT_WIDTH)] * 3 + tail_shapes
  return pl.pallas_call(
      functools.partial(_inproj_kernel, prompt=prompt),
      grid=(nb, nt),
      in_specs=[tok(d), mspec(MOD_SC1), mspec(MOD_SH1),
                pl.BlockSpec((1, d), lambda b, i: (0, 0)),
                pl.BlockSpec(w_main.shape, lambda b, i: (0, 0)),
                pl.BlockSpec(w_kt.shape, lambda b, i: (0, 0)),
                pl.BlockSpec(w_dt.shape, lambda b, i: (0, 0))],
      out_specs=out_specs,
      out_shape=out_shape,
      compiler_params=pltpu.CompilerParams(
          dimension_semantics=("arbitrary", "arbitrary"), vmem_limit_bytes=VMEM_LIMIT),
      name="in_proj",
  )(x3, mods, mods, g_mix.reshape(1, d), w_main, w_kt, w_dt)


def _diff_lambda(lam_ref):
  lv = lam_ref[...]
  s1 = jnp.sum(lv[0:1] * lv[1:2], axis=-1, keepdims=True)
  s2 = jnp.sum(lv[2:3] * lv[3:4], axis=-1, keepdims=True)
  return jnp.exp(s1) - jnp.exp(s2) + LAM_INIT


def _diff_combine(o0, o1, lam, g):
  d = o0 - lam * o1
  d = d * lax.rsqrt(jnp.mean(d * d, axis=-1, keepdims=True) + EPS) * g
  return d * (1.0 - LAM_INIT)


def _attn_kernel(q_ref, k_ref, v_ref, near_ref, diag_ref, lam_ref, g_ref, o_ref,
                 m_sc, acc_sc):
  t = ATT_T
  qi = pl.program_id(1)
  lane = lax.broadcasted_iota(jnp.int32, (t, ATT_V_DIM), 1)
  m_sc[...] = jnp.full(m_sc.shape, NEG, F32)
  acc_sc[...] = jnp.zeros(acc_sc.shape, F32)

  def scores(h, c, bias_refs):
    q = q_ref[0, h]
    zero = jnp.zeros_like(q)
    q2 = jnp.concatenate([jnp.where(lane < ATT_HEAD_DIM, q, zero),
                          jnp.where(lane >= ATT_HEAD_DIM, q, zero)], axis=0)
    kc = jnp.concatenate([k_ref[0, h, c + j] for j in range(len(bias_refs))], axis=1)
    s = _dot(q2, kc)
    if any(b is not None for b in bias_refs):
      bias = jnp.concatenate([jnp.zeros((t, t), F32) if b is None else b[h] for b in bias_refs],
                             axis=1)
      s = s + jnp.concatenate([bias, bias], axis=0)
    return s

  def softmax_probs(h, s):
    m_prev = m_sc[h]
    m_new = jnp.maximum(m_prev, jnp.max(s, axis=-1, keepdims=True))
    alpha = jnp.exp2(m_prev - m_new)
    p = jnp.exp2(s - jnp.concatenate([m_new] * (s.shape[1] // LANES), axis=1))
    m_sc[h] = m_new
    return alpha, p.astype(BF16)

  def step(c, bias_refs):
    keys = len(bias_refs) * t
    start = pl.multiple_of(c * t, t)
    ones = jnp.ones((keys, LANES), BF16)
    for h in range(ATT_HEADS):
      alpha, pb = softmax_probs(h, scores(h, c, bias_refs))
      vaug = jnp.concatenate([v_ref[0, h, pl.ds(start, keys), :], ones], axis=1)
      acc_sc[h] = acc_sc[h] * jnp.concatenate([alpha, alpha], axis=1) + _dot(pb, vaug)

  n_far = jnp.maximum(qi - 1, 0)

  def far_pair(k, carry):
    step(2 * k, (None, None))
    return carry

  lax.fori_loop(0, n_far // 2, far_pair, 0)

  @pl.when(n_far % 2 == 1)
  def _():
    step(n_far - 1, (None,))

  @pl.when(qi >= 1)
  def _():
    step(qi - 1, (near_ref, diag_ref))

  @pl.when(qi == 0)
  def _():
    step(qi, (diag_ref,))

  lam = _diff_lambda(lam_ref)
  for h in range(ATT_HEADS):
    acc = acc_sc[h]
    o = acc[:, :ATT_V_DIM] / acc[:, ATT_V_DIM:]
    d = _diff_combine(o[:t], o[t:], lam, g_ref[...])
    o_ref[0, :, h * ATT_V_DIM:(h + 1) * ATT_V_DIM] = d.astype(o_ref.dtype)


def _prompt_attention(q_hm, kt_hm, v_hm, near, diag, lam4, subln_g):
  b, nh, s, e = q_hm.shape
  t = ATT_T
  nq = s // t
  return pl.pallas_call(
      _attn_kernel,
      grid=(b, nq),
      in_specs=[pl.BlockSpec((1, nh, t, e), lambda bi, qi: (bi, 0, qi, 0)),
                pl.BlockSpec((1, nh, nq, e, t), lambda bi, qi: (bi, 0, 0, 0, 0)),
                pl.BlockSpec((1, nh, s, e), lambda bi, qi: (bi, 0, 0, 0)),
                pl.BlockSpec((nh, t, t), lambda bi, qi: (0, 0, 0)),
                pl.BlockSpec((nh, t, t), lambda bi, qi: (0, 0, 0)),
                pl.BlockSpec((4, ATT_HEAD_DIM), lambda bi, qi: (0, 0)),
                pl.BlockSpec((1, e), lambda bi, qi: (0, 0))],
      out_specs=pl.BlockSpec((1, t, nh * e), lambda bi, qi: (bi, qi, 0)),
      out_shape=jax.ShapeDtypeStruct((b, s, nh * e), BF16),
      scratch_shapes=[pltpu.VMEM((nh, 2 * t, LANES), F32),
                      pltpu.VMEM((nh, 2 * t, 2 * LANES), F32)],
      compiler_params=pltpu.CompilerParams(
          dimension_semantics=("arbitrary", "arbitrary"), vmem_limit_bytes=VMEM_LIMIT),
      name="prompt_attention",
  )(q_hm, kt_hm, v_hm, near, diag, lam4, subln_g)


def _ssd_kernel(xbc_ref, z_ref, dt_ref, pre_ref, h0_ref, *refs, tv, nc, alternate):
  consts, (ssd_ref, conv_ref, h_ref, xp_sc) = refs[:8], refs[8:]
  hp = SUBLANES
  b = pl.program_id(0)
  c = pl.program_id(1)
  n_seq = xbc_ref.shape[0]

  @pl.when((b == 0) & (c == 0))
  def _():
    xp_sc[...] = jnp.zeros(xp_sc.shape, F32)

  @pl.when(c == 0)
  def _():
    for bb in range(n_seq):
      xp_sc[bb, hp - (CONV_W - 1):hp, :] = pre_ref[bb]
      h_ref[bb] = h0_ref[bb]

  stages = [_ssd_chunk(xbc_ref.at[bb], z_ref.at[bb], dt_ref.at[bb], *consts,
                       ssd_ref.at[bb], h_ref.at[bb], xp_sc.at[bb], tv=tv) for bb in range(n_seq)]
  for _ in (itertools.zip_longest(*stages) if alternate else itertools.chain(*stages)):
    pass

  @pl.when(c == nc - 1)
  def _():
    for bb in range(n_seq):
      conv_ref[bb] = xp_sc[bb, hp + tv - (CONV_W - 1):hp + tv, :]


def _ssd_chunk(xbc_ref, z_ref, dt_ref, cw_ref, cb_ref, dtb_ref, alog_ref, dsk_ref, ng_ref, e_ref,
               tri_ref, ssd_ref, h_ref, xp_sc, *, tv):
  L = SSD_CHUNK
  hp = SUBLANES
  xp_sc[hp:hp + tv, :] = xbc_ref[...]
  cw = cw_ref[...]
  y = cb_ref[...]
  cur = xp_sc[hp:hp + L, :]
  hist = xp_sc[0:hp, :]
  head_row = lax.broadcasted_iota(jnp.int32, (hp, CONV_DIM), 0)
  for i in range(CONV_W):
    back = CONV_W - 1 - i
    if back == 0:
      tap = cur
    else:
      rolled = pltpu.roll(cur, back, 0)
      head = jnp.where(head_row < back, pltpu.roll(hist, back, 0), rolled[0:hp])
      tap = jnp.concatenate([head, rolled[hp:]], axis=0)
    y = y + tap * cw[i:i + 1]
  xa = _silu(y)
  xp_sc[0:hp, :] = xp_sc[tv:tv + hp, :]
  yield

  xs = xa[:, :SSD_WIDTH]
  gw = SSD_GROUPS * D_STATE
  bmat = xa[:, SSD_WIDTH:SSD_WIDTH + gw].astype(BF16)
  cmat = xa[:, SSD_WIDTH + gw:].astype(BF16)

  dt_raw = dt_ref[...]
  if dt_raw.shape[0] != tv:
    dt_raw = jnp.concatenate([dt_raw, jnp.zeros((L - dt_raw.shape[0], L), F32)], axis=0).T
  x = dt_raw + dtb_ref[...]
  dtv = jnp.maximum(x, 0.0) + jnp.log1p(jnp.exp(-jnp.abs(x)))
  if tv < L:
    dtv = jnp.concatenate([dtv, jnp.zeros((L - tv, LANES), F32)], axis=0)
  a = dtv * (-jnp.exp(alog_ref[...]))

  tri = tri_ref[...]
  acs = sum(_dot(tri, part) for part in _split3(a))
  yield
  e = e_ref[...]
  aexp = sum(_dot(part, e) for part in _split3(acs))
  dtexp = _dot(dtv.astype(BF16), e)
  xd = xs * dtexp
  a_last = acs[L - 1:L, :]
  xdd_t = (xd * jnp.exp(aexp[L - 1:L, :] - aexp)).T
  acs_t = acs.T
  yield

  ri = lax.broadcasted_iota(jnp.int32, (L, L), 0)
  ci = lax.broadcasted_iota(jnp.int32, (L, L), 1)
  causal = ri >= ci
  lane = lax.broadcasted_iota(jnp.int32, (L, LANES), 1)
  heads_per_group = SSD_HEADS // SSD_GROUPS
  hprev = h_ref[...]
  hb = hprev.astype(BF16)

  ydiag, yoff, upd = [], [], []
  for g in range(SSD_GROUPS):
    cg = cmat[:, g * D_STATE:(g + 1) * D_STATE]
    bg = bmat[:, g * D_STATE:(g + 1) * D_STATE]
    scores = _dot_nt(cg, bg)
    for pair in range(heads_per_group // 2):
      h0 = g * heads_per_group + 2 * pair
      ms = []
      for h in (h0, h0 + 1):
        col = jnp.broadcast_to(acs[:, h:h + 1], (L, L))
        row = acs_t[h:h + 1, :]
        lm = jnp.exp(jnp.where(causal, col - row, NEG))
        ms.append((scores * lm).astype(BF16))
      xpair = xd[:, h0 * SSD_HEAD_DIM:(h0 + 2) * SSD_HEAD_DIM]
      xbd = jnp.concatenate([jnp.where(lane < SSD_HEAD_DIM, xpair, 0.0),
                             jnp.where(lane >= SSD_HEAD_DIM, xpair, 0.0)], axis=0)
      ydiag.append(_dot(jnp.concatenate(ms, axis=1), xbd.astype(BF16)))
      yield
    rows = slice(g * heads_per_group * SSD_HEAD_DIM, (g + 1) * heads_per_group * SSD_HEAD_DIM)
    yoff.append(_dot_nt(cg, hb[rows, :]))
    upd.append(_dot(xdd_t[rows, :].astype(BF16), bg))
    yield

  y = (jnp.concatenate(ydiag, axis=1) + jnp.concatenate(yoff, axis=1) * jnp.exp(aexp)
       + dsk_ref[...] * xs)
  yield

  decayed = []
  chunk_decay = jnp.exp(a_last)
  for h in range(SSD_HEADS):
    dec = jnp.broadcast_to(chunk_decay[:, h:h + 1], (SSD_HEAD_DIM, D_STATE))
    decayed.append(hprev[h * SSD_HEAD_DIM:(h + 1) * SSD_HEAD_DIM, :] * dec)
  h_ref[...] = jnp.concatenate(decayed, axis=0) + jnp.concatenate(upd, axis=0)
  yield

  yv = y[:tv] * _silu(z_ref[...])
  sq = yv * yv
  half = SSD_WIDTH // SSD_GROUPS
  r0 = lax.rsqrt(jnp.mean(sq[:, :half], axis=-1, keepdims=True) + EPS)
  r1 = lax.rsqrt(jnp.mean(sq[:, half:], axis=-1, keepdims=True) + EPS)
  out = jnp.concatenate([yv[:, :half] * r0, yv[:, half:] * r1], axis=1) * ng_ref[...]
  ssd_ref[...] = out.astype(ssd_ref.dtype)


def _ssd(xbc, z, dt, prefix, h0, conv_w, conv_b, dtb_p, alog_p, dsk_e, norm_g, tv, seqs_per_step):
  nb, tb, _ = xbc.shape
  nc = tb // tv
  ns = seqs_per_step
  assert nb % ns == 0
  e_np = np.zeros((LANES, SSD_WIDTH), np.float32)
  for h in range(SSD_HEADS):
    e_np[h, h * SSD_HEAD_DIM:(h + 1) * SSD_HEAD_DIM] = 1.0
  tri_np = np.tril(np.ones((SSD_CHUNK, SSD_CHUNK), np.float32))
  full = lambda shape: pl.BlockSpec(shape, lambda b, c: (0,) * len(shape))
  tok = lambda w: pl.BlockSpec((ns, tv, w), lambda b, c: (b, c, 0))
  per_b = lambda r, w: pl.BlockSpec((ns, r, w), lambda b, c: (b, 0, 0))
  if dt.shape[1] == tb:
    dt_spec = tok(LANES)
  else:
    assert tv == SSD_CHUNK and dt.shape == (nb, DT_ROWS, tb)
    dt_spec = pl.BlockSpec((ns, DT_ROWS, tv), lambda b, c: (b, 0, c))
  return pl.pallas_call(
      functools.partial(_ssd_kernel, tv=tv, nc=nc, alternate=tv < SSD_CHUNK),
      grid=(nb // ns, nc),
      in_specs=[tok(CONV_DIM), tok(SSD_WIDTH), dt_spec,
                per_b(CONV_W - 1, CONV_DIM), per_b(SSD_WIDTH, D_STATE),
                full((CONV_W, CONV_DIM)), full((1, CONV_DIM)), full((1, LANES)), full((1, LANES)),
                full((1, SSD_WIDTH)), full((1, SSD_WIDTH)),
                full((LANES, SSD_WIDTH)), full((SSD_CHUNK, SSD_CHUNK))],
      out_specs=[tok(SSD_WIDTH), per_b(CONV_W - 1, CONV_DIM), per_b(SSD_WIDTH, D_STATE)],
      out_shape=[jax.ShapeDtypeStruct((nb, tb, SSD_WIDTH), BF16),
                 jax.ShapeDtypeStruct((nb, CONV_W - 1, CONV_DIM), F32),
                 jax.ShapeDtypeStruct((nb, SSD_WIDTH, D_STATE), F32)],
      scratch_shapes=[pltpu.VMEM((ns, SSD_CHUNK + SUBLANES, CONV_DIM), F32)],
      compiler_params=pltpu.CompilerParams(
          dimension_semantics=("arbitrary", "arbitrary"), vmem_limit_bytes=VMEM_LIMIT),
      name="ssd_scan",
  )(xbc, z, dt, prefix, h0, conv_w, conv_b, dtb_p, alog_p, dsk_e, norm_g,
    jnp.asarray(e_np, BF16), jnp.asarray(tri_np, BF16))


def _ffn_pre(x_ref, att_ref, ssd_ref, ga1_ref, sc2_ref, sh2_ref, gf_ref, wo_ref):
  mix = _dot(att_ref[0], wo_ref[0:ATT_WIDTH, :]) + _dot(ssd_ref[0], wo_ref[ATT_WIDTH:, :])
  x1 = x_ref[0] + ga1_ref[0] * mix
  var = jnp.mean(x1 * x1, axis=-1, keepdims=True)
  h2 = x1 * lax.rsqrt(var + EPS) * gf_ref[...]
  return x1, (h2 * (1.0 + sc2_ref[0]) + sh2_ref[0]).astype(BF16)


def _ffn_piece(h2, wu_ref, wd_ref, lo, hi):
  u = jnp.maximum(_dot(h2, wu_ref[:, lo:hi]), 0.0)
  return _dot((u * u).astype(BF16), wd_ref[lo:hi, :])


def _ffn_post(x1, acc, ga2_ref, gl_ref):
  x2 = x1 + ga2_ref[0] * acc
  var2 = jnp.mean(x2 * x2, axis=-1, keepdims=True)
  return x2 * lax.rsqrt(var2 + EPS) * gl_ref[...]


def _ffn_kernel(x_ref, att_ref, ssd_ref, ga1_ref, sc2_ref, sh2_ref, ga2_ref, gf_ref, gl_ref,
                wo_ref, wu_ref, wd_ref, y_ref):
  x1, h2 = _ffn_pre(x_ref, att_ref, ssd_ref, ga1_ref, sc2_ref, sh2_ref, gf_ref, wo_ref)
  fc = D_MODEL
  acc = jnp.zeros(x1.shape, F32)
  for f in range(D_FF // fc):
    acc = acc + _ffn_piece(h2, wu_ref, wd_ref, f * fc, (f + 1) * fc)
  y_ref[0] = _ffn_post(x1, acc, ga2_ref, gl_ref)


def _out_ffn(x3, att, ssd, mods, g_ffn, g_final, w_out_b, w_up_b, w_down_b, tm):
  nb, tb, d = x3.shape
  nt = tb // tm
  mspec = lambda col: _mod_spec(mods, tm, col, lambda b, i: b, lambda b, i: i)
  tok = lambda w: pl.BlockSpec((1, tm, w), lambda b, i: (b, i, 0))
  const = lambda shape: pl.BlockSpec(shape, lambda b, i: (0, 0), pipeline_mode=pl.Buffered(1))
  return pl.pallas_call(
      _ffn_kernel,
      grid=(nb, nt),
      in_specs=[tok(d), tok(ATT_WIDTH), tok(SSD_WIDTH),
                mspec(MOD_GA1), mspec(MOD_SC2), mspec(MOD_SH2), mspec(MOD_GA2),
                const((1, d)), const((1, d)),
                const((d, d)), const((d, D_FF)), const((D_FF, d))],
      out_specs=tok(d),
      out_shape=jax.ShapeDtypeStruct((nb, tb, d), F32),
      compiler_params=pltpu.CompilerParams(
          dimension_semantics=("arbitrary", "arbitrary"), vmem_limit_bytes=VMEM_LIMIT),
      name="out_ffn",
  )(x3, att, ssd, mods, mods, mods, mods, g_ffn.reshape(1, d), g_final.reshape(1, d),
    w_out_b, w_up_b, w_down_b)


def _ffn_sample_kernel(pt_ref, x_ref, att_ref, ssd_ref, ga1_ref, sc2_ref, sh2_ref, ga2_ref,
                       gf_ref, gl_ref, wo_ref, wu_ref, wd_ref,
                       qn_ref, kn_ref, vn_ref, blast_ref, bnew_ref, lam_ref, g_ref, ck_hbm, cv_hbm,
                       y_ref, o_ref,
                       kbuf, vbuf, sem, m_sc, l_sc, acc_sc, kpad, vpad, facc_sc,
                       *, n_steps, steps_per_seq):
  i = pl.program_id(0)
  seq = i // steps_per_seq
  part = i % steps_per_seq
  rps = ROUNDS_PER_STEP
  ahead = ROUND_SLOTS - 1
  last_part = part == steps_per_seq - 1

  def round_copies(r):
    if r < rps:
      sq, rnd = seq, part * rps + r
    else:
      nxt = i + 1
      sq, rnd = nxt // steps_per_seq, (nxt % steps_per_seq) * rps + r - rps
    sl = r % ROUND_SLOTS
    copies = []
    for p in range(PAGES_PER_GROUP):
      pg = pt_ref[sq, rnd * PAGES_PER_GROUP + p]
      copies.append((pltpu.make_async_copy(ck_hbm.at[pg], kbuf.at[sl, p], sem.at[0, sl]), 0))
      copies.append((pltpu.make_async_copy(cv_hbm.at[pg], vbuf.at[sl, p], sem.at[1, sl]), 1))
    return copies

  def start_round(r):
    for cp, prio in round_copies(r):
      cp.start(priority=prio)

  @pl.when(i == 0)
  def _():
    kpad[...] = jnp.zeros(kpad.shape, kpad.dtype)
    vpad[...] = jnp.zeros(vpad.shape, vpad.dtype)
    for r in range(ahead):
      start_round(r)

  @pl.when(part == 0)
  def _():
    m_sc[...] = jnp.full(m_sc.shape, NEG, F32)
    l_sc[...] = jnp.zeros(l_sc.shape, F32)
    acc_sc[...] = jnp.zeros(acc_sc.shape, F32)

  qn = qn_ref[0].astype(BF16)
  half_of_lane = lax.broadcasted_iota(jnp.int32, qn.shape, 1) // ATT_HEAD_DIM
  qb = jnp.concatenate([jnp.where(half_of_lane == hj, qn, jnp.zeros_like(qn))
                        for hj in range(2 * ATT_HEADS)], axis=0)
  rows_per_head = 2 * SUBLANES

  def softmax_probs(s):
    m_prev = m_sc[...]
    m_new = jnp.maximum(m_prev, jnp.max(s, axis=-1, keepdims=True))
    alpha = jnp.exp2(m_prev - m_new)
    p = jnp.exp2(s - m_new[:, 0:1])
    l_sc[...] = l_sc[...] * alpha + jnp.sum(p, axis=-1, keepdims=True)
    m_sc[...] = m_new
    return alpha, p.astype(BF16)

  def accumulate(alpha, pb, values_of_head):
    pv = [_dot(pb[h * rows_per_head:(h + 1) * rows_per_head, :], values_of_head(h))
          for h in range(ATT_HEADS)]
    acc_sc[...] = acc_sc[...] * alpha + jnp.concatenate(pv, axis=0)

  def round_scores(r):
    sl = r % ROUND_SLOTS
    kt = jnp.concatenate([kbuf[sl, p].astype(BF16) for p in range(PAGES_PER_GROUP)], axis=1)
    hr, hf = qb.shape[0] // 2, qb.shape[1] // 2
    s = jnp.concatenate([_dot(qb[:hr, :hf], kt[:hf]), _dot(qb[hr:, hf:], kt[hf:])], axis=0)
    if r == rps - 1:
      tail = s[:, -PAGE_SIZE:] + jnp.where(last_part, blast_ref[...], 0.0)
      s = jnp.concatenate([s[:, :-PAGE_SIZE], tail], axis=1)
    return s

  def round_values(r):
    sl = r % ROUND_SLOTS

    def values_of_head(h):
      v = [vbuf[sl, p, pl.ds(h, PAGE_SIZE, stride=ATT_HEADS), :] for p in range(PAGES_PER_GROUP)]
      return jnp.concatenate(v, axis=0).astype(BF16)

    return values_of_head

  x1, h2 = _ffn_pre(x_ref, att_ref, ssd_ref, ga1_ref, sc2_ref, sh2_ref, gf_ref, wo_ref)
  fc = D_FF // rps
  for r in range(rps):
    for cp, _ in round_copies(r):
      cp.wait()
    if r + ahead < rps:
      start_round(r + ahead)
    else:
      pl.when(i + 1 < n_steps)(functools.partial(start_round, r + ahead))
    s = round_scores(r)
    u = jnp.maximum(_dot(h2, wu_ref[:, r * fc:(r + 1) * fc]), 0.0)
    ub = (u * u).astype(BF16)
    alpha, pb = softmax_probs(s)
    accumulate(alpha, pb, round_values(r))
    piece = _dot(ub, wd_ref[r * fc:(r + 1) * fc, :])
    facc_sc[...] = piece if r == 0 else facc_sc[...] + piece
  y_ref[0] = _ffn_post(x1, facc_sc[...], ga2_ref, gl_ref)

  @pl.when(last_part)
  def _():
    t_new = kn_ref.shape[1]
    kpad[0:t_new, :] = kn_ref[0].astype(BF16)
    vpad[0:t_new, :] = vn_ref[0].astype(BF16)
    sn = _dot_nt(qb, kpad[...]) + bnew_ref[...]
    accumulate(*softmax_probs(sn), lambda h: vpad[:, h * ATT_V_DIM:(h + 1) * ATT_V_DIM])
    o = acc_sc[...] / l_sc[...]
    lam = _diff_lambda(lam_ref)
    for h in range(ATT_HEADS):
      o0 = o[(2 * h) * SUBLANES:(2 * h + 1) * SUBLANES, :]
      o1 = o[(2 * h + 1) * SUBLANES:(2 * h + 2) * SUBLANES, :]
      o_ref[0, :, h * ATT_V_DIM:(h + 1) * ATT_V_DIM] = _diff_combine(
          o0, o1, lam, g_ref[...]).astype(o_ref.dtype)


def _ffn_with_sample_attention(x_p, att_p, ssd_p, mods, g_ffn, g_final,
                               w_out_b, w_up_b, w_down_b,
                               page_table, q_new, k_new, v_new, blast, bnew, lam4, subln_g,
                               cache_k, cache_v):
  nb, tb, d = x_p.shape
  tm = FFN_TM
  tiles_per_b = tb // tm
  n_steps = nb * tiles_per_b
  n_seq, n_pages = page_table.shape
  t_new = k_new.shape[1]
  pages_per_step = PAGES_PER_GROUP * ROUNDS_PER_STEP
  steps_per_seq = n_pages // pages_per_step
  ring = ROUND_SLOTS
  assert t_new == SUBLANES and n_pages % pages_per_step == 0
  assert n_seq * steps_per_seq == n_steps, "one batch of page rounds per FFN tile"
  assert ROUNDS_PER_STEP % ROUND_SLOTS == 0, "ring slots must be static per round"
  w = ATT_WIDTH
  assert cache_k.shape[1:] == (w, PAGE_SIZE) and cache_v.shape[1:] == (PAGE_SIZE * ATT_HEADS, ATT_V_DIM)
  tok = lambda width: pl.BlockSpec((1, tm, width),
                                   lambda i, pt: (i // tiles_per_b, i % tiles_per_b, 0))
  mspec = lambda col: _mod_spec(mods, tm, col, lambda i, pt: i // tiles_per_b,
                                lambda i, pt: i % tiles_per_b)
  const = lambda shape: pl.BlockSpec(shape, lambda i, pt: (0,) * len(shape),
                                     pipeline_mode=pl.Buffered(1))
  per_seq = lambda r: pl.BlockSpec((1, r, w), lambda i, pt: (i // steps_per_seq, 0, 0))
  grid_spec = pltpu.PrefetchScalarGridSpec(
      num_scalar_prefetch=1,
      grid=(n_steps,),
      in_specs=[tok(d), tok(ATT_WIDTH), tok(SSD_WIDTH),
                mspec(MOD_GA1), mspec(MOD_SC2), mspec(MOD_SH2), mspec(MOD_GA2),
                const((1, d)), const((1, d)),
                const((d, d)), const((d, D_FF)), const((D_FF, d)),
                per_seq(t_new), per_seq(t_new), per_seq(t_new),
                const((8 * SUBLANES, LANES)), const((8 * SUBLANES, LANES)),
                const((4, ATT_HEAD_DIM)), const((1, ATT_V_DIM)),
                pl.BlockSpec(memory_space=pl.ANY),
                pl.BlockSpec(memory_space=pl.ANY)],
      out_specs=[tok(d), per_seq(t_new)],
      scratch_shapes=[pltpu.VMEM((ring, PAGES_PER_GROUP) + cache_k.shape[1:], F32),
                      pltpu.VMEM((ring, PAGES_PER_GROUP) + cache_v.shape[1:], F32),
                      pltpu.SemaphoreType.DMA((2, ring)),
                      pltpu.VMEM((8 * SUBLANES, LANES), F32),
                      pltpu.VMEM((8 * SUBLANES, LANES), F32),
                      pltpu.VMEM((8 * SUBLANES, ATT_V_DIM), F32),
                      pltpu.VMEM((PAGE_SIZE, w), BF16),
                      pltpu.VMEM((PAGE_SIZE, w), BF16),
                      pltpu.VMEM((tm, d), F32)])
  return pl.pallas_call(
      functools.partial(_ffn_sample_kernel, n_steps=n_steps, steps_per_seq=steps_per_seq),
      grid_spec=grid_spec,
      out_shape=[jax.ShapeDtypeStruct((nb, tb, d), F32),
                 jax.ShapeDtypeStruct((n_seq, t_new, w), BF16)],
      compiler_params=pltpu.CompilerParams(
          dimension_semantics=("arbitrary",), vmem_limit_bytes=VMEM_LIMIT),
      name="ffn_sample_attention",
  )(page_table, x_p, att_p, ssd_p, mods, mods, mods, mods,
    g_ffn.reshape(1, d), g_final.reshape(1, d), w_out_b, w_up_b, w_down_b,
    q_new, k_new, v_new, blast, bnew, lam4, subln_g, cache_k, cache_v)


def kernel(x_prompt, x_sample, cache_k, cache_v, state_conv, state_ssm, page_table, c_prompt,
           c_sample, rel_bias, w_ada, b_ada, g_mix, g_ffn, w_in, w_out, lam_q1, lam_k1, lam_q2,
           lam_k2, subln_g, conv_w, conv_b, dt_bias, a_log, d_skip, ssd_norm_g, w_up, w_down,
           g_final):
  assert w_ada.shape[0] == 1, "single-layer step"
  bp, sp, d = x_prompt.shape
  bs, ts, _ = x_sample.shape
  n_pool = cache_k.shape[1]

  w_main = w_in[0][:, :IN_MAIN].astype(BF16)
  w_kt = w_in[0][:, ATT_WIDTH:2 * ATT_WIDTH].T.astype(BF16)
  w_dt = w_in[0][:, IN_MAIN:]
  w_dt_rows = jnp.pad(w_dt.T, ((0, DT_ROWS - SSD_HEADS), (0, 0))).astype(BF16)
  w_dt_cols = jnp.pad(w_dt, ((0, 0), (0, LANES - SSD_HEADS))).astype(BF16)
  w_out_b = w_out[0].astype(BF16)
  w_up_b = w_up[0].astype(BF16)
  w_down_b = w_down[0].astype(BF16)
  lam4 = jnp.concatenate([lam_q1, lam_k1, lam_q2, lam_k2], axis=0)
  subln = subln_g.reshape(1, ATT_V_DIM)
  dtb_p = jnp.pad(dt_bias, ((0, 0), (0, LANES - SSD_HEADS)))
  alog_p = jnp.pad(a_log, ((0, 0), (0, LANES - SSD_HEADS)))
  dsk_e = jnp.repeat(d_skip[0], SSD_HEAD_DIM).reshape(1, SSD_WIDTH)
  ssd_args = (conv_w[0], conv_b, dtb_p, alog_p, dsk_e, ssd_norm_g)

  mods = _modulation(jnp.concatenate([c_prompt, c_sample], axis=0), w_ada[0], b_ada[0])
  mods_p = mods[:bp].reshape(bp, 1, 6 * d)
  mods_s = jnp.repeat(mods[bp:], ts, axis=0).reshape(1, bs * ts, 6 * d)
  near, diag, slast, snew = _bias_tiles(rel_bias)

  q_hm, kt_hm, v_hm, kt_p, v_p, z_p, xbc_p, dt_p = _in_proj(
      x_prompt, mods_p, g_mix[0], w_main, w_kt, w_dt_rows, tm=512, prompt=True)
  att_p = _prompt_attention(q_hm, kt_hm, v_hm, near, diag, lam4, subln)
  ssd_p, conv_p, h_p = _ssd(
      xbc_p, z_p, dt_p, jnp.zeros((bp, CONV_W - 1, CONV_DIM), F32),
      jnp.zeros((bp, SSD_WIDTH, D_STATE), F32), *ssd_args, tv=SSD_CHUNK, seqs_per_step=4)

  n_tok = bs * ts
  xs3 = x_sample.reshape(1, n_tok, d)
  q_s, k_s, v_s, z_s, xbc_s, dt_s = _in_proj(
      xs3, mods_s, g_mix[0], w_main, w_kt, w_dt_cols, tm=n_tok, prompt=False)
  blast = jnp.repeat(slast, 2, axis=0).reshape(2 * ATT_HEADS * SUBLANES, LANES)
  bnew = jnp.repeat(snew, 2, axis=0).reshape(2 * ATT_HEADS * SUBLANES, LANES)
  per_seq = lambda a: a.reshape(bs, ts, ATT_WIDTH)
  y_p, att_s = _ffn_with_sample_attention(
      x_prompt, att_p, ssd_p, mods_p, g_ffn[0], g_final, w_out_b, w_up_b, w_down_b,
      page_table, per_seq(q_s), per_seq(k_s), per_seq(v_s), blast, bnew, lam4, subln,
      jnp.transpose(cache_k[0], (0, 2, 3, 4, 1)).reshape(n_pool, ATT_WIDTH, PAGE_SIZE),
      cache_v[0].reshape(n_pool, PAGE_SIZE * ATT_HEADS, ATT_V_DIM))
  ssd_s, conv_s, h_s = _ssd(
      xbc_s.reshape(bs, ts, CONV_DIM), z_s.reshape(bs, ts, SSD_WIDTH), dt_s.reshape(bs, ts, LANES),
      state_conv[0], state_ssm[0].reshape(bs, SSD_WIDTH, D_STATE), *ssd_args, tv=ts,
      seqs_per_step=4)
  y_s = _out_ffn(xs3, att_s.reshape(1, n_tok, ATT_WIDTH), ssd_s.reshape(1, n_tok, SSD_WIDTH),
                 mods_s, g_ffn[0], g_final, w_out_b, w_up_b, w_down_b, tm=n_tok)

  hshape = (SSD_HEADS, SSD_HEAD_DIM, D_STATE)
  return (y_p, y_s.reshape(bs, ts, d),
          kt_p.reshape(1, bp, ATT_HEADS, 2, ATT_HEAD_DIM, sp).transpose(0, 1, 5, 2, 3, 4),
          v_p.reshape(1, bp, sp, ATT_HEADS, ATT_V_DIM),
          conv_p[None], h_p.reshape(1, bp, *hshape),
          k_s.reshape(1, bs, ts, ATT_HEADS, 2, ATT_HEAD_DIM),
          v_s.reshape(1, bs, ts, ATT_HEADS, ATT_V_DIM),
          conv_s[None], h_s.reshape(1, bs, *hshape))
```

```python
import functools
import itertools
import math

import numpy as np
import jax
import jax.numpy as jnp
from jax import lax
from jax.experimental import pallas as pl
from jax.experimental.pallas import tpu as pltpu

F32 = jnp.float32
BF16 = jnp.bfloat16

D_MODEL = 1024
PAGE_SIZE = 128
ATT_HEADS = 4
ATT_HEAD_DIM = 64
ATT_V_DIM = 2 * ATT_HEAD_DIM
ATT_WIDTH = ATT_HEADS * ATT_V_DIM
SSD_HEADS = 8
SSD_HEAD_DIM = 64
SSD_WIDTH = SSD_HEADS * SSD_HEAD_DIM
SSD_GROUPS = 2
D_STATE = 128
CONV_W = 4
CONV_DIM = SSD_WIDTH + 2 * SSD_GROUPS * D_STATE
SSD_CHUNK = 128
IN_WIDTH = 3 * ATT_WIDTH + SSD_WIDTH + CONV_DIM + SSD_HEADS
D_FF = 4 * D_MODEL
N_BUCKETS = 32
MAX_DISTANCE = 128
MAX_EXACT = N_BUCKETS // 2
EPS = 1e-6
LAM_INIT = 0.8 - 0.6 * math.exp(-0.3 * 0)
MOD_SH1, MOD_SC1, MOD_GA1, MOD_SH2, MOD_SC2, MOD_GA2 = range(6)

LANES = 128
SUBLANES = 8
IN_MAIN = IN_WIDTH - SSD_HEADS
DT_ROWS = 2 * SUBLANES
LOG2E = 1.4426950408889634
NEG = -1e30
VMEM_LIMIT = 56 * 1024 * 1024

PROJ_TM = 512
SSD_SEQS_PER_STEP = 4
ATT_T = 256
FFN_TM = 256
PAGES_PER_GROUP = 8
ROUNDS_PER_STEP = 8
ROUND_SLOTS = 4


def _bucket_lower_bounds():
  d = np.arange(0, 4 * MAX_DISTANCE)
  nf = np.maximum(d, 1).astype(np.float64)
  large = MAX_EXACT + (np.log(nf / MAX_EXACT) / math.log(MAX_DISTANCE / MAX_EXACT)
                       * (N_BUCKETS - MAX_EXACT)).astype(np.int64)
  large = np.minimum(large, N_BUCKETS - 1)
  bucket = np.where(d < MAX_EXACT, d, large)
  return [int(np.argmax(bucket >= b)) for b in range(N_BUCKETS)]


_LOWER = _bucket_lower_bounds()
FAR_DIST = _LOWER[N_BUCKETS - 1]
assert FAR_DIST <= PAGE_SIZE and FAR_DIST <= ATT_T


def _silu(x):
  h = 0.5 * x
  return h * jnp.tanh(h) + h


def _dot(a, b):
  return jnp.dot(a, b, preferred_element_type=F32)


def _dot_nt(a, b):
  return lax.dot_general(a, b, (((1,), (1,)), ((), ())), preferred_element_type=F32)


def _split3(x):
  hi = x.astype(BF16)
  r1 = x - hi.astype(F32)
  mid = r1.astype(BF16)
  lo = (r1 - mid.astype(F32)).astype(BF16)
  return hi, mid, lo


def _mod_kernel(c_ref, w_ref, b_ref, o_ref):
  s = _silu(c_ref[...]).astype(BF16)
  o_ref[...] = _dot(s, w_ref[...].astype(BF16)) + b_ref[...]


def _modulation(c_all, w_ada, b_ada):
  n = c_all.shape[0]
  tn = D_MODEL
  return pl.pallas_call(
      _mod_kernel,
      grid=(6 * D_MODEL // tn,),
      in_specs=[pl.BlockSpec((n, D_MODEL), lambda j: (0, 0)),
                pl.BlockSpec((D_MODEL, tn), lambda j: (0, j)),
                pl.BlockSpec((1, tn), lambda j: (0, j))],
      out_specs=pl.BlockSpec((n, tn), lambda j: (0, j)),
      out_shape=jax.ShapeDtypeStruct((n, 6 * D_MODEL), F32),
      compiler_params=pltpu.CompilerParams(vmem_limit_bytes=VMEM_LIMIT),
      name="modulation",
  )(c_all, w_ada, b_ada.reshape(1, -1))


def _bias_kernel(tab_ref, near_ref, diag_ref, slast_ref, snew_ref):
  def bias_of(dist, h):
    val = jnp.full(dist.shape, tab_ref[0, h], F32)
    for b in range(1, N_BUCKETS):
      val = jnp.where(dist >= _LOWER[b], tab_ref[b, h], val)
    return (val - tab_ref[N_BUCKETS - 1, h]) * LOG2E

  t = ATT_T
  r = lax.broadcasted_iota(jnp.int32, (t, t), 0)
  c = lax.broadcasted_iota(jnp.int32, (t, t), 1)
  r8 = lax.broadcasted_iota(jnp.int32, (SUBLANES, LANES), 0)
  c8 = lax.broadcasted_iota(jnp.int32, (SUBLANES, LANES), 1)
  for h in range(ATT_HEADS):
    near_ref[h] = bias_of(t + r - c, h)
    d = r - c
    diag_ref[h] = jnp.where(d >= 0, bias_of(d, h), NEG)
    slast_ref[h] = bias_of(PAGE_SIZE + r8 - c8, h)
    dn = r8 - c8
    snew_ref[h] = jnp.where(dn >= 0, bias_of(dn, h), NEG)


def _bias_tiles(rel_bias):
  t = ATT_T
  return pl.pallas_call(
      _bias_kernel,
      in_specs=[pl.BlockSpec(memory_space=pltpu.SMEM)],
      out_shape=(jax.ShapeDtypeStruct((ATT_HEADS, t, t), F32),
                 jax.ShapeDtypeStruct((ATT_HEADS, t, t), F32),
                 jax.ShapeDtypeStruct((ATT_HEADS, SUBLANES, LANES), F32),
                 jax.ShapeDtypeStruct((ATT_HEADS, SUBLANES, LANES), F32)),
      compiler_params=pltpu.CompilerParams(vmem_limit_bytes=VMEM_LIMIT),
      name="bias_tiles",
  )(rel_bias)


def _inproj_kernel(x_ref, sc_ref, sh_ref, g_ref, wt_ref, wdt_ref, *out_refs, prompt):
  x = x_ref[0]
  var = jnp.mean(x * x, axis=-1, keepdims=True)
  h = x * lax.rsqrt(var + EPS) * g_ref[...]
  h = h * (1.0 + sc_ref[0]) + sh_ref[0]
  hb = h.astype(BF16)
  aw = ATT_WIDTH
  proj = lambda lo, hi: _dot_nt(hb, wt_ref[lo:hi, :])
  q = proj(0, aw) * (ATT_HEAD_DIM ** -0.5 * LOG2E)
  v = proj(2 * aw, 3 * aw)
  if prompt:
    q_ref, ktb_ref, vb_ref, kt_ref, v_ref, z_ref, xbc_ref, dt_ref = out_refs
    kt = _dot_nt(wt_ref[aw:2 * aw, :], hb)
    kt_ref[0] = kt
    t = ATT_T
    for hd in range(ATT_HEADS):
      sl = slice(hd * ATT_V_DIM, (hd + 1) * ATT_V_DIM)
      vb_ref[0, hd] = v[:, sl].astype(BF16)
      v_ref[0, pl.ds(hd, v.shape[0], stride=ATT_HEADS), :] = v[:, sl]
      q_ref[0, hd] = q[:, sl].astype(BF16)
      for cc in range(kt.shape[1] // t):
        ktb_ref[0, hd, cc] = kt[sl, cc * t:(cc + 1) * t].astype(BF16)
  else:
    q_ref, k_ref, v_ref, z_ref, xbc_ref, dt_ref = out_refs
    q_ref[0] = q
    k_ref[0] = proj(aw, 2 * aw)
    v_ref[0] = v
  o = 3 * aw
  z_ref[0] = proj(o, o + SSD_WIDTH)
  o += SSD_WIDTH
  xbc_ref[0] = proj(o, o + CONV_DIM)
  dt_ref[0] = _dot_nt(wdt_ref[...], hb) if prompt else _dot_nt(hb, wdt_ref[...])


def _mod_spec(mods, tm, col, b_of, i_of):
  d = mods.shape[2] // 6
  if mods.shape[1] != 1:
    return pl.BlockSpec((1, tm, d), lambda *g: (b_of(*g), i_of(*g), col))
  return pl.BlockSpec((1, 1, d), lambda *g: (b_of(*g), 0, col))


def _in_proj(x3, mods, g_mix, w_t, w_dt, tm, prompt):
  nb, tb, d = x3.shape
  nt = tb // tm
  mspec = lambda col: _mod_spec(mods, tm, col, lambda b, i: b, lambda b, i: i)
  hm = pl.BlockSpec((1, ATT_HEADS, tm, ATT_V_DIM), lambda b, i: (b, 0, i, 0))
  tok = lambda w: pl.BlockSpec((1, tm, w), lambda b, i: (b, i, 0))
  hm_shape = jax.ShapeDtypeStruct((nb, ATT_HEADS, tb, ATT_V_DIM), BF16)
  tok_shape = lambda w: jax.ShapeDtypeStruct((nb, tb, w), F32)
  if prompt:
    dt_spec = pl.BlockSpec((1, DT_ROWS, tm), lambda b, i: (b, 0, i))
    dt_shape = jax.ShapeDtypeStruct((nb, DT_ROWS, tb), F32)
  else:
    dt_spec, dt_shape = tok(LANES), tok_shape(LANES)
  tail_specs = [tok(SSD_WIDTH), tok(CONV_DIM), dt_spec]
  tail_shapes = [tok_shape(SSD_WIDTH), tok_shape(CONV_DIM), dt_shape]
  if prompt:
    t = ATT_T
    fm = pl.BlockSpec((1, ATT_HEADS, tm // t, ATT_V_DIM, t), lambda b, i: (b, 0, i, 0, 0))
    fm_shape = jax.ShapeDtypeStruct((nb, ATT_HEADS, tb // t, ATT_V_DIM, t), BF16)
    out_specs = [hm, fm, hm,
                 pl.BlockSpec((1, ATT_WIDTH, tm), lambda b, i: (b, 0, i)),
                 pl.BlockSpec((1, tm * ATT_HEADS, ATT_V_DIM), lambda b, i: (b, i, 0))] + tail_specs
    out_shape = [hm_shape, fm_shape, hm_shape,
                 jax.ShapeDtypeStruct((nb, ATT_WIDTH, tb), F32),
                 jax.ShapeDtypeStruct((nb, tb * ATT_HEADS, ATT_V_DIM), F32)] + tail_shapes
  else:
    out_specs = [tok(ATT_WIDTH)] * 3 + tail_specs
    out_shape = [tok_shape(ATT_WIDTH)] * 3 + tail_shapes
  return pl.pallas_call(
      functools.partial(_inproj_kernel, prompt=prompt),
      grid=(nb, nt),
      in_specs=[tok(d), mspec(MOD_SC1), mspec(MOD_SH1),
                pl.BlockSpec((1, d), lambda b, i: (0, 0)),
                pl.BlockSpec(w_t.shape, lambda b, i: (0, 0)),
                pl.BlockSpec(w_dt.shape, lambda b, i: (0, 0))],
      out_specs=out_specs,
      out_shape=out_shape,
      compiler_params=pltpu.CompilerParams(
          dimension_semantics=("arbitrary", "arbitrary"), vmem_limit_bytes=VMEM_LIMIT),
      name="in_proj",
  )(x3, mods, mods, g_mix.reshape(1, d), w_t, w_dt)


def _diff_lambda(lam_ref):
  lv = lam_ref[...]
  s1 = jnp.sum(lv[0:1] * lv[1:2], axis=-1, keepdims=True)
  s2 = jnp.sum(lv[2:3] * lv[3:4], axis=-1, keepdims=True)
  return jnp.exp(s1) - jnp.exp(s2) + LAM_INIT


def _diff_combine(o0, o1, lam, g):
  d = o0 - lam * o1
  d = d * lax.rsqrt(jnp.mean(d * d, axis=-1, keepdims=True) + EPS) * g
  return d * (1.0 - LAM_INIT)


def _attn_kernel(q_ref, k_ref, v_ref, near_ref, diag_ref, lam_ref, g_ref, o_ref,
                 m_sc, acc_sc):
  t = ATT_T
  qi = pl.program_id(1)
  lane = lax.broadcasted_iota(jnp.int32, (t, ATT_V_DIM), 1)
  m_sc[...] = jnp.full(m_sc.shape, NEG, F32)
  acc_sc[...] = jnp.zeros(acc_sc.shape, F32)

  def scores(h, c, bias_refs):
    q = q_ref[0, h]
    zero = jnp.zeros_like(q)
    q2 = jnp.concatenate([jnp.where(lane < ATT_HEAD_DIM, q, zero),
                          jnp.where(lane >= ATT_HEAD_DIM, q, zero)], axis=0)
    kc = jnp.concatenate([k_ref[0, h, c + j] for j in range(len(bias_refs))], axis=1)
    s = _dot(q2, kc)
    if any(b is not None for b in bias_refs):
      bias = jnp.concatenate([jnp.zeros((t, t), F32) if b is None else b[h] for b in bias_refs],
                             axis=1)
      s = s + jnp.concatenate([bias, bias], axis=0)
    return s

  def softmax_probs(h, s):
    m_prev = m_sc[h]
    m_new = jnp.maximum(m_prev, jnp.max(s, axis=-1, keepdims=True))
    alpha = jnp.exp2(m_prev - m_new)
    p = jnp.exp2(s - jnp.concatenate([m_new] * (s.shape[1] // LANES), axis=1))
    m_sc[h] = m_new
    return alpha, p.astype(BF16)

  def step(c, bias_refs):
    keys = len(bias_refs) * t
    start = pl.multiple_of(c * t, t)
    ones = jnp.ones((keys, LANES), BF16)
    for h in range(ATT_HEADS):
      alpha, pb = softmax_probs(h, scores(h, c, bias_refs))
      vaug = jnp.concatenate([v_ref[0, h, pl.ds(start, keys), :], ones], axis=1)
      acc_sc[h] = acc_sc[h] * jnp.concatenate([alpha, alpha], axis=1) + _dot(pb, vaug)

  n_far = jnp.maximum(qi - 1, 0)

  def far_pair(k, carry):
    step(2 * k, (None, None))
    return carry

  lax.fori_loop(0, n_far // 2, far_pair, 0)

  @pl.when(n_far % 2 == 1)
  def _():
    step(n_far - 1, (None,))

  @pl.when(qi >= 1)
  def _():
    step(qi - 1, (near_ref, diag_ref))

  @pl.when(qi == 0)
  def _():
    step(qi, (diag_ref,))

  lam = _diff_lambda(lam_ref)
  for h in range(ATT_HEADS):
    acc = acc_sc[h]
    o = acc[:, :ATT_V_DIM] / acc[:, ATT_V_DIM:]
    d = _diff_combine(o[:t], o[t:], lam, g_ref[...])
    o_ref[0, :, h * ATT_V_DIM:(h + 1) * ATT_V_DIM] = d.astype(o_ref.dtype)


def _prompt_attention(q_hm, kt_hm, v_hm, near, diag, lam4, subln_g):
  b, nh, s, e = q_hm.shape
  t = ATT_T
  nq = s // t
  return pl.pallas_call(
      _attn_kernel,
      grid=(b, nq),
      in_specs=[pl.BlockSpec((1, nh, t, e), lambda bi, qi: (bi, 0, qi, 0)),
                pl.BlockSpec((1, nh, nq, e, t), lambda bi, qi: (bi, 0, 0, 0, 0)),
                pl.BlockSpec((1, nh, s, e), lambda bi, qi: (bi, 0, 0, 0)),
                pl.BlockSpec((nh, t, t), lambda bi, qi: (0, 0, 0)),
                pl.BlockSpec((nh, t, t), lambda bi, qi: (0, 0, 0)),
                pl.BlockSpec((4, ATT_HEAD_DIM), lambda bi, qi: (0, 0)),
                pl.BlockSpec((1, e), lambda bi, qi: (0, 0))],
      out_specs=pl.BlockSpec((1, t, nh * e), lambda bi, qi: (bi, qi, 0)),
      out_shape=jax.ShapeDtypeStruct((b, s, nh * e), BF16),
      scratch_shapes=[pltpu.VMEM((nh, 2 * t, LANES), F32),
                      pltpu.VMEM((nh, 2 * t, 2 * LANES), F32)],
      compiler_params=pltpu.CompilerParams(
          dimension_semantics=("arbitrary", "arbitrary"), vmem_limit_bytes=VMEM_LIMIT),
      name="prompt_attention",
  )(q_hm, kt_hm, v_hm, near, diag, lam4, subln_g)


def _ssd_kernel(xbc_ref, z_ref, dt_ref, pre_ref, h0_ref, *refs, tv, nc, alternate):
  consts, (ssd_ref, conv_ref, h_ref, xp_sc) = refs[:8], refs[8:]
  hp = SUBLANES
  b = pl.program_id(0)
  c = pl.program_id(1)
  n_seq = xbc_ref.shape[0]

  @pl.when((b == 0) & (c == 0))
  def _():
    xp_sc[...] = jnp.zeros(xp_sc.shape, F32)

  @pl.when(c == 0)
  def _():
    for bb in range(n_seq):
      xp_sc[bb, hp - (CONV_W - 1):hp, :] = pre_ref[bb]
      h_ref[bb] = h0_ref[bb]

  stages = [_ssd_chunk(xbc_ref.at[bb], z_ref.at[bb], dt_ref.at[bb], *consts,
                       ssd_ref.at[bb], h_ref.at[bb], xp_sc.at[bb], tv=tv) for bb in range(n_seq)]
  for _ in (itertools.zip_longest(*stages) if alternate else itertools.chain(*stages)):
    pass

  @pl.when(c == nc - 1)
  def _():
    for bb in range(n_seq):
      conv_ref[bb] = xp_sc[bb, hp + tv - (CONV_W - 1):hp + tv, :]


def _ssd_chunk(xbc_ref, z_ref, dt_ref, cw_ref, cb_ref, dtb_ref, alog_ref, dsk_ref, ng_ref, e_ref,
               tri_ref, ssd_ref, h_ref, xp_sc, *, tv):
  L = SSD_CHUNK
  hp = SUBLANES
  xp_sc[hp:hp + tv, :] = xbc_ref[...]
  cw = cw_ref[...]
  y = cb_ref[...]
  cur = xp_sc[hp:hp + L, :]
  hist = xp_sc[0:hp, :]
  head_row = lax.broadcasted_iota(jnp.int32, (hp, CONV_DIM), 0)
  for i in range(CONV_W):
    back = CONV_W - 1 - i
    if back == 0:
      tap = cur
    else:
      rolled = pltpu.roll(cur, back, 0)
      head = jnp.where(head_row < back, pltpu.roll(hist, back, 0), rolled[0:hp])
      tap = jnp.concatenate([head, rolled[hp:]], axis=0)
    y = y + tap * cw[i:i + 1]
  xa = _silu(y)
  xp_sc[0:hp, :] = xp_sc[tv:tv + hp, :]
  yield

  xs = xa[:, :SSD_WIDTH]
  gw = SSD_GROUPS * D_STATE
  bmat = xa[:, SSD_WIDTH:SSD_WIDTH + gw].astype(BF16)
  cmat = xa[:, SSD_WIDTH + gw:].astype(BF16)

  dt_raw = dt_ref[...]
  if dt_raw.shape[0] != tv:
    dt_raw = jnp.concatenate([dt_raw, jnp.zeros((L - dt_raw.shape[0], L), F32)], axis=0).T
  x = dt_raw + dtb_ref[...]
  dtv = jnp.maximum(x, 0.0) + jnp.log1p(jnp.exp(-jnp.abs(x)))
  if tv < L:
    dtv = jnp.concatenate([dtv, jnp.zeros((L - tv, LANES), F32)], axis=0)
  a = dtv * (-jnp.exp(alog_ref[...]))

  tri = tri_ref[...]
  acs = sum(_dot(tri, part) for part in _split3(a))
  yield
  e = e_ref[...]
  aexp = sum(_dot(part, e) for part in _split3(acs))
  dtexp = _dot(dtv.astype(BF16), e)
  xd = xs * dtexp
  a_last = acs[L - 1:L, :]
  xdd_t = (xd * jnp.exp(aexp[L - 1:L, :] - aexp)).T
  acs_t = acs.T
  yield

  ri = lax.broadcasted_iota(jnp.int32, (L, L), 0)
  ci = lax.broadcasted_iota(jnp.int32, (L, L), 1)
  causal = ri >= ci
  lane = lax.broadcasted_iota(jnp.int32, (L, LANES), 1)
  heads_per_group = SSD_HEADS // SSD_GROUPS
  hprev = h_ref[...]
  hb = hprev.astype(BF16)

  ydiag, yoff, upd = [], [], []
  for g in range(SSD_GROUPS):
    cg = cmat[:, g * D_STATE:(g + 1) * D_STATE]
    bg = bmat[:, g * D_STATE:(g + 1) * D_STATE]
    scores = _dot_nt(cg, bg)
    for pair in range(heads_per_group // 2):
      h0 = g * heads_per_group + 2 * pair
      ms = []
      for h in (h0, h0 + 1):
        col = jnp.broadcast_to(acs[:, h:h + 1], (L, L))
        row = acs_t[h:h + 1, :]
        lm = jnp.exp(jnp.where(causal, col - row, NEG))
        ms.append((scores * lm).astype(BF16))
      xpair = xd[:, h0 * SSD_HEAD_DIM:(h0 + 2) * SSD_HEAD_DIM]
      xbd = jnp.concatenate([jnp.where(lane < SSD_HEAD_DIM, xpair, 0.0),
                             jnp.where(lane >= SSD_HEAD_DIM, xpair, 0.0)], axis=0)
      ydiag.append(_dot(jnp.concatenate(ms, axis=1), xbd.astype(BF16)))
      yield
    rows = slice(g * heads_per_group * SSD_HEAD_DIM, (g + 1) * heads_per_group * SSD_HEAD_DIM)
    yoff.append(_dot_nt(cg, hb[rows, :]))
    upd.append(_dot(xdd_t[rows, :].astype(BF16), bg))
    yield

  y = (jnp.concatenate(ydiag, axis=1) + jnp.concatenate(yoff, axis=1) * jnp.exp(aexp)
       + dsk_ref[...] * xs)
  yield

  decayed = []
  chunk_decay = jnp.exp(a_last)
  for h in range(SSD_HEADS):
    dec = jnp.broadcast_to(chunk_decay[:, h:h + 1], (SSD_HEAD_DIM, D_STATE))
    decayed.append(hprev[h * SSD_HEAD_DIM:(h + 1) * SSD_HEAD_DIM, :] * dec)
  h_ref[...] = jnp.concatenate(decayed, axis=0) + jnp.concatenate(upd, axis=0)
  yield

  yv = y[:tv] * _silu(z_ref[...])
  sq = yv * yv
  half = SSD_WIDTH // SSD_GROUPS
  r0 = lax.rsqrt(jnp.mean(sq[:, :half], axis=-1, keepdims=True) + EPS)
  r1 = lax.rsqrt(jnp.mean(sq[:, half:], axis=-1, keepdims=True) + EPS)
  out = jnp.concatenate([yv[:, :half] * r0, yv[:, half:] * r1], axis=1) * ng_ref[...]
  ssd_ref[...] = out.astype(ssd_ref.dtype)


def _ssd(xbc, z, dt, prefix, h0, conv_w, conv_b, dtb_p, alog_p, dsk_e, norm_g, tv, seqs_per_step):
  nb, tb, _ = xbc.shape
  nc = tb // tv
  ns = seqs_per_step
  assert nb % ns == 0
  e_np = np.zeros((LANES, SSD_WIDTH), np.float32)
  for h in range(SSD_HEADS):
    e_np[h, h * SSD_HEAD_DIM:(h + 1) * SSD_HEAD_DIM] = 1.0
  tri_np = np.tril(np.ones((SSD_CHUNK, SSD_CHUNK), np.float32))
  full = lambda shape: pl.BlockSpec(shape, lambda b, c: (0,) * len(shape))
  tok = lambda w: pl.BlockSpec((ns, tv, w), lambda b, c: (b, c, 0))
  per_b = lambda r, w: pl.BlockSpec((ns, r, w), lambda b, c: (b, 0, 0))
  if dt.shape[1] == tb:
    dt_spec = tok(LANES)
  else:
    assert tv == SSD_CHUNK and dt.shape == (nb, DT_ROWS, tb)
    dt_spec = pl.BlockSpec((ns, DT_ROWS, tv), lambda b, c: (b, 0, c))
  return pl.pallas_call(
      functools.partial(_ssd_kernel, tv=tv, nc=nc, alternate=tv < SSD_CHUNK),
      grid=(nb // ns, nc),
      in_specs=[tok(CONV_DIM), tok(SSD_WIDTH), dt_spec,
                per_b(CONV_W - 1, CONV_DIM), per_b(SSD_WIDTH, D_STATE),
                full((CONV_W, CONV_DIM)), full((1, CONV_DIM)), full((1, LANES)), full((1, LANES)),
                full((1, SSD_WIDTH)), full((1, SSD_WIDTH)),
                full((LANES, SSD_WIDTH)), full((SSD_CHUNK, SSD_CHUNK))],
      out_specs=[tok(SSD_WIDTH), per_b(CONV_W - 1, CONV_DIM), per_b(SSD_WIDTH, D_STATE)],
      out_shape=[jax.ShapeDtypeStruct((nb, tb, SSD_WIDTH), BF16),
                 jax.ShapeDtypeStruct((nb, CONV_W - 1, CONV_DIM), F32),
                 jax.ShapeDtypeStruct((nb, SSD_WIDTH, D_STATE), F32)],
      scratch_shapes=[pltpu.VMEM((ns, SSD_CHUNK + SUBLANES, CONV_DIM), F32)],
      compiler_params=pltpu.CompilerParams(
          dimension_semantics=("arbitrary", "arbitrary"), vmem_limit_bytes=VMEM_LIMIT),
      name="ssd_scan",
  )(xbc, z, dt, prefix, h0, conv_w, conv_b, dtb_p, alog_p, dsk_e, norm_g,
    jnp.asarray(e_np, BF16), jnp.asarray(tri_np, BF16))


def _ffn_pre(x_ref, att_ref, ssd_ref, ga1_ref, sc2_ref, sh2_ref, gf_ref, wo_ref):
  mix = _dot(att_ref[0], wo_ref[0:ATT_WIDTH, :]) + _dot(ssd_ref[0], wo_ref[ATT_WIDTH:, :])
  x1 = x_ref[0] + ga1_ref[0] * mix
  var = jnp.mean(x1 * x1, axis=-1, keepdims=True)
  h2 = x1 * lax.rsqrt(var + EPS) * gf_ref[...]
  return x1, (h2 * (1.0 + sc2_ref[0]) + sh2_ref[0]).astype(BF16)


def _ffn_piece(h2, wu_ref, wd_ref, lo, hi):
  u = jnp.maximum(_dot(h2, wu_ref[:, lo:hi]), 0.0)
  return _dot((u * u).astype(BF16), wd_ref[lo:hi, :])


def _ffn_post(x1, acc, ga2_ref, gl_ref):
  x2 = x1 + ga2_ref[0] * acc
  var2 = jnp.mean(x2 * x2, axis=-1, keepdims=True)
  return x2 * lax.rsqrt(var2 + EPS) * gl_ref[...]


def _ffn_kernel(x_ref, att_ref, ssd_ref, ga1_ref, sc2_ref, sh2_ref, ga2_ref, gf_ref, gl_ref,
                wo_ref, wu_ref, wd_ref, y_ref):
  x1, h2 = _ffn_pre(x_ref, att_ref, ssd_ref, ga1_ref, sc2_ref, sh2_ref, gf_ref, wo_ref)
  fc = D_MODEL
  acc = jnp.zeros(x1.shape, F32)
  for f in range(D_FF // fc):
    acc = acc + _ffn_piece(h2, wu_ref, wd_ref, f * fc, (f + 1) * fc)
  y_ref[0] = _ffn_post(x1, acc, ga2_ref, gl_ref)


def _out_ffn(x3, att, ssd, mods, g_ffn, g_final, w_out_b, w_up_b, w_down_b, tm):
  nb, tb, d = x3.shape
  nt = tb // tm
  mspec = lambda col: _mod_spec(mods, tm, col, lambda b, i: b, lambda b, i: i)
  tok = lambda w: pl.BlockSpec((1, tm, w), lambda b, i: (b, i, 0))
  const = lambda shape: pl.BlockSpec(shape, lambda b, i: (0, 0), pipeline_mode=pl.Buffered(1))
  return pl.pallas_call(
      _ffn_kernel,
      grid=(nb, nt),
      in_specs=[tok(d), tok(ATT_WIDTH), tok(SSD_WIDTH),
                mspec(MOD_GA1), mspec(MOD_SC2), mspec(MOD_SH2), mspec(MOD_GA2),
                const((1, d)), const((1, d)),
                const((d, d)), const((d, D_FF)), const((D_FF, d))],
      out_specs=tok(d),
      out_shape=jax.ShapeDtypeStruct((nb, tb, d), F32),
      compiler_params=pltpu.CompilerParams(
          dimension_semantics=("arbitrary", "arbitrary"), vmem_limit_bytes=VMEM_LIMIT),
      name="out_ffn",
  )(x3, att, ssd, mods, mods, mods, mods, g_ffn.reshape(1, d), g_final.reshape(1, d),
    w_out_b, w_up_b, w_down_b)


def _ffn_sample_kernel(pt_ref, x_ref, att_ref, ssd_ref, ga1_ref, sc2_ref, sh2_ref, ga2_ref,
                       gf_ref, gl_ref, wo_ref, wu_ref, wd_ref,
                       qn_ref, kn_ref, vn_ref, blast_ref, bnew_ref, lam_ref, g_ref, ck_hbm, cv_hbm,
                       y_ref, o_ref,
                       kbuf, vbuf, sem, m_sc, l_sc, acc_sc, kpad, vpad, facc_sc,
                       *, n_steps, steps_per_seq):
  i = pl.program_id(0)
  seq = i // steps_per_seq
  part = i % steps_per_seq
  rps = ROUNDS_PER_STEP
  ahead = ROUND_SLOTS - 1
  last_part = part == steps_per_seq - 1

  def round_copies(r):
    if r < rps:
      sq, rnd = seq, part * rps + r
    else:
      nxt = i + 1
      sq, rnd = nxt // steps_per_seq, (nxt % steps_per_seq) * rps + r - rps
    sl = r % ROUND_SLOTS
    copies = []
    for p in range(PAGES_PER_GROUP):
      pg = pt_ref[sq, rnd * PAGES_PER_GROUP + p]
      copies.append((pltpu.make_async_copy(ck_hbm.at[pg], kbuf.at[sl, p], sem.at[0, sl]), 0))
      copies.append((pltpu.make_async_copy(cv_hbm.at[pg], vbuf.at[sl, p], sem.at[1, sl]), 1))
    return copies

  def start_round(r):
    for cp, prio in round_copies(r):
      cp.start(priority=prio)

  @pl.when(i == 0)
  def _():
    kpad[...] = jnp.zeros(kpad.shape, kpad.dtype)
    vpad[...] = jnp.zeros(vpad.shape, vpad.dtype)
    for r in range(ahead):
      start_round(r)

  @pl.when(part == 0)
  def _():
    m_sc[...] = jnp.full(m_sc.shape, NEG, F32)
    l_sc[...] = jnp.zeros(l_sc.shape, F32)
    acc_sc[...] = jnp.zeros(acc_sc.shape, F32)

  qn = qn_ref[0].astype(BF16)
  half_of_lane = lax.broadcasted_iota(jnp.int32, qn.shape, 1) // ATT_HEAD_DIM
  qb = jnp.concatenate([jnp.where(half_of_lane == hj, qn, jnp.zeros_like(qn))
                        for hj in range(2 * ATT_HEADS)], axis=0)
  rows_per_head = 2 * SUBLANES

  def softmax_probs(s):
    m_prev = m_sc[...]
    m_new = jnp.maximum(m_prev, jnp.max(s, axis=-1, keepdims=True))
    alpha = jnp.exp2(m_prev - m_new)
    p = jnp.exp2(s - m_new[:, 0:1])
    l_sc[...] = l_sc[...] * alpha + jnp.sum(p, axis=-1, keepdims=True)
    m_sc[...] = m_new
    return alpha, p.astype(BF16)

  def accumulate(alpha, pb, values_of_head):
    pv = [_dot(pb[h * rows_per_head:(h + 1) * rows_per_head, :], values_of_head(h))
          for h in range(ATT_HEADS)]
    acc_sc[...] = acc_sc[...] * alpha + jnp.concatenate(pv, axis=0)

  def round_scores(r):
    sl = r % ROUND_SLOTS
    kt = jnp.concatenate([kbuf[sl, p].astype(BF16) for p in range(PAGES_PER_GROUP)], axis=1)
    hr, hf = qb.shape[0] // 2, qb.shape[1] // 2
    s = jnp.concatenate([_dot(qb[:hr, :hf], kt[:hf]), _dot(qb[hr:, hf:], kt[hf:])], axis=0)
    if r == rps - 1:
      tail = s[:, -PAGE_SIZE:] + jnp.where(last_part, blast_ref[...], 0.0)
      s = jnp.concatenate([s[:, :-PAGE_SIZE], tail], axis=1)
    return s

  def round_values(r):
    sl = r % ROUND_SLOTS

    def values_of_head(h):
      v = [vbuf[sl, p, pl.ds(h, PAGE_SIZE, stride=ATT_HEADS), :] for p in range(PAGES_PER_GROUP)]
      return jnp.concatenate(v, axis=0).astype(BF16)

    return values_of_head

  x1, h2 = _ffn_pre(x_ref, att_ref, ssd_ref, ga1_ref, sc2_ref, sh2_ref, gf_ref, wo_ref)
  fc = D_FF // rps
  for r in range(rps):
    for cp, _ in round_copies(r):
      cp.wait()
    if r + ahead < rps:
      start_round(r + ahead)
    else:
      pl.when(i + 1 < n_steps)(functools.partial(start_round, r + ahead))
    s = round_scores(r)
    u = jnp.maximum(_dot(h2, wu_ref[:, r * fc:(r + 1) * fc]), 0.0)
    ub = (u * u).astype(BF16)
    alpha, pb = softmax_probs(s)
    accumulate(alpha, pb, round_values(r))
    piece = _dot(ub, wd_ref[r * fc:(r + 1) * fc, :])
    facc_sc[...] = piece if r == 0 else facc_sc[...] + piece
  y_ref[0] = _ffn_post(x1, facc_sc[...], ga2_ref, gl_ref)

  @pl.when(last_part)
  def _():
    t_new = kn_ref.shape[1]
    kpad[0:t_new, :] = kn_ref[0].astype(BF16)
    vpad[0:t_new, :] = vn_ref[0].astype(BF16)
    sn = _dot_nt(qb, kpad[...]) + bnew_ref[...]
    accumulate(*softmax_probs(sn), lambda h: vpad[:, h * ATT_V_DIM:(h + 1) * ATT_V_DIM])
    o = acc_sc[...] / l_sc[...]
    lam = _diff_lambda(lam_ref)
    for h in range(ATT_HEADS):
      o0 = o[(2 * h) * SUBLANES:(2 * h + 1) * SUBLANES, :]
      o1 = o[(2 * h + 1) * SUBLANES:(2 * h + 2) * SUBLANES, :]
      o_ref[0, :, h * ATT_V_DIM:(h + 1) * ATT_V_DIM] = _diff_combine(
          o0, o1, lam, g_ref[...]).astype(o_ref.dtype)


def _ffn_with_sample_attention(x_p, att_p, ssd_p, mods, g_ffn, g_final,
                               w_out_b, w_up_b, w_down_b,
                               page_table, q_new, k_new, v_new, blast, bnew, lam4, subln_g,
                               cache_k, cache_v):
  nb, tb, d = x_p.shape
  tm = FFN_TM
  tiles_per_b = tb // tm
  n_steps = nb * tiles_per_b
  n_seq, n_pages = page_table.shape
  t_new = k_new.shape[1]
  pages_per_step = PAGES_PER_GROUP * ROUNDS_PER_STEP
  steps_per_seq = n_pages // pages_per_step
  ring = ROUND_SLOTS
  q_rows = 2 * ATT_HEADS * t_new
  assert t_new == SUBLANES and n_pages % pages_per_step == 0
  assert n_seq * steps_per_seq == n_steps, "one batch of page rounds per FFN tile"
  assert ROUNDS_PER_STEP % ROUND_SLOTS == 0, "ring slots must be static per round"
  w = ATT_WIDTH
  assert cache_k.shape[1:] == (w, PAGE_SIZE) and cache_v.shape[1:] == (PAGE_SIZE * ATT_HEADS, ATT_V_DIM)
  tok = lambda width: pl.BlockSpec((1, tm, width),
                                   lambda i, pt: (i // tiles_per_b, i % tiles_per_b, 0))
  mspec = lambda col: _mod_spec(mods, tm, col, lambda i, pt: i // tiles_per_b,
                                lambda i, pt: i % tiles_per_b)
  const = lambda shape: pl.BlockSpec(shape, lambda i, pt: (0,) * len(shape),
                                     pipeline_mode=pl.Buffered(1))
  per_seq = lambda r: pl.BlockSpec((1, r, w), lambda i, pt: (i // steps_per_seq, 0, 0))
  grid_spec = pltpu.PrefetchScalarGridSpec(
      num_scalar_prefetch=1,
      grid=(n_steps,),
      in_specs=[tok(d), tok(ATT_WIDTH), tok(SSD_WIDTH),
                mspec(MOD_GA1), mspec(MOD_SC2), mspec(MOD_SH2), mspec(MOD_GA2),
                const((1, d)), const((1, d)),
                const((d, d)), const((d, D_FF)), const((D_FF, d)),
                per_seq(t_new), per_seq(t_new), per_seq(t_new),
                const((q_rows, LANES)), const((q_rows, LANES)),
                const((4, ATT_HEAD_DIM)), const((1, ATT_V_DIM)),
                pl.BlockSpec(memory_space=pl.ANY),
                pl.BlockSpec(memory_space=pl.ANY)],
      out_specs=[tok(d), per_seq(t_new)],
      scratch_shapes=[pltpu.VMEM((ring, PAGES_PER_GROUP) + cache_k.shape[1:], F32),
                      pltpu.VMEM((ring, PAGES_PER_GROUP) + cache_v.shape[1:], F32),
                      pltpu.SemaphoreType.DMA((2, ring)),
                      pltpu.VMEM((q_rows, LANES), F32),
                      pltpu.VMEM((q_rows, LANES), F32),
                      pltpu.VMEM((q_rows, ATT_V_DIM), F32),
                      pltpu.VMEM((PAGE_SIZE, w), BF16),
                      pltpu.VMEM((PAGE_SIZE, w), BF16),
                      pltpu.VMEM((tm, d), F32)])
  return pl.pallas_call(
      functools.partial(_ffn_sample_kernel, n_steps=n_steps, steps_per_seq=steps_per_seq),
      grid_spec=grid_spec,
      out_shape=[jax.ShapeDtypeStruct((nb, tb, d), F32),
                 jax.ShapeDtypeStruct((n_seq, t_new, w), BF16)],
      compiler_params=pltpu.CompilerParams(
          dimension_semantics=("arbitrary",), vmem_limit_bytes=VMEM_LIMIT),
      name="ffn_sample_attention",
  )(page_table, x_p, att_p, ssd_p, mods, mods, mods, mods,
    g_ffn.reshape(1, d), g_final.reshape(1, d), w_out_b, w_up_b, w_down_b,
    q_new, k_new, v_new, blast, bnew, lam4, subln_g, cache_k, cache_v)


def kernel(x_prompt, x_sample, cache_k, cache_v, state_conv, state_ssm, page_table, c_prompt,
           c_sample, rel_bias, w_ada, b_ada, g_mix, g_ffn, w_in, w_out, lam_q1, lam_k1, lam_q2,
           lam_k2, subln_g, conv_w, conv_b, dt_bias, a_log, d_skip, ssd_norm_g, w_up, w_down,
           g_final):
  assert w_ada.shape[0] == 1, "single-layer step"
  bp, sp, d = x_prompt.shape
  bs, ts, _ = x_sample.shape
  n_pool = cache_k.shape[1]

  w_in_t = w_in[0].T.astype(BF16)
  w_t = w_in_t[:IN_MAIN]
  w_dt_p = jnp.pad(w_in_t[IN_MAIN:], ((0, DT_ROWS - SSD_HEADS), (0, 0)))
  w_dt_s = jnp.pad(w_in_t[IN_MAIN:], ((0, LANES - SSD_HEADS), (0, 0)))
  w_out_b = w_out[0].astype(BF16)
  w_up_b = w_up[0].astype(BF16)
  w_down_b = w_down[0].astype(BF16)
  lam4 = jnp.concatenate([lam_q1, lam_k1, lam_q2, lam_k2], axis=0)
  subln = subln_g.reshape(1, ATT_V_DIM)
  dtb_p = jnp.pad(dt_bias, ((0, 0), (0, LANES - SSD_HEADS)))
  alog_p = jnp.pad(a_log, ((0, 0), (0, LANES - SSD_HEADS)))
  dsk_e = jnp.repeat(d_skip[0], SSD_HEAD_DIM).reshape(1, SSD_WIDTH)
  ssd_args = (conv_w[0], conv_b, dtb_p, alog_p, dsk_e, ssd_norm_g)

  mods = _modulation(jnp.concatenate([c_prompt, c_sample], axis=0), w_ada[0], b_ada[0])
  mods_p = mods[:bp].reshape(bp, 1, 6 * d)
  mods_s = jnp.repeat(mods[bp:], ts, axis=0).reshape(1, bs * ts, 6 * d)
  near, diag, slast, snew = _bias_tiles(rel_bias)

  q_hm, kt_hm, v_hm, kt_p, v_p, z_p, xbc_p, dt_p = _in_proj(
      x_prompt, mods_p, g_mix[0], w_t, w_dt_p, tm=PROJ_TM, prompt=True)
  att_p = _prompt_attention(q_hm, kt_hm, v_hm, near, diag, lam4, subln)
  ssd_p, conv_p, h_p = _ssd(
      xbc_p, z_p, dt_p, jnp.zeros((bp, CONV_W - 1, CONV_DIM), F32),
      jnp.zeros((bp, SSD_WIDTH, D_STATE), F32), *ssd_args, tv=SSD_CHUNK, seqs_per_step=SSD_SEQS_PER_STEP)

  n_tok = bs * ts
  xs3 = x_sample.reshape(1, n_tok, d)
  q_s, k_s, v_s, z_s, xbc_s, dt_s = _in_proj(
      xs3, mods_s, g_mix[0], w_t, w_dt_s, tm=n_tok, prompt=False)
  blast = jnp.repeat(slast, 2, axis=0).reshape(2 * ATT_HEADS * SUBLANES, LANES)
  bnew = jnp.repeat(snew, 2, axis=0).reshape(2 * ATT_HEADS * SUBLANES, LANES)
  per_seq = lambda a: a.reshape(bs, ts, ATT_WIDTH)
  y_p, att_s = _ffn_with_sample_attention(
      x_prompt, att_p, ssd_p, mods_p, g_ffn[0], g_final, w_out_b, w_up_b, w_down_b,
      page_table, per_seq(q_s), per_seq(k_s), per_seq(v_s), blast, bnew, lam4, subln,
      jnp.transpose(cache_k[0], (0, 2, 3, 4, 1)).reshape(n_pool, ATT_WIDTH, PAGE_SIZE),
      cache_v[0].reshape(n_pool, PAGE_SIZE * ATT_HEADS, ATT_V_DIM))
  ssd_s, conv_s, h_s = _ssd(
      xbc_s.reshape(bs, ts, CONV_DIM), z_s.reshape(bs, ts, SSD_WIDTH), dt_s.reshape(bs, ts, LANES),
      state_conv[0], state_ssm[0].reshape(bs, SSD_WIDTH, D_STATE), *ssd_args, tv=ts,
      seqs_per_step=SSD_SEQS_PER_STEP)
  y_s = _out_ffn(xs3, att_s.reshape(1, n_tok, ATT_WIDTH), ssd_s.reshape(1, n_tok, SSD_WIDTH),
                 mods_s, g_ffn[0], g_final, w_out_b, w_up_b, w_down_b, tm=n_tok)

  hshape = (SSD_HEADS, SSD_HEAD_DIM, D_STATE)
  return (y_p, y_s.reshape(bs, ts, d),
          kt_p.reshape(1, bp, ATT_HEADS, 2, ATT_HEAD_DIM, sp).transpose(0, 1, 5, 2, 3, 4),
          v_p.reshape(1, bp, sp, ATT_HEADS, ATT_V_DIM),
          conv_p[None], h_p.reshape(1, bp, *hshape),
          k_s.reshape(1, bs, ts, ATT_HEADS, 2, ATT_HEAD_DIM),
          v_s.reshape(1, bs, ts, ATT_HEADS, ATT_V_DIM),
          conv_s[None], h_s.reshape(1, bs, *hshape))
```

```python
import functools
import itertools
import math

import numpy as np
import jax
import jax.numpy as jnp
from jax import lax
from jax.experimental import pallas as pl
from jax.experimental.pallas import tpu as pltpu

F32 = jnp.float32
BF16 = jnp.bfloat16

D_MODEL = 1024
PAGE_SIZE = 128
ATT_HEADS = 4
ATT_HEAD_DIM = 64
ATT_V_DIM = 2 * ATT_HEAD_DIM
ATT_WIDTH = ATT_HEADS * ATT_V_DIM
SSD_HEADS = 8
SSD_HEAD_DIM = 64
SSD_WIDTH = SSD_HEADS * SSD_HEAD_DIM
SSD_GROUPS = 2
D_STATE = 128
CONV_W = 4
CONV_DIM = SSD_WIDTH + 2 * SSD_GROUPS * D_STATE
SSD_CHUNK = 128
IN_WIDTH = 3 * ATT_WIDTH + SSD_WIDTH + CONV_DIM + SSD_HEADS
D_FF = 4 * D_MODEL
N_BUCKETS = 32
MAX_DISTANCE = 128
MAX_EXACT = N_BUCKETS // 2
EPS = 1e-6
LAM_INIT = 0.8 - 0.6 * math.exp(-0.3 * 0)
MOD_SH1, MOD_SC1, MOD_GA1, MOD_SH2, MOD_SC2, MOD_GA2 = range(6)

LANES = 128
SUBLANES = 8
IN_MAIN = IN_WIDTH - SSD_HEADS
DT_ROWS = 2 * SUBLANES
LOG2E = 1.4426950408889634
NEG = -1e30
VMEM_LIMIT = 56 * 1024 * 1024

PROJ_TM = 512
SSD_SEQS_PER_STEP = 4
SSD_SHORT_SEQS_PER_STEP = 8
ATT_T = 256
FFN_TM = 256
PAGES_PER_GROUP = 8
ROUNDS_PER_STEP = 8
ROUND_SLOTS = 4


def _bucket_lower_bounds():
  d = np.arange(0, 4 * MAX_DISTANCE)
  nf = np.maximum(d, 1).astype(np.float64)
  large = MAX_EXACT + (np.log(nf / MAX_EXACT) / math.log(MAX_DISTANCE / MAX_EXACT)
                       * (N_BUCKETS - MAX_EXACT)).astype(np.int64)
  large = np.minimum(large, N_BUCKETS - 1)
  bucket = np.where(d < MAX_EXACT, d, large)
  return [int(np.argmax(bucket >= b)) for b in range(N_BUCKETS)]


_LOWER = _bucket_lower_bounds()
FAR_DIST = _LOWER[N_BUCKETS - 1]
assert FAR_DIST <= PAGE_SIZE and FAR_DIST <= ATT_T


def _silu(x):
  h = 0.5 * x
  return h * jnp.tanh(h) + h


def _dot(a, b):
  return jnp.dot(a, b, preferred_element_type=F32)


def _dot_nt(a, b):
  return lax.dot_general(a, b, (((1,), (1,)), ((), ())), preferred_element_type=F32)


def _split3(x):
  hi = x.astype(BF16)
  r1 = x - hi.astype(F32)
  mid = r1.astype(BF16)
  lo = (r1 - mid.astype(F32)).astype(BF16)
  return hi, mid, lo


def _mod_kernel(c_ref, w_ref, b_ref, o_ref):
  s = _silu(c_ref[...]).astype(BF16)
  o_ref[...] = _dot(s, w_ref[...].astype(BF16)) + b_ref[...]


def _modulation(c_all, w_ada, b_ada):
  n = c_all.shape[0]
  tn = D_MODEL
  return pl.pallas_call(
      _mod_kernel,
      grid=(6 * D_MODEL // tn,),
      in_specs=[pl.BlockSpec((n, D_MODEL), lambda j: (0, 0)),
                pl.BlockSpec((D_MODEL, tn), lambda j: (0, j)),
                pl.BlockSpec((1, tn), lambda j: (0, j))],
      out_specs=pl.BlockSpec((n, tn), lambda j: (0, j)),
      out_shape=jax.ShapeDtypeStruct((n, 6 * D_MODEL), F32),
      compiler_params=pltpu.CompilerParams(vmem_limit_bytes=VMEM_LIMIT),
      name="modulation",
  )(c_all, w_ada, b_ada.reshape(1, -1))


def _bias_kernel(tab_ref, near_ref, diag_ref, slast_ref, snew_ref):
  def bias_of(dist, h):
    val = jnp.full(dist.shape, tab_ref[0, h], F32)
    for b in range(1, N_BUCKETS):
      val = jnp.where(dist >= _LOWER[b], tab_ref[b, h], val)
    return (val - tab_ref[N_BUCKETS - 1, h]) * LOG2E

  t = ATT_T
  r = lax.broadcasted_iota(jnp.int32, (t, t), 0)
  c = lax.broadcasted_iota(jnp.int32, (t, t), 1)
  r8 = lax.broadcasted_iota(jnp.int32, (SUBLANES, LANES), 0)
  c8 = lax.broadcasted_iota(jnp.int32, (SUBLANES, LANES), 1)
  for h in range(ATT_HEADS):
    near_ref[h] = bias_of(t + r - c, h)
    d = r - c
    diag_ref[h] = jnp.where(d >= 0, bias_of(d, h), NEG)
    slast_ref[h] = bias_of(PAGE_SIZE + r8 - c8, h)
    dn = r8 - c8
    snew_ref[h] = jnp.where(dn >= 0, bias_of(dn, h), NEG)


def _bias_tiles(rel_bias):
  t = ATT_T
  return pl.pallas_call(
      _bias_kernel,
      in_specs=[pl.BlockSpec(memory_space=pltpu.SMEM)],
      out_shape=(jax.ShapeDtypeStruct((ATT_HEADS, t, t), F32),
                 jax.ShapeDtypeStruct((ATT_HEADS, t, t), F32),
                 jax.ShapeDtypeStruct((ATT_HEADS, SUBLANES, LANES), F32),
                 jax.ShapeDtypeStruct((ATT_HEADS, SUBLANES, LANES), F32)),
      compiler_params=pltpu.CompilerParams(vmem_limit_bytes=VMEM_LIMIT),
      name="bias_tiles",
  )(rel_bias)


def _inproj_kernel(x_ref, sc_ref, sh_ref, g_ref, wt_ref, wdt_ref, *out_refs, prompt):
  x = x_ref[0]
  var = jnp.mean(x * x, axis=-1, keepdims=True)
  h = x * lax.rsqrt(var + EPS) * g_ref[...]
  h = h * (1.0 + sc_ref[0]) + sh_ref[0]
  hb = h.astype(BF16)
  aw = ATT_WIDTH
  proj = lambda lo, hi: _dot_nt(hb, wt_ref[lo:hi, :])
  q = proj(0, aw) * (ATT_HEAD_DIM ** -0.5 * LOG2E)
  v = proj(2 * aw, 3 * aw)
  if prompt:
    q_ref, ktb_ref, vb_ref, kt_ref, v_ref, z_ref, xbc_ref, dt_ref = out_refs
    kt = _dot_nt(wt_ref[aw:2 * aw, :], hb)
    kt_ref[0] = kt
    t = ATT_T
    for hd in range(ATT_HEADS):
      sl = slice(hd * ATT_V_DIM, (hd + 1) * ATT_V_DIM)
      vb_ref[0, hd] = v[:, sl].astype(BF16)
      v_ref[0, pl.ds(hd, v.shape[0], stride=ATT_HEADS), :] = v[:, sl]
      q_ref[0, hd] = q[:, sl].astype(BF16)
      for cc in range(kt.shape[1] // t):
        ktb_ref[0, hd, cc] = kt[sl, cc * t:(cc + 1) * t].astype(BF16)
  else:
    q_ref, k_ref, v_ref, z_ref, xbc_ref, dt_ref = out_refs
    q_ref[0] = q
    k_ref[0] = proj(aw, 2 * aw)
    v_ref[0] = v
  o = 3 * aw
  z_ref[0] = proj(o, o + SSD_WIDTH)
  o += SSD_WIDTH
  xbc_ref[0] = proj(o, o + CONV_DIM)
  dt_ref[0] = _dot_nt(wdt_ref[...], hb) if prompt else _dot_nt(hb, wdt_ref[...])


def _mod_spec(mods, tm, col, b_of, i_of):
  d = mods.shape[2] // 6
  if mods.shape[1] != 1:
    return pl.BlockSpec((1, tm, d), lambda *g: (b_of(*g), i_of(*g), col))
  return pl.BlockSpec((1, 1, d), lambda *g: (b_of(*g), 0, col))


def _in_proj(x3, mods, g_mix, w_t, w_dt, tm, prompt):
  nb, tb, d = x3.shape
  nt = tb // tm
  mspec = lambda col: _mod_spec(mods, tm, col, lambda b, i: b, lambda b, i: i)
  hm = pl.BlockSpec((1, ATT_HEADS, tm, ATT_V_DIM), lambda b, i: (b, 0, i, 0))
  tok = lambda w: pl.BlockSpec((1, tm, w), lambda b, i: (b, i, 0))
  hm_shape = jax.ShapeDtypeStruct((nb, ATT_HEADS, tb, ATT_V_DIM), BF16)
  tok_shape = lambda w: jax.ShapeDtypeStruct((nb, tb, w), F32)
  if prompt:
    dt_spec = pl.BlockSpec((1, DT_ROWS, tm), lambda b, i: (b, 0, i))
    dt_shape = jax.ShapeDtypeStruct((nb, DT_ROWS, tb), F32)
  else:
    dt_spec, dt_shape = tok(LANES), tok_shape(LANES)
  tail_specs = [tok(SSD_WIDTH), tok(CONV_DIM), dt_spec]
  tail_shapes = [tok_shape(SSD_WIDTH), tok_shape(CONV_DIM), dt_shape]
  if prompt:
    t = ATT_T
    fm = pl.BlockSpec((1, ATT_HEADS, tm // t, ATT_V_DIM, t), lambda b, i: (b, 0, i, 0, 0))
    fm_shape = jax.ShapeDtypeStruct((nb, ATT_HEADS, tb // t, ATT_V_DIM, t), BF16)
    out_specs = [hm, fm, hm,
                 pl.BlockSpec((1, ATT_WIDTH, tm), lambda b, i: (b, 0, i)),
                 pl.BlockSpec((1, tm * ATT_HEADS, ATT_V_DIM), lambda b, i: (b, i, 0))] + tail_specs
    out_shape = [hm_shape, fm_shape, hm_shape,
                 jax.ShapeDtypeStruct((nb, ATT_WIDTH, tb), F32),
                 jax.ShapeDtypeStruct((nb, tb * ATT_HEADS, ATT_V_DIM), F32)] + tail_shapes
  else:
    out_specs = [tok(ATT_WIDTH)] * 3 + tail_specs
    out_shape = [tok_shape(ATT_WIDTH)] * 3 + tail_shapes
  return pl.pallas_call(
      functools.partial(_inproj_kernel, prompt=prompt),
      grid=(nb, nt),
      in_specs=[tok(d), mspec(MOD_SC1), mspec(MOD_SH1),
                pl.BlockSpec((1, d), lambda b, i: (0, 0)),
                pl.BlockSpec(w_t.shape, lambda b, i: (0, 0)),
                pl.BlockSpec(w_dt.shape, lambda b, i: (0, 0))],
      out_specs=out_specs,
      out_shape=out_shape,
      compiler_params=pltpu.CompilerParams(
          dimension_semantics=("arbitrary", "arbitrary"), vmem_limit_bytes=VMEM_LIMIT),
      name="in_proj",
  )(x3, mods, mods, g_mix.reshape(1, d), w_t, w_dt)


def _diff_lambda(lam_ref):
  lv = lam_ref[...]
  s1 = jnp.sum(lv[0:1] * lv[1:2], axis=-1, keepdims=True)
  s2 = jnp.sum(lv[2:3] * lv[3:4], axis=-1, keepdims=True)
  return jnp.exp(s1) - jnp.exp(s2) + LAM_INIT


def _diff_combine(o0, o1, lam, g):
  d = o0 - lam * o1
  d = d * lax.rsqrt(jnp.mean(d * d, axis=-1, keepdims=True) + EPS) * g
  return d * (1.0 - LAM_INIT)


def _attn_kernel(q_ref, k_ref, v_ref, near_ref, diag_ref, lam_ref, g_ref, o_ref,
                 m_sc, acc_sc):
  t = ATT_T
  qi = pl.program_id(1)
  lane = lax.broadcasted_iota(jnp.int32, (t, ATT_V_DIM), 1)
  m_sc[...] = jnp.full(m_sc.shape, NEG, F32)
  acc_sc[...] = jnp.zeros(acc_sc.shape, F32)

  def scores(h, c, bias_refs):
    q = q_ref[0, h]
    zero = jnp.zeros_like(q)
    q2 = jnp.concatenate([jnp.where(lane < ATT_HEAD_DIM, q, zero),
                          jnp.where(lane >= ATT_HEAD_DIM, q, zero)], axis=0)
    kc = jnp.concatenate([k_ref[0, h, c + j] for j in range(len(bias_refs))], axis=1)
    s = _dot(q2, kc)
    if any(b is not None for b in bias_refs):
      bias = jnp.concatenate([jnp.zeros((t, t), F32) if b is None else b[h] for b in bias_refs],
                             axis=1)
      s = s + jnp.concatenate([bias, bias], axis=0)
    return s

  def softmax_probs(h, s):
    m_prev = m_sc[h]
    m_new = jnp.maximum(m_prev, jnp.max(s, axis=-1, keepdims=True))
    alpha = jnp.exp2(m_prev - m_new)
    p = jnp.exp2(s - jnp.concatenate([m_new] * (s.shape[1] // LANES), axis=1))
    m_sc[h] = m_new
    return alpha, p.astype(BF16)

  def step(c, bias_refs):
    keys = len(bias_refs) * t
    start = pl.multiple_of(c * t, t)
    ones = jnp.ones((keys, LANES), BF16)
    for h in range(ATT_HEADS):
      alpha, pb = softmax_probs(h, scores(h, c, bias_refs))
      vaug = jnp.concatenate([v_ref[0, h, pl.ds(start, keys), :], ones], axis=1)
      acc_sc[h] = acc_sc[h] * jnp.concatenate([alpha, alpha], axis=1) + _dot(pb, vaug)

  n_far = jnp.maximum(qi - 1, 0)

  def far_pair(k, carry):
    step(2 * k, (None, None))
    return carry

  lax.fori_loop(0, n_far // 2, far_pair, 0)

  @pl.when(n_far % 2 == 1)
  def _():
    step(n_far - 1, (None,))

  @pl.when(qi >= 1)
  def _():
    step(qi - 1, (near_ref, diag_ref))

  @pl.when(qi == 0)
  def _():
    step(qi, (diag_ref,))

  lam = _diff_lambda(lam_ref)
  for h in range(ATT_HEADS):
    acc = acc_sc[h]
    o = acc[:, :ATT_V_DIM] / acc[:, ATT_V_DIM:]
    d = _diff_combine(o[:t], o[t:], lam, g_ref[...])
    o_ref[0, :, h * ATT_V_DIM:(h + 1) * ATT_V_DIM] = d.astype(o_ref.dtype)


def _prompt_attention(q_hm, kt_hm, v_hm, near, diag, lam4, subln_g):
  b, nh, s, e = q_hm.shape
  t = ATT_T
  nq = s // t
  return pl.pallas_call(
      _attn_kernel,
      grid=(b, nq),
      in_specs=[pl.BlockSpec((1, nh, t, e), lambda bi, qi: (bi, 0, qi, 0)),
                pl.BlockSpec((1, nh, nq, e, t), lambda bi, qi: (bi, 0, 0, 0, 0)),
                pl.BlockSpec((1, nh, s, e), lambda bi, qi: (bi, 0, 0, 0)),
                pl.BlockSpec((nh, t, t), lambda bi, qi: (0, 0, 0)),
                pl.BlockSpec((nh, t, t), lambda bi, qi: (0, 0, 0)),
                pl.BlockSpec((4, ATT_HEAD_DIM), lambda bi, qi: (0, 0)),
                pl.BlockSpec((1, e), lambda bi, qi: (0, 0))],
      out_specs=pl.BlockSpec((1, t, nh * e), lambda bi, qi: (bi, qi, 0)),
      out_shape=jax.ShapeDtypeStruct((b, s, nh * e), BF16),
      scratch_shapes=[pltpu.VMEM((nh, 2 * t, LANES), F32),
                      pltpu.VMEM((nh, 2 * t, 2 * LANES), F32)],
      compiler_params=pltpu.CompilerParams(
          dimension_semantics=("arbitrary", "arbitrary"), vmem_limit_bytes=VMEM_LIMIT),
      name="prompt_attention",
  )(q_hm, kt_hm, v_hm, near, diag, lam4, subln_g)


def _ssd_kernel(xbc_ref, z_ref, dt_ref, pre_ref, h0_ref, *refs, tv, nc, alternate):
  consts, (ssd_ref, conv_ref, h_ref, xp_sc) = refs[:8], refs[8:]
  hp = SUBLANES
  b = pl.program_id(0)
  c = pl.program_id(1)
  n_seq = xbc_ref.shape[0]

  @pl.when((b == 0) & (c == 0))
  def _():
    xp_sc[...] = jnp.zeros(xp_sc.shape, F32)

  @pl.when(c == 0)
  def _():
    for bb in range(n_seq):
      xp_sc[bb, hp - (CONV_W - 1):hp, :] = pre_ref[bb]
      h_ref[bb] = h0_ref[bb]

  stages = [_ssd_chunk(xbc_ref.at[bb], z_ref.at[bb], dt_ref.at[bb], *consts,
                       ssd_ref.at[bb], h_ref.at[bb], xp_sc.at[bb], tv=tv) for bb in range(n_seq)]
  for _ in (itertools.zip_longest(*stages) if alternate else itertools.chain(*stages)):
    pass

  @pl.when(c == nc - 1)
  def _():
    for bb in range(n_seq):
      conv_ref[bb] = xp_sc[bb, hp + tv - (CONV_W - 1):hp + tv, :]


def _ssd_chunk(xbc_ref, z_ref, dt_ref, cw_ref, cb_ref, dtb_ref, alog_ref, dsk_ref, ng_ref, e_ref,
               tri_ref, ssd_ref, h_ref, xp_sc, *, tv):
  L = SSD_CHUNK
  hp = SUBLANES

  def pad_rows(x):
    if tv == L:
      return x
    return jnp.concatenate([x, jnp.zeros((L - tv,) + x.shape[1:], x.dtype)], axis=0)

  xp_sc[hp:hp + tv, :] = xbc_ref[...]
  cw = cw_ref[...]
  y = cb_ref[...]
  cur = xp_sc[hp:hp + tv, :]
  hist = xp_sc[0:hp, :]
  head_row = lax.broadcasted_iota(jnp.int32, (hp, CONV_DIM), 0)
  for i in range(CONV_W):
    back = CONV_W - 1 - i
    if back == 0:
      tap = cur
    else:
      rolled = pltpu.roll(cur, back, 0)
      head = jnp.where(head_row < back, pltpu.roll(hist, back, 0), rolled[0:hp])
      tap = head if tv == hp else jnp.concatenate([head, rolled[hp:]], axis=0)
    y = y + tap * cw[i:i + 1]
  xa = _silu(y)
  xp_sc[0:hp, :] = xp_sc[tv:tv + hp, :]
  yield

  xs = xa[:, :SSD_WIDTH]
  gw = SSD_GROUPS * D_STATE
  bmat = pad_rows(xa[:, SSD_WIDTH:SSD_WIDTH + gw]).astype(BF16)
  cmat = pad_rows(xa[:, SSD_WIDTH + gw:]).astype(BF16)

  dt_raw = dt_ref[...]
  if dt_raw.shape[0] != tv:
    dt_raw = jnp.concatenate([dt_raw, jnp.zeros((L - dt_raw.shape[0], L), F32)], axis=0).T
  x = dt_raw + dtb_ref[...]
  dtv = pad_rows(jnp.maximum(x, 0.0) + jnp.log1p(jnp.exp(-jnp.abs(x))))
  a = dtv * (-jnp.exp(alog_ref[...]))

  tri = tri_ref[...]
  acs = sum(_dot(tri, part) for part in _split3(a))
  yield
  e = e_ref[...]
  aexp = sum(_dot(part, e) for part in _split3(acs))
  dtexp = _dot(dtv.astype(BF16), e)
  xd = xs * dtexp[:tv]
  a_last = acs[L - 1:L, :]
  xdd_t = pad_rows(xd * jnp.exp(aexp[L - 1:L, :] - aexp[:tv])).T
  xd = pad_rows(xd)
  acs_t = acs.T
  yield

  ri = lax.broadcasted_iota(jnp.int32, (L, L), 0)
  ci = lax.broadcasted_iota(jnp.int32, (L, L), 1)
  causal = ri >= ci
  lane = lax.broadcasted_iota(jnp.int32, (L, LANES), 1)
  heads_per_group = SSD_HEADS // SSD_GROUPS
  hprev = h_ref[...]
  hb = hprev.astype(BF16)

  ydiag, yoff, upd = [], [], []
  for g in range(SSD_GROUPS):
    cg = cmat[:, g * D_STATE:(g + 1) * D_STATE]
    bg = bmat[:, g * D_STATE:(g + 1) * D_STATE]
    scores = _dot_nt(cg, bg)
    for pair in range(heads_per_group // 2):
      h0 = g * heads_per_group + 2 * pair
      ms = []
      for h in (h0, h0 + 1):
        col = jnp.broadcast_to(acs[:, h:h + 1], (L, L))
        row = acs_t[h:h + 1, :]
        lm = jnp.exp(jnp.where(causal, col - row, NEG))
        ms.append((scores * lm).astype(BF16))
      xpair = xd[:, h0 * SSD_HEAD_DIM:(h0 + 2) * SSD_HEAD_DIM]
      xbd = jnp.concatenate([jnp.where(lane < SSD_HEAD_DIM, xpair, 0.0),
                             jnp.where(lane >= SSD_HEAD_DIM, xpair, 0.0)], axis=0)
      ydiag.append(_dot(jnp.concatenate(ms, axis=1), xbd.astype(BF16)))
      yield
    rows = slice(g * heads_per_group * SSD_HEAD_DIM, (g + 1) * heads_per_group * SSD_HEAD_DIM)
    yoff.append(_dot_nt(cg, hb[rows, :]))
    upd.append(_dot(xdd_t[rows, :].astype(BF16), bg))
    yield

  y = (jnp.concatenate(ydiag, axis=1)[:tv]
       + jnp.concatenate(yoff, axis=1)[:tv] * jnp.exp(aexp[:tv]) + dsk_ref[...] * xs)
  yield

  decayed = []
  chunk_decay = jnp.exp(a_last)
  for h in range(SSD_HEADS):
    dec = jnp.broadcast_to(chunk_decay[:, h:h + 1], (SSD_HEAD_DIM, D_STATE))
    decayed.append(hprev[h * SSD_HEAD_DIM:(h + 1) * SSD_HEAD_DIM, :] * dec)
  h_ref[...] = jnp.concatenate(decayed, axis=0) + jnp.concatenate(upd, axis=0)
  yield

  yv = y * _silu(z_ref[...])
  sq = yv * yv
  half = SSD_WIDTH // SSD_GROUPS
  r0 = lax.rsqrt(jnp.mean(sq[:, :half], axis=-1, keepdims=True) + EPS)
  r1 = lax.rsqrt(jnp.mean(sq[:, half:], axis=-1, keepdims=True) + EPS)
  out = jnp.concatenate([yv[:, :half] * r0, yv[:, half:] * r1], axis=1) * ng_ref[...]
  ssd_ref[...] = out.astype(ssd_ref.dtype)


def _ssd(xbc, z, dt, prefix, h0, conv_w, conv_b, dtb_p, alog_p, dsk_e, norm_g, tv, seqs_per_step):
  nb, tb, _ = xbc.shape
  nc = tb // tv
  ns = seqs_per_step
  assert nb % ns == 0
  e_np = np.zeros((LANES, SSD_WIDTH), np.float32)
  for h in range(SSD_HEADS):
    e_np[h, h * SSD_HEAD_DIM:(h + 1) * SSD_HEAD_DIM] = 1.0
  tri_np = np.tril(np.ones((SSD_CHUNK, SSD_CHUNK), np.float32))
  full = lambda shape: pl.BlockSpec(shape, lambda b, c: (0,) * len(shape))
  tok = lambda w: pl.BlockSpec((ns, tv, w), lambda b, c: (b, c, 0))
  per_b = lambda r, w: pl.BlockSpec((ns, r, w), lambda b, c: (b, 0, 0))
  if dt.shape[1] == tb:
    dt_spec = tok(LANES)
  else:
    assert tv == SSD_CHUNK and dt.shape == (nb, DT_ROWS, tb)
    dt_spec = pl.BlockSpec((ns, DT_ROWS, tv), lambda b, c: (b, 0, c))
  return pl.pallas_call(
      functools.partial(_ssd_kernel, tv=tv, nc=nc, alternate=tv < SSD_CHUNK),
      grid=(nb // ns, nc),
      in_specs=[tok(CONV_DIM), tok(SSD_WIDTH), dt_spec,
                per_b(CONV_W - 1, CONV_DIM), per_b(SSD_WIDTH, D_STATE),
                full((CONV_W, CONV_DIM)), full((1, CONV_DIM)), full((1, LANES)), full((1, LANES)),
                full((1, SSD_WIDTH)), full((1, SSD_WIDTH)),
                full((LANES, SSD_WIDTH)), full((SSD_CHUNK, SSD_CHUNK))],
      out_specs=[tok(SSD_WIDTH), per_b(CONV_W - 1, CONV_DIM), per_b(SSD_WIDTH, D_STATE)],
      out_shape=[jax.ShapeDtypeStruct((nb, tb, SSD_WIDTH), BF16),
                 jax.ShapeDtypeStruct((nb, CONV_W - 1, CONV_DIM), F32),
                 jax.ShapeDtypeStruct((nb, SSD_WIDTH, D_STATE), F32)],
      scratch_shapes=[pltpu.VMEM((ns, SSD_CHUNK + SUBLANES, CONV_DIM), F32)],
      compiler_params=pltpu.CompilerParams(
          dimension_semantics=("arbitrary", "arbitrary"), vmem_limit_bytes=VMEM_LIMIT),
      name="ssd_scan",
  )(xbc, z, dt, prefix, h0, conv_w, conv_b, dtb_p, alog_p, dsk_e, norm_g,
    jnp.asarray(e_np, BF16), jnp.asarray(tri_np, BF16))


def _ffn_pre(x_ref, att_ref, ssd_ref, ga1_ref, sc2_ref, sh2_ref, gf_ref, wo_ref):
  mix = _dot(att_ref[0], wo_ref[0:ATT_WIDTH, :]) + _dot(ssd_ref[0], wo_ref[ATT_WIDTH:, :])
  x1 = x_ref[0] + ga1_ref[0] * mix
  var = jnp.mean(x1 * x1, axis=-1, keepdims=True)
  h2 = x1 * lax.rsqrt(var + EPS) * gf_ref[...]
  return x1, (h2 * (1.0 + sc2_ref[0]) + sh2_ref[0]).astype(BF16)


def _ffn_piece(h2, wu_ref, wd_ref, lo, hi):
  u = jnp.maximum(_dot(h2, wu_ref[:, lo:hi]), 0.0)
  return _dot((u * u).astype(BF16), wd_ref[lo:hi, :])


def _ffn_post(x1, acc, ga2_ref, gl_ref):
  x2 = x1 + ga2_ref[0] * acc
  var2 = jnp.mean(x2 * x2, axis=-1, keepdims=True)
  return x2 * lax.rsqrt(var2 + EPS) * gl_ref[...]


def _ffn_kernel(x_ref, att_ref, ssd_ref, ga1_ref, sc2_ref, sh2_ref, ga2_ref, gf_ref, gl_ref,
                wo_ref, wu_ref, wd_ref, y_ref, x1_sc, h2_sc, acc_sc):
  f = pl.program_id(2)

  @pl.when(f == 0)
  def _():
    x1, h2 = _ffn_pre(x_ref, att_ref, ssd_ref, ga1_ref, sc2_ref, sh2_ref, gf_ref, wo_ref)
    x1_sc[...] = x1
    h2_sc[...] = h2
    acc_sc[...] = jnp.zeros(acc_sc.shape, F32)

  acc_sc[...] += _ffn_piece(h2_sc[...], wu_ref, wd_ref, 0, wu_ref.shape[1])

  @pl.when(f == pl.num_programs(2) - 1)
  def _():
    y_ref[0] = _ffn_post(x1_sc[...], acc_sc[...], ga2_ref, gl_ref)


def _out_ffn(x3, att, ssd, mods, g_ffn, g_final, w_out_b, w_up_b, w_down_b, tm):
  nb, tb, d = x3.shape
  nt = tb // tm
  fc = D_MODEL
  mspec = lambda col: _mod_spec(mods, tm, col, lambda b, i, f: b, lambda b, i, f: i)
  tok = lambda w: pl.BlockSpec((1, tm, w), lambda b, i, f: (b, i, 0))
  const = lambda shape: pl.BlockSpec(shape, lambda b, i, f: (0, 0))
  return pl.pallas_call(
      _ffn_kernel,
      grid=(nb, nt, D_FF // fc),
      in_specs=[tok(d), tok(ATT_WIDTH), tok(SSD_WIDTH),
                mspec(MOD_GA1), mspec(MOD_SC2), mspec(MOD_SH2), mspec(MOD_GA2),
                const((1, d)), const((1, d)), const((d, d)),
                pl.BlockSpec((d, fc), lambda b, i, f: (0, f)),
                pl.BlockSpec((fc, d), lambda b, i, f: (f, 0))],
      out_specs=tok(d),
      out_shape=jax.ShapeDtypeStruct((nb, tb, d), F32),
      scratch_shapes=[pltpu.VMEM((tm, d), F32), pltpu.VMEM((tm, d), BF16),
                      pltpu.VMEM((tm, d), F32)],
      compiler_params=pltpu.CompilerParams(
          dimension_semantics=("arbitrary", "arbitrary", "arbitrary"),
          vmem_limit_bytes=VMEM_LIMIT),
      name="out_ffn",
  )(x3, att, ssd, mods, mods, mods, mods, g_ffn.reshape(1, d), g_final.reshape(1, d),
    w_out_b, w_up_b, w_down_b)


def _ffn_sample_kernel(pt_ref, x_ref, att_ref, ssd_ref, ga1_ref, sc2_ref, sh2_ref, ga2_ref,
                       gf_ref, gl_ref, wo_ref, wu_ref, wd_ref,
                       qn_ref, kn_ref, vn_ref, blast_ref, bnew_ref, lam_ref, g_ref, ck_hbm, cv_hbm,
                       y_ref, o_ref,
                       kbuf, vbuf, sem, m_sc, l_sc, acc_sc, kpad, vpad, facc_sc,
                       *, n_steps, steps_per_seq):
  i = pl.program_id(0)
  seq = i // steps_per_seq
  part = i % steps_per_seq
  rps = ROUNDS_PER_STEP
  ahead = ROUND_SLOTS - 1
  last_part = part == steps_per_seq - 1

  def round_copies(r):
    if r < rps:
      sq, rnd = seq, part * rps + r
    else:
      nxt = i + 1
      sq, rnd = nxt // steps_per_seq, (nxt % steps_per_seq) * rps + r - rps
    sl = r % ROUND_SLOTS
    copies = []
    for p in range(PAGES_PER_GROUP):
      pg = pt_ref[sq, rnd * PAGES_PER_GROUP + p]
      copies.append((pltpu.make_async_copy(ck_hbm.at[pg], kbuf.at[sl, p], sem.at[0, sl]), 0))
      copies.append((pltpu.make_async_copy(cv_hbm.at[pg], vbuf.at[sl, p], sem.at[1, sl]), 1))
    return copies

  def start_round(r):
    for cp, prio in round_copies(r):
      cp.start(priority=prio)

  @pl.when(i == 0)
  def _():
    kpad[...] = jnp.zeros(kpad.shape, kpad.dtype)
    vpad[...] = jnp.zeros(vpad.shape, vpad.dtype)
    for r in range(ahead):
      start_round(r)

  @pl.when(part == 0)
  def _():
    m_sc[...] = jnp.full(m_sc.shape, NEG, F32)
    l_sc[...] = jnp.zeros(l_sc.shape, F32)
    acc_sc[...] = jnp.zeros(acc_sc.shape, F32)

  qn = qn_ref[0].astype(BF16)
  half_of_lane = lax.broadcasted_iota(jnp.int32, qn.shape, 1) // ATT_HEAD_DIM
  qb = jnp.concatenate([jnp.where(half_of_lane == hj, qn, jnp.zeros_like(qn))
                        for hj in range(2 * ATT_HEADS)], axis=0)
  rows_per_head = 2 * SUBLANES

  def softmax_probs(s):
    m_prev = m_sc[...]
    m_new = jnp.maximum(m_prev, jnp.max(s, axis=-1, keepdims=True))
    alpha = jnp.exp2(m_prev - m_new)
    p = jnp.exp2(s - m_new[:, 0:1])
    l_sc[...] = l_sc[...] * alpha + jnp.sum(p, axis=-1, keepdims=True)
    m_sc[...] = m_new
    return alpha, p.astype(BF16)

  def accumulate(alpha, pb, values_of_head):
    pv = [_dot(pb[h * rows_per_head:(h + 1) * rows_per_head, :], values_of_head(h))
          for h in range(ATT_HEADS)]
    acc_sc[...] = acc_sc[...] * alpha + jnp.concatenate(pv, axis=0)

  def round_scores(r):
    sl = r % ROUND_SLOTS
    kt = jnp.concatenate([kbuf[sl, p].astype(BF16) for p in range(PAGES_PER_GROUP)], axis=1)
    hr, hf = qb.shape[0] // 2, qb.shape[1] // 2
    s = jnp.concatenate([_dot(qb[:hr, :hf], kt[:hf]), _dot(qb[hr:, hf:], kt[hf:])], axis=0)
    if r == rps - 1:
      tail = s[:, -PAGE_SIZE:] + jnp.where(last_part, blast_ref[...], 0.0)
      s = jnp.concatenate([s[:, :-PAGE_SIZE], tail], axis=1)
    return s

  def round_values(r):
    sl = r % ROUND_SLOTS

    def values_of_head(h):
      v = [vbuf[sl, p, pl.ds(h, PAGE_SIZE, stride=ATT_HEADS), :] for p in range(PAGES_PER_GROUP)]
      return jnp.concatenate(v, axis=0).astype(BF16)

    return values_of_head

  x1, h2 = _ffn_pre(x_ref, att_ref, ssd_ref, ga1_ref, sc2_ref, sh2_ref, gf_ref, wo_ref)
  fc = D_FF // rps
  for r in range(rps):
    for cp, _ in round_copies(r):
      cp.wait()
    if r + ahead < rps:
      start_round(r + ahead)
    else:
      pl.when(i + 1 < n_steps)(functools.partial(start_round, r + ahead))
    s = round_scores(r)
    u = jnp.maximum(_dot(h2, wu_ref[:, r * fc:(r + 1) * fc]), 0.0)
    ub = (u * u).astype(BF16)
    alpha, pb = softmax_probs(s)
    accumulate(alpha, pb, round_values(r))
    piece = _dot(ub, wd_ref[r * fc:(r + 1) * fc, :])
    facc_sc[...] = piece if r == 0 else facc_sc[...] + piece
  y_ref[0] = _ffn_post(x1, facc_sc[...], ga2_ref, gl_ref)

  @pl.when(last_part)
  def _():
    t_new = kn_ref.shape[1]
    kpad[0:t_new, :] = kn_ref[0].astype(BF16)
    vpad[0:t_new, :] = vn_ref[0].astype(BF16)
    sn = _dot_nt(qb, kpad[...]) + bnew_ref[...]
    accumulate(*softmax_probs(sn), lambda h: vpad[:, h * ATT_V_DIM:(h + 1) * ATT_V_DIM])
    o = acc_sc[...] / l_sc[...]
    lam = _diff_lambda(lam_ref)
    for h in range(ATT_HEADS):
      o0 = o[(2 * h) * SUBLANES:(2 * h + 1) * SUBLANES, :]
      o1 = o[(2 * h + 1) * SUBLANES:(2 * h + 2) * SUBLANES, :]
      o_ref[0, :, h * ATT_V_DIM:(h + 1) * ATT_V_DIM] = _diff_combine(
          o0, o1, lam, g_ref[...]).astype(o_ref.dtype)


def _ffn_with_sample_attention(x_p, att_p, ssd_p, mods, g_ffn, g_final,
                               w_out_b, w_up_b, w_down_b,
                               page_table, q_new, k_new, v_new, blast, bnew, lam4, subln_g,
                               cache_k, cache_v):
  nb, tb, d = x_p.shape
  tm = FFN_TM
  tiles_per_b = tb // tm
  n_steps = nb * tiles_per_b
  n_seq, n_pages = page_table.shape
  t_new = k_new.shape[1]
  pages_per_step = PAGES_PER_GROUP * ROUNDS_PER_STEP
  steps_per_seq = n_pages // pages_per_step
  ring = ROUND_SLOTS
  q_rows = 2 * ATT_HEADS * t_new
  assert t_new == SUBLANES and n_pages % pages_per_step == 0
  assert n_seq * steps_per_seq == n_steps, "one batch of page rounds per FFN tile"
  assert ROUNDS_PER_STEP % ROUND_SLOTS == 0, "ring slots must be static per round"
  w = ATT_WIDTH
  assert cache_k.shape[1:] == (w, PAGE_SIZE) and cache_v.shape[1:] == (PAGE_SIZE * ATT_HEADS, ATT_V_DIM)
  tok = lambda width: pl.BlockSpec((1, tm, width),
                                   lambda i, pt: (i // tiles_per_b, i % tiles_per_b, 0))
  mspec = lambda col: _mod_spec(mods, tm, col, lambda i, pt: i // tiles_per_b,
                                lambda i, pt: i % tiles_per_b)
  const = lambda shape: pl.BlockSpec(shape, lambda i, pt: (0,) * len(shape),
                                     pipeline_mode=pl.Buffered(1))
  per_seq = lambda r: pl.BlockSpec((1, r, w), lambda i, pt: (i // steps_per_seq, 0, 0))
  grid_spec = pltpu.PrefetchScalarGridSpec(
      num_scalar_prefetch=1,
      grid=(n_steps,),
      in_specs=[tok(d), tok(ATT_WIDTH), tok(SSD_WIDTH),
                mspec(MOD_GA1), mspec(MOD_SC2), mspec(MOD_SH2), mspec(MOD_GA2),
                const((1, d)), const((1, d)),
                const((d, d)), const((d, D_FF)), const((D_FF, d)),
                per_seq(t_new), per_seq(t_new), per_seq(t_new),
                const((q_rows, LANES)), const((q_rows, LANES)),
                const((4, ATT_HEAD_DIM)), const((1, ATT_V_DIM)),
                pl.BlockSpec(memory_space=pl.ANY),
                pl.BlockSpec(memory_space=pl.ANY)],
      out_specs=[tok(d), per_seq(t_new)],
      scratch_shapes=[pltpu.VMEM((ring, PAGES_PER_GROUP) + cache_k.shape[1:], F32),
                      pltpu.VMEM((ring, PAGES_PER_GROUP) + cache_v.shape[1:], F32),
                      pltpu.SemaphoreType.DMA((2, ring)),
                      pltpu.VMEM((q_rows, LANES), F32),
                      pltpu.VMEM((q_rows, LANES), F32),
                      pltpu.VMEM((q_rows, ATT_V_DIM), F32),
                      pltpu.VMEM((PAGE_SIZE, w), BF16),
                      pltpu.VMEM((PAGE_SIZE, w), BF16),
                      pltpu.VMEM((tm, d), F32)])
  return pl.pallas_call(
      functools.partial(_ffn_sample_kernel, n_steps=n_steps, steps_per_seq=steps_per_seq),
      grid_spec=grid_spec,
      out_shape=[jax.ShapeDtypeStruct((nb, tb, d), F32),
                 jax.ShapeDtypeStruct((n_seq, t_new, w), BF16)],
      compiler_params=pltpu.CompilerParams(
          dimension_semantics=("arbitrary",), vmem_limit_bytes=VMEM_LIMIT),
      name="ffn_sample_attention",
  )(page_table, x_p, att_p, ssd_p, mods, mods, mods, mods,
    g_ffn.reshape(1, d), g_final.reshape(1, d), w_out_b, w_up_b, w_down_b,
    q_new, k_new, v_new, blast, bnew, lam4, subln_g, cache_k, cache_v)


def kernel(x_prompt, x_sample, cache_k, cache_v, state_conv, state_ssm, page_table, c_prompt,
           c_sample, rel_bias, w_ada, b_ada, g_mix, g_ffn, w_in, w_out, lam_q1, lam_k1, lam_q2,
           lam_k2, subln_g, conv_w, conv_b, dt_bias, a_log, d_skip, ssd_norm_g, w_up, w_down,
           g_final):
  assert w_ada.shape[0] == 1, "single-layer step"
  bp, sp, d = x_prompt.shape
  bs, ts, _ = x_sample.shape
  n_pool = cache_k.shape[1]

  w_in_t = w_in[0].T.astype(BF16)
  w_t = w_in_t[:IN_MAIN]
  w_dt_p = jnp.pad(w_in_t[IN_MAIN:], ((0, DT_ROWS - SSD_HEADS), (0, 0)))
  w_dt_s = jnp.pad(w_in_t[IN_MAIN:], ((0, LANES - SSD_HEADS), (0, 0)))
  w_out_b = w_out[0].astype(BF16)
  w_up_b = w_up[0].astype(BF16)
  w_down_b = w_down[0].astype(BF16)
  lam4 = jnp.concatenate([lam_q1, lam_k1, lam_q2, lam_k2], axis=0)
  subln = subln_g.reshape(1, ATT_V_DIM)
  dtb_p = jnp.pad(dt_bias, ((0, 0), (0, LANES - SSD_HEADS)))
  alog_p = jnp.pad(a_log, ((0, 0), (0, LANES - SSD_HEADS)))
  dsk_e = jnp.repeat(d_skip[0], SSD_HEAD_DIM).reshape(1, SSD_WIDTH)
  ssd_args = (conv_w[0], conv_b, dtb_p, alog_p, dsk_e, ssd_norm_g)

  mods = _modulation(jnp.concatenate([c_prompt, c_sample], axis=0), w_ada[0], b_ada[0])
  mods_p = mods[:bp].reshape(bp, 1, 6 * d)
  mods_s = jnp.repeat(mods[bp:], ts, axis=0).reshape(1, bs * ts, 6 * d)
  near, diag, slast, snew = _bias_tiles(rel_bias)

  q_hm, kt_hm, v_hm, kt_p, v_p, z_p, xbc_p, dt_p = _in_proj(
      x_prompt, mods_p, g_mix[0], w_t, w_dt_p, tm=PROJ_TM, prompt=True)
  att_p = _prompt_attention(q_hm, kt_hm, v_hm, near, diag, lam4, subln)
  ssd_p, conv_p, h_p = _ssd(
      xbc_p, z_p, dt_p, jnp.zeros((bp, CONV_W - 1, CONV_DIM), F32),
      jnp.zeros((bp, SSD_WIDTH, D_STATE), F32), *ssd_args, tv=SSD_CHUNK, seqs_per_step=SSD_SEQS_PER_STEP)

  n_tok = bs * ts
  xs3 = x_sample.reshape(1, n_tok, d)
  q_s, k_s, v_s, z_s, xbc_s, dt_s = _in_proj(
      xs3, mods_s, g_mix[0], w_t, w_dt_s, tm=n_tok, prompt=False)
  blast = jnp.repeat(slast, 2, axis=0).reshape(2 * ATT_HEADS * SUBLANES, LANES)
  bnew = jnp.repeat(snew, 2, axis=0).reshape(2 * ATT_HEADS * SUBLANES, LANES)
  per_seq = lambda a: a.reshape(bs, ts, ATT_WIDTH)
  y_p, att_s = _ffn_with_sample_attention(
      x_prompt, att_p, ssd_p, mods_p, g_ffn[0], g_final, w_out_b, w_up_b, w_down_b,
      page_table, per_seq(q_s), per_seq(k_s), per_seq(v_s), blast, bnew, lam4, subln,
      jnp.transpose(cache_k[0], (0, 2, 3, 4, 1)).reshape(n_pool, ATT_WIDTH, PAGE_SIZE),
      cache_v[0].reshape(n_pool, PAGE_SIZE * ATT_HEADS, ATT_V_DIM))
  ssd_s, conv_s, h_s = _ssd(
      xbc_s.reshape(bs, ts, CONV_DIM), z_s.reshape(bs, ts, SSD_WIDTH), dt_s.reshape(bs, ts, LANES),
      state_conv[0], state_ssm[0].reshape(bs, SSD_WIDTH, D_STATE), *ssd_args, tv=ts,
      seqs_per_step=SSD_SHORT_SEQS_PER_STEP)
  y_s = _out_ffn(xs3, att_s.reshape(1, n_tok, ATT_WIDTH), ssd_s.reshape(1, n_tok, SSD_WIDTH),
                 mods_s, g_ffn[0], g_final, w_out_b, w_up_b, w_down_b, tm=n_tok)

  hshape = (SSD_HEADS, SSD_HEAD_DIM, D_STATE)
  return (y_p, y_s.reshape(bs, ts, d),
          kt_p.reshape(1, bp, ATT_HEADS, 2, ATT_HEAD_DIM, sp).transpose(0, 1, 5, 2, 3, 4),
          v_p.reshape(1, bp, sp, ATT_HEADS, ATT_V_DIM),
          conv_p[None], h_p.reshape(1, bp, *hshape),
          k_s.reshape(1, bs, ts, ATT_HEADS, 2, ATT_HEAD_DIM),
          v_s.reshape(1, bs, ts, ATT_HEADS, ATT_V_DIM),
          conv_s[None], h_s.reshape(1, bs, *hshape))
```

```python
import functools
import itertools
import math

import numpy as np
import jax
import jax.numpy as jnp
from jax import lax
from jax.experimental import pallas as pl
from jax.experimental.pallas import tpu as pltpu

F32 = jnp.float32
BF16 = jnp.bfloat16

D_MODEL = 1024
PAGE_SIZE = 128
ATT_HEADS = 4
ATT_HEAD_DIM = 64
ATT_V_DIM = 2 * ATT_HEAD_DIM
ATT_WIDTH = ATT_HEADS * ATT_V_DIM
SSD_HEADS = 8
SSD_HEAD_DIM = 64
SSD_WIDTH = SSD_HEADS * SSD_HEAD_DIM
SSD_GROUPS = 2
D_STATE = 128
CONV_W = 4
CONV_DIM = SSD_WIDTH + 2 * SSD_GROUPS * D_STATE
SSD_CHUNK = 128
IN_WIDTH = 3 * ATT_WIDTH + SSD_WIDTH + CONV_DIM + SSD_HEADS
D_FF = 4 * D_MODEL
N_BUCKETS = 32
MAX_DISTANCE = 128
MAX_EXACT = N_BUCKETS // 2
EPS = 1e-6
LAM_INIT = 0.8 - 0.6 * math.exp(-0.3 * 0)
MOD_SH1, MOD_SC1, MOD_GA1, MOD_SH2, MOD_SC2, MOD_GA2 = range(6)

LANES = 128
SUBLANES = 8
IN_MAIN = IN_WIDTH - SSD_HEADS
DT_ROWS = 2 * SUBLANES
LOG2E = 1.4426950408889634
NEG = -1e30
VMEM_LIMIT = 56 * 1024 * 1024

PROJ_TM = 512
SSD_SEQS_PER_STEP = 4
SSD_SHORT_SEQS_PER_STEP = 8
ATT_T = 256
FFN_TM = 256
PAGES_PER_GROUP = 8
ROUNDS_PER_STEP = 8
ROUND_SLOTS = 4


def _bucket_lower_bounds():
  d = np.arange(0, 4 * MAX_DISTANCE)
  nf = np.maximum(d, 1).astype(np.float64)
  large = MAX_EXACT + (np.log(nf / MAX_EXACT) / math.log(MAX_DISTANCE / MAX_EXACT)
                       * (N_BUCKETS - MAX_EXACT)).astype(np.int64)
  large = np.minimum(large, N_BUCKETS - 1)
  bucket = np.where(d < MAX_EXACT, d, large)
  return [int(np.argmax(bucket >= b)) for b in range(N_BUCKETS)]


_LOWER = _bucket_lower_bounds()
FAR_DIST = _LOWER[N_BUCKETS - 1]
assert FAR_DIST <= PAGE_SIZE and FAR_DIST <= ATT_T


def _silu(x):
  h = 0.5 * x
  return h * jnp.tanh(h) + h


def _dot(a, b):
  return jnp.dot(a, b, preferred_element_type=F32)


def _dot_nt(a, b):
  return lax.dot_general(a, b, (((1,), (1,)), ((), ())), preferred_element_type=F32)


def _split3(x):
  hi = x.astype(BF16)
  r1 = x - hi.astype(F32)
  mid = r1.astype(BF16)
  lo = (r1 - mid.astype(F32)).astype(BF16)
  return hi, mid, lo


def _mod_kernel(c_ref, w_ref, b_ref, o_ref):
  s = _silu(c_ref[...]).astype(BF16)
  o_ref[...] = _dot(s, w_ref[...].astype(BF16)) + b_ref[...]


def _modulation(c_all, w_ada, b_ada):
  n = c_all.shape[0]
  tn = D_MODEL
  return pl.pallas_call(
      _mod_kernel,
      grid=(6 * D_MODEL // tn,),
      in_specs=[pl.BlockSpec((n, D_MODEL), lambda j: (0, 0)),
                pl.BlockSpec((D_MODEL, tn), lambda j: (0, j)),
                pl.BlockSpec((1, tn), lambda j: (0, j))],
      out_specs=pl.BlockSpec((n, tn), lambda j: (0, j)),
      out_shape=jax.ShapeDtypeStruct((n, 6 * D_MODEL), F32),
      compiler_params=pltpu.CompilerParams(vmem_limit_bytes=VMEM_LIMIT),
      name="modulation",
  )(c_all, w_ada, b_ada.reshape(1, -1))


def _bias_kernel(tab_ref, near_ref, diag_ref, slast_ref, snew_ref):
  def bias_of(dist, h):
    val = jnp.full(dist.shape, tab_ref[0, h], F32)
    for b in range(1, N_BUCKETS):
      val = jnp.where(dist >= _LOWER[b], tab_ref[b, h], val)
    return (val - tab_ref[N_BUCKETS - 1, h]) * LOG2E

  t = ATT_T
  r = lax.broadcasted_iota(jnp.int32, (t, t), 0)
  c = lax.broadcasted_iota(jnp.int32, (t, t), 1)
  r8 = lax.broadcasted_iota(jnp.int32, (SUBLANES, LANES), 0)
  c8 = lax.broadcasted_iota(jnp.int32, (SUBLANES, LANES), 1)
  for h in range(ATT_HEADS):
    near_ref[h] = bias_of(t + r - c, h)
    d = r - c
    diag_ref[h] = jnp.where(d >= 0, bias_of(d, h), NEG)
    slast_ref[h] = bias_of(PAGE_SIZE + r8 - c8, h)
    dn = r8 - c8
    snew_ref[h] = jnp.where(dn >= 0, bias_of(dn, h), NEG)


def _bias_tiles(rel_bias):
  t = ATT_T
  return pl.pallas_call(
      _bias_kernel,
      in_specs=[pl.BlockSpec(memory_space=pltpu.SMEM)],
      out_shape=(jax.ShapeDtypeStruct((ATT_HEADS, t, t), F32),
                 jax.ShapeDtypeStruct((ATT_HEADS, t, t), F32),
                 jax.ShapeDtypeStruct((ATT_HEADS, SUBLANES, LANES), F32),
                 jax.ShapeDtypeStruct((ATT_HEADS, SUBLANES, LANES), F32)),
      compiler_params=pltpu.CompilerParams(vmem_limit_bytes=VMEM_LIMIT),
      name="bias_tiles",
  )(rel_bias)


def _inproj_kernel(x_ref, sc_ref, sh_ref, g_ref, wt_ref, wdt_ref, *out_refs, prompt):
  x = x_ref[0]
  var = jnp.mean(x * x, axis=-1, keepdims=True)
  h = x * lax.rsqrt(var + EPS) * g_ref[...]
  h = h * (1.0 + sc_ref[0]) + sh_ref[0]
  hb = h.astype(BF16)
  aw = ATT_WIDTH
  proj = lambda lo, hi: _dot_nt(hb, wt_ref[lo:hi, :])
  q = proj(0, aw) * (ATT_HEAD_DIM ** -0.5 * LOG2E)
  v = proj(2 * aw, 3 * aw)
  if prompt:
    q_ref, ktb_ref, vb_ref, kt_ref, v_ref, z_ref, xbc_ref, dt_ref = out_refs
    kt = _dot_nt(wt_ref[aw:2 * aw, :], hb)
    kt_ref[0] = kt
    t = ATT_T
    for hd in range(ATT_HEADS):
      sl = slice(hd * ATT_V_DIM, (hd + 1) * ATT_V_DIM)
      vb_ref[0, hd] = v[:, sl].astype(BF16)
      v_ref[0, pl.ds(hd, v.shape[0], stride=ATT_HEADS), :] = v[:, sl]
      q_ref[0, hd] = q[:, sl].astype(BF16)
      for cc in range(kt.shape[1] // t):
        ktb_ref[0, hd, cc] = kt[sl, cc * t:(cc + 1) * t].astype(BF16)
  else:
    q_ref, k_ref, v_ref, z_ref, xbc_ref, dt_ref = out_refs
    q_ref[0] = q
    k_ref[0] = proj(aw, 2 * aw)
    v_ref[0] = v
  o = 3 * aw
  z_ref[0] = proj(o, o + SSD_WIDTH)
  o += SSD_WIDTH
  xbc_ref[0] = proj(o, o + CONV_DIM)
  dt_ref[0] = _dot_nt(wdt_ref[...], hb) if prompt else _dot_nt(hb, wdt_ref[...])


def _mod_spec(mods, tm, col, b_of, i_of):
  d = mods.shape[2] // 6
  if mods.shape[1] != 1:
    return pl.BlockSpec((1, tm, d), lambda *g: (b_of(*g), i_of(*g), col))
  return pl.BlockSpec((1, 1, d), lambda *g: (b_of(*g), 0, col))


def _in_proj(x3, mods, g_mix, w_t, w_dt, tm, prompt):
  nb, tb, d = x3.shape
  nt = tb // tm
  mspec = lambda col: _mod_spec(mods, tm, col, lambda b, i: b, lambda b, i: i)
  hm = pl.BlockSpec((1, ATT_HEADS, tm, ATT_V_DIM), lambda b, i: (b, 0, i, 0))
  tok = lambda w: pl.BlockSpec((1, tm, w), lambda b, i: (b, i, 0))
  hm_shape = jax.ShapeDtypeStruct((nb, ATT_HEADS, tb, ATT_V_DIM), BF16)
  tok_shape = lambda w: jax.ShapeDtypeStruct((nb, tb, w), F32)
  if prompt:
    dt_spec = pl.BlockSpec((1, DT_ROWS, tm), lambda b, i: (b, 0, i))
    dt_shape = jax.ShapeDtypeStruct((nb, DT_ROWS, tb), F32)
  else:
    dt_spec, dt_shape = tok(LANES), tok_shape(LANES)
  tail_specs = [tok(SSD_WIDTH), tok(CONV_DIM), dt_spec]
  tail_shapes = [tok_shape(SSD_WIDTH), tok_shape(CONV_DIM), dt_shape]
  if prompt:
    t = ATT_T
    fm = pl.BlockSpec((1, ATT_HEADS, tm // t, ATT_V_DIM, t), lambda b, i: (b, 0, i, 0, 0))
    fm_shape = jax.ShapeDtypeStruct((nb, ATT_HEADS, tb // t, ATT_V_DIM, t), BF16)
    out_specs = [hm, fm, hm,
                 pl.BlockSpec((1, ATT_WIDTH, tm), lambda b, i: (b, 0, i)),
                 pl.BlockSpec((1, tm * ATT_HEADS, ATT_V_DIM), lambda b, i: (b, i, 0))] + tail_specs
    out_shape = [hm_shape, fm_shape, hm_shape,
                 jax.ShapeDtypeStruct((nb, ATT_WIDTH, tb), F32),
                 jax.ShapeDtypeStruct((nb, tb * ATT_HEADS, ATT_V_DIM), F32)] + tail_shapes
  else:
    out_specs = [tok(ATT_WIDTH)] * 3 + tail_specs
    out_shape = [tok_shape(ATT_WIDTH)] * 3 + tail_shapes
  return pl.pallas_call(
      functools.partial(_inproj_kernel, prompt=prompt),
      grid=(nb, nt),
      in_specs=[tok(d), mspec(MOD_SC1), mspec(MOD_SH1),
                pl.BlockSpec((1, d), lambda b, i: (0, 0)),
                pl.BlockSpec(w_t.shape, lambda b, i: (0, 0)),
                pl.BlockSpec(w_dt.shape, lambda b, i: (0, 0))],
      out_specs=out_specs,
      out_shape=out_shape,
      compiler_params=pltpu.CompilerParams(
          dimension_semantics=("arbitrary", "arbitrary"), vmem_limit_bytes=VMEM_LIMIT),
      name="in_proj",
  )(x3, mods, mods, g_mix.reshape(1, d), w_t, w_dt)


def _diff_lambda(lam_ref):
  lv = lam_ref[...]
  s1 = jnp.sum(lv[0:1] * lv[1:2], axis=-1, keepdims=True)
  s2 = jnp.sum(lv[2:3] * lv[3:4], axis=-1, keepdims=True)
  return jnp.exp(s1) - jnp.exp(s2) + LAM_INIT


def _diff_combine(o0, o1, lam, g):
  d = o0 - lam * o1
  d = d * lax.rsqrt(jnp.mean(d * d, axis=-1, keepdims=True) + EPS) * g
  return d * (1.0 - LAM_INIT)


def _attn_kernel(q_ref, k_ref, v_ref, near_ref, diag_ref, lam_ref, g_ref, *refs):
  weights_f32, weights_bf16 = refs[0:3], refs[4:7]
  o_ref, (m_sc, acc_sc) = refs[3], refs[7:]
  for src, dst in zip(weights_f32, weights_bf16):
    dst[...] = src[...].astype(dst.dtype)
  t = ATT_T
  qi = pl.program_id(1)
  lane = lax.broadcasted_iota(jnp.int32, (t, ATT_V_DIM), 1)
  m_sc[...] = jnp.full(m_sc.shape, NEG, F32)
  acc_sc[...] = jnp.zeros(acc_sc.shape, F32)

  def scores(h, c, bias_refs):
    q = q_ref[0, h]
    zero = jnp.zeros_like(q)
    q2 = jnp.concatenate([jnp.where(lane < ATT_HEAD_DIM, q, zero),
                          jnp.where(lane >= ATT_HEAD_DIM, q, zero)], axis=0)
    kc = jnp.concatenate([k_ref[0, h, c + j] for j in range(len(bias_refs))], axis=1)
    s = _dot(q2, kc)
    if any(b is not None for b in bias_refs):
      bias = jnp.concatenate([jnp.zeros((t, t), F32) if b is None else b[h] for b in bias_refs],
                             axis=1)
      s = s + jnp.concatenate([bias, bias], axis=0)
    return s

  def softmax_probs(h, s):
    m_prev = m_sc[h]
    m_new = jnp.maximum(m_prev, jnp.max(s, axis=-1, keepdims=True))
    alpha = jnp.exp2(m_prev - m_new)
    p = jnp.exp2(s - jnp.concatenate([m_new] * (s.shape[1] // LANES), axis=1))
    m_sc[h] = m_new
    return alpha, p.astype(BF16)

  def step(c, bias_refs):
    keys = len(bias_refs) * t
    start = pl.multiple_of(c * t, t)
    ones = jnp.ones((keys, LANES), BF16)
    for h in range(ATT_HEADS):
      alpha, pb = softmax_probs(h, scores(h, c, bias_refs))
      vaug = jnp.concatenate([v_ref[0, h, pl.ds(start, keys), :], ones], axis=1)
      acc_sc[h] = acc_sc[h] * jnp.concatenate([alpha, alpha], axis=1) + _dot(pb, vaug)

  n_far = jnp.maximum(qi - 1, 0)

  def far_pair(k, carry):
    step(2 * k, (None, None))
    return carry

  lax.fori_loop(0, n_far // 2, far_pair, 0)

  @pl.when(n_far % 2 == 1)
  def _():
    step(n_far - 1, (None,))

  @pl.when(qi >= 1)
  def _():
    step(qi - 1, (near_ref, diag_ref))

  @pl.when(qi == 0)
  def _():
    step(qi, (diag_ref,))

  lam = _diff_lambda(lam_ref)
  for h in range(ATT_HEADS):
    acc = acc_sc[h]
    o = acc[:, :ATT_V_DIM] / acc[:, ATT_V_DIM:]
    d = _diff_combine(o[:t], o[t:], lam, g_ref[...])
    o_ref[0, :, h * ATT_V_DIM:(h + 1) * ATT_V_DIM] = d.astype(o_ref.dtype)


def _prompt_attention(q_hm, kt_hm, v_hm, near, diag, lam4, subln_g, weights):
  b, nh, s, e = q_hm.shape
  t = ATT_T
  nq = s // t
  n_steps = b * nq
  slab = lambda w: pl.BlockSpec((w.shape[0] // n_steps, w.shape[1]),
                                lambda bi, qi: (bi * nq + qi, 0))
  assert all(w.shape[0] % (n_steps * 2 * SUBLANES) == 0 for w in weights)
  return pl.pallas_call(
      _attn_kernel,
      grid=(b, nq),
      in_specs=[pl.BlockSpec((1, nh, t, e), lambda bi, qi: (bi, 0, qi, 0)),
                pl.BlockSpec((1, nh, nq, e, t), lambda bi, qi: (bi, 0, 0, 0, 0)),
                pl.BlockSpec((1, nh, s, e), lambda bi, qi: (bi, 0, 0, 0)),
                pl.BlockSpec((nh, t, t), lambda bi, qi: (0, 0, 0)),
                pl.BlockSpec((nh, t, t), lambda bi, qi: (0, 0, 0)),
                pl.BlockSpec((4, ATT_HEAD_DIM), lambda bi, qi: (0, 0)),
                pl.BlockSpec((1, e), lambda bi, qi: (0, 0))] + [slab(w) for w in weights],
      out_specs=[pl.BlockSpec((1, t, nh * e), lambda bi, qi: (bi, qi, 0))]
      + [slab(w) for w in weights],
      out_shape=[jax.ShapeDtypeStruct((b, s, nh * e), BF16)]
      + [jax.ShapeDtypeStruct(w.shape, BF16) for w in weights],
      scratch_shapes=[pltpu.VMEM((nh, 2 * t, LANES), F32),
                      pltpu.VMEM((nh, 2 * t, 2 * LANES), F32)],
      compiler_params=pltpu.CompilerParams(
          dimension_semantics=("arbitrary", "arbitrary"), vmem_limit_bytes=VMEM_LIMIT),
      name="prompt_attention",
  )(q_hm, kt_hm, v_hm, near, diag, lam4, subln_g, *weights)


def _ssd_kernel(xbc_ref, z_ref, dt_ref, pre_ref, h0_ref, *refs, tv, nc, alternate):
  consts, (ssd_ref, conv_ref, h_ref, xp_sc) = refs[:8], refs[8:]
  hp = SUBLANES
  b = pl.program_id(0)
  c = pl.program_id(1)
  n_seq = xbc_ref.shape[0]

  @pl.when((b == 0) & (c == 0))
  def _():
    xp_sc[...] = jnp.zeros(xp_sc.shape, F32)

  @pl.when(c == 0)
  def _():
    for bb in range(n_seq):
      xp_sc[bb, hp - (CONV_W - 1):hp, :] = pre_ref[bb]
      h_ref[bb] = h0_ref[bb]

  stages = [_ssd_chunk(xbc_ref.at[bb], z_ref.at[bb], dt_ref.at[bb], *consts,
                       ssd_ref.at[bb], h_ref.at[bb], xp_sc.at[bb], tv=tv) for bb in range(n_seq)]
  for _ in (itertools.zip_longest(*stages) if alternate else itertools.chain(*stages)):
    pass

  @pl.when(c == nc - 1)
  def _():
    for bb in range(n_seq):
      conv_ref[bb] = xp_sc[bb, hp + tv - (CONV_W - 1):hp + tv, :]


def _ssd_chunk(xbc_ref, z_ref, dt_ref, cw_ref, cb_ref, dtb_ref, alog_ref, dsk_ref, ng_ref, e_ref,
               tri_ref, ssd_ref, h_ref, xp_sc, *, tv):
  L = SSD_CHUNK
  hp = SUBLANES

  def pad_rows(x):
    if tv == L:
      return x
    return jnp.concatenate([x, jnp.zeros((L - tv,) + x.shape[1:], x.dtype)], axis=0)

  xp_sc[hp:hp + tv, :] = xbc_ref[...]
  cw = cw_ref[...]
  y = cb_ref[...]
  cur = xp_sc[hp:hp + tv, :]
  hist = xp_sc[0:hp, :]
  head_row = lax.broadcasted_iota(jnp.int32, (hp, CONV_DIM), 0)
  for i in range(CONV_W):
    back = CONV_W - 1 - i
    if back == 0:
      tap = cur
    else:
      rolled = pltpu.roll(cur, back, 0)
      head = jnp.where(head_row < back, pltpu.roll(hist, back, 0), rolled[0:hp])
      tap = head if tv == hp else jnp.concatenate([head, rolled[hp:]], axis=0)
    y = y + tap * cw[i:i + 1]
  xa = _silu(y)
  xp_sc[0:hp, :] = xp_sc[tv:tv + hp, :]
  yield

  xs = xa[:, :SSD_WIDTH]
  gw = SSD_GROUPS * D_STATE
  bmat = pad_rows(xa[:, SSD_WIDTH:SSD_WIDTH + gw]).astype(BF16)
  cmat = pad_rows(xa[:, SSD_WIDTH + gw:]).astype(BF16)

  dt_raw = dt_ref[...]
  if dt_raw.shape[0] != tv:
    dt_raw = jnp.concatenate([dt_raw, jnp.zeros((L - dt_raw.shape[0], L), F32)], axis=0).T
  x = dt_raw + dtb_ref[...]
  dtv = pad_rows(jnp.maximum(x, 0.0) + jnp.log1p(jnp.exp(-jnp.abs(x))))
  a = dtv * (-jnp.exp(alog_ref[...]))

  tri = tri_ref[...]
  acs = sum(_dot(tri, part) for part in _split3(a))
  yield
  e = e_ref[...]
  aexp = sum(_dot(part, e) for part in _split3(acs))
  dtexp = _dot(dtv.astype(BF16), e)
  xd = xs * dtexp[:tv]
  a_last = acs[L - 1:L, :]
  xdd_t = pad_rows(xd * jnp.exp(aexp[L - 1:L, :] - aexp[:tv])).T
  xd = pad_rows(xd)
  acs_t = acs.T
  yield

  ri = lax.broadcasted_iota(jnp.int32, (L, L), 0)
  ci = lax.broadcasted_iota(jnp.int32, (L, L), 1)
  causal = ri >= ci
  lane = lax.broadcasted_iota(jnp.int32, (L, LANES), 1)
  heads_per_group = SSD_HEADS // SSD_GROUPS
  hprev = h_ref[...]
  hb = hprev.astype(BF16)

  ydiag, yoff, upd = [], [], []
  for g in range(SSD_GROUPS):
    cg = cmat[:, g * D_STATE:(g + 1) * D_STATE]
    bg = bmat[:, g * D_STATE:(g + 1) * D_STATE]
    scores = _dot_nt(cg, bg)
    for pair in range(heads_per_group // 2):
      h0 = g * heads_per_group + 2 * pair
      ms = []
      for h in (h0, h0 + 1):
        col = jnp.broadcast_to(acs[:, h:h + 1], (L, L))
        row = acs_t[h:h + 1, :]
        lm = jnp.exp(jnp.where(causal, col - row, NEG))
        ms.append((scores * lm).astype(BF16))
      xpair = xd[:, h0 * SSD_HEAD_DIM:(h0 + 2) * SSD_HEAD_DIM]
      xbd = jnp.concatenate([jnp.where(lane < SSD_HEAD_DIM, xpair, 0.0),
                             jnp.where(lane >= SSD_HEAD_DIM, xpair, 0.0)], axis=0)
      ydiag.append(_dot(jnp.concatenate(ms, axis=1), xbd.astype(BF16)))
      yield
    rows = slice(g * heads_per_group * SSD_HEAD_DIM, (g + 1) * heads_per_group * SSD_HEAD_DIM)
    yoff.append(_dot_nt(cg, hb[rows, :]))
    upd.append(_dot(xdd_t[rows, :].astype(BF16), bg))
    yield

  y = (jnp.concatenate(ydiag, axis=1)[:tv]
       + jnp.concatenate(yoff, axis=1)[:tv] * jnp.exp(aexp[:tv]) + dsk_ref[...] * xs)
  yield

  decayed = []
  chunk_decay = jnp.exp(a_last)
  for h in range(SSD_HEADS):
    dec = jnp.broadcast_to(chunk_decay[:, h:h + 1], (SSD_HEAD_DIM, D_STATE))
    decayed.append(hprev[h * SSD_HEAD_DIM:(h + 1) * SSD_HEAD_DIM, :] * dec)
  h_ref[...] = jnp.concatenate(decayed, axis=0) + jnp.concatenate(upd, axis=0)
  yield

  yv = y * _silu(z_ref[...])
  sq = yv * yv
  half = SSD_WIDTH // SSD_GROUPS
  r0 = lax.rsqrt(jnp.mean(sq[:, :half], axis=-1, keepdims=True) + EPS)
  r1 = lax.rsqrt(jnp.mean(sq[:, half:], axis=-1, keepdims=True) + EPS)
  out = jnp.concatenate([yv[:, :half] * r0, yv[:, half:] * r1], axis=1) * ng_ref[...]
  ssd_ref[...] = out.astype(ssd_ref.dtype)


def _ssd(xbc, z, dt, prefix, h0, conv_w, conv_b, dtb_p, alog_p, dsk_e, norm_g, tv, seqs_per_step):
  nb, tb, _ = xbc.shape
  nc = tb // tv
  ns = seqs_per_step
  assert nb % ns == 0
  e_np = np.zeros((LANES, SSD_WIDTH), np.float32)
  for h in range(SSD_HEADS):
    e_np[h, h * SSD_HEAD_DIM:(h + 1) * SSD_HEAD_DIM] = 1.0
  tri_np = np.tril(np.ones((SSD_CHUNK, SSD_CHUNK), np.float32))
  full = lambda shape: pl.BlockSpec(shape, lambda b, c: (0,) * len(shape))
  tok = lambda w: pl.BlockSpec((ns, tv, w), lambda b, c: (b, c, 0))
  per_b = lambda r, w: pl.BlockSpec((ns, r, w), lambda b, c: (b, 0, 0))
  if dt.shape[1] == tb:
    dt_spec = tok(LANES)
  else:
    assert tv == SSD_CHUNK and dt.shape == (nb, DT_ROWS, tb)
    dt_spec = pl.BlockSpec((ns, DT_ROWS, tv), lambda b, c: (b, 0, c))
  return pl.pallas_call(
      functools.partial(_ssd_kernel, tv=tv, nc=nc, alternate=tv < SSD_CHUNK),
      grid=(nb // ns, nc),
      in_specs=[tok(CONV_DIM), tok(SSD_WIDTH), dt_spec,
                per_b(CONV_W - 1, CONV_DIM), per_b(SSD_WIDTH, D_STATE),
                full((CONV_W, CONV_DIM)), full((1, CONV_DIM)), full((1, LANES)), full((1, LANES)),
                full((1, SSD_WIDTH)), full((1, SSD_WIDTH)),
                full((LANES, SSD_WIDTH)), full((SSD_CHUNK, SSD_CHUNK))],
      out_specs=[tok(SSD_WIDTH), per_b(CONV_W - 1, CONV_DIM), per_b(SSD_WIDTH, D_STATE)],
      out_shape=[jax.ShapeDtypeStruct((nb, tb, SSD_WIDTH), BF16),
                 jax.ShapeDtypeStruct((nb, CONV_W - 1, CONV_DIM), F32),
                 jax.ShapeDtypeStruct((nb, SSD_WIDTH, D_STATE), F32)],
      scratch_shapes=[pltpu.VMEM((ns, SSD_CHUNK + SUBLANES, CONV_DIM), F32)],
      compiler_params=pltpu.CompilerParams(
          dimension_semantics=("arbitrary", "arbitrary"), vmem_limit_bytes=VMEM_LIMIT),
      name="ssd_scan",
  )(xbc, z, dt, prefix, h0, conv_w, conv_b, dtb_p, alog_p, dsk_e, norm_g,
    jnp.asarray(e_np, BF16), jnp.asarray(tri_np, BF16))


def _ffn_pre(x_ref, att_ref, ssd_ref, ga1_ref, sc2_ref, sh2_ref, gf_ref, wo_ref):
  mix = _dot(att_ref[0], wo_ref[0:ATT_WIDTH, :]) + _dot(ssd_ref[0], wo_ref[ATT_WIDTH:, :])
  x1 = x_ref[0] + ga1_ref[0] * mix
  var = jnp.mean(x1 * x1, axis=-1, keepdims=True)
  h2 = x1 * lax.rsqrt(var + EPS) * gf_ref[...]
  return x1, (h2 * (1.0 + sc2_ref[0]) + sh2_ref[0]).astype(BF16)


def _ffn_piece(h2, wu_ref, wd_ref, lo, hi):
  u = jnp.maximum(_dot(h2, wu_ref[:, lo:hi]), 0.0)
  return _dot((u * u).astype(BF16), wd_ref[lo:hi, :])


def _ffn_post(x1, acc, ga2_ref, gl_ref):
  x2 = x1 + ga2_ref[0] * acc
  var2 = jnp.mean(x2 * x2, axis=-1, keepdims=True)
  return x2 * lax.rsqrt(var2 + EPS) * gl_ref[...]


def _ffn_kernel(x_ref, att_ref, ssd_ref, ga1_ref, sc2_ref, sh2_ref, ga2_ref, gf_ref, gl_ref,
                wo_ref, wu_ref, wd_ref, y_ref, x1_sc, h2_sc, acc_sc):
  f = pl.program_id(2)

  @pl.when(f == 0)
  def _():
    x1, h2 = _ffn_pre(x_ref, att_ref, ssd_ref, ga1_ref, sc2_ref, sh2_ref, gf_ref, wo_ref)
    x1_sc[...] = x1
    h2_sc[...] = h2
    acc_sc[...] = jnp.zeros(acc_sc.shape, F32)

  acc_sc[...] += _ffn_piece(h2_sc[...], wu_ref, wd_ref, 0, wu_ref.shape[1])

  @pl.when(f == pl.num_programs(2) - 1)
  def _():
    y_ref[0] = _ffn_post(x1_sc[...], acc_sc[...], ga2_ref, gl_ref)


def _out_ffn(x3, att, ssd, mods, g_ffn, g_final, w_out_b, w_up_b, w_down_b, tm):
  nb, tb, d = x3.shape
  nt = tb // tm
  fc = D_MODEL
  mspec = lambda col: _mod_spec(mods, tm, col, lambda b, i, f: b, lambda b, i, f: i)
  tok = lambda w: pl.BlockSpec((1, tm, w), lambda b, i, f: (b, i, 0))
  const = lambda shape: pl.BlockSpec(shape, lambda b, i, f: (0, 0))
  return pl.pallas_call(
      _ffn_kernel,
      grid=(nb, nt, D_FF // fc),
      in_specs=[tok(d), tok(ATT_WIDTH), tok(SSD_WIDTH),
                mspec(MOD_GA1), mspec(MOD_SC2), mspec(MOD_SH2), mspec(MOD_GA2),
                const((1, d)), const((1, d)), const((d, d)),
                pl.BlockSpec((d, fc), lambda b, i, f: (0, f)),
                pl.BlockSpec((fc, d), lambda b, i, f: (f, 0))],
      out_specs=tok(d),
      out_shape=jax.ShapeDtypeStruct((nb, tb, d), F32),
      scratch_shapes=[pltpu.VMEM((tm, d), F32), pltpu.VMEM((tm, d), BF16),
                      pltpu.VMEM((tm, d), F32)],
      compiler_params=pltpu.CompilerParams(
          dimension_semantics=("arbitrary", "arbitrary", "arbitrary"),
          vmem_limit_bytes=VMEM_LIMIT),
      name="out_ffn",
  )(x3, att, ssd, mods, mods, mods, mods, g_ffn.reshape(1, d), g_final.reshape(1, d),
    w_out_b, w_up_b, w_down_b)


def _ffn_sample_kernel(pt_ref, x_ref, att_ref, ssd_ref, ga1_ref, sc2_ref, sh2_ref, ga2_ref,
                       gf_ref, gl_ref, wo_ref, wu_ref, wd_ref,
                       qn_ref, kn_ref, vn_ref, blast_ref, bnew_ref, lam_ref, g_ref, ck_hbm, cv_hbm,
                       y_ref, o_ref,
                       kbuf, vbuf, sem, m_sc, l_sc, acc_sc, kpad, vpad, facc_sc,
                       *, n_steps, steps_per_seq):
  i = pl.program_id(0)
  seq = i // steps_per_seq
  part = i % steps_per_seq
  rps = ROUNDS_PER_STEP
  ahead = ROUND_SLOTS - 1
  last_part = part == steps_per_seq - 1

  def round_copies(r):
    if r < rps:
      sq, rnd = seq, part * rps + r
    else:
      nxt = i + 1
      sq, rnd = nxt // steps_per_seq, (nxt % steps_per_seq) * rps + r - rps
    sl = r % ROUND_SLOTS
    copies = []
    for p in range(PAGES_PER_GROUP):
      pg = pt_ref[sq, rnd * PAGES_PER_GROUP + p]
      copies.append((pltpu.make_async_copy(ck_hbm.at[pg], kbuf.at[sl, p], sem.at[0, sl]), 0))
      copies.append((pltpu.make_async_copy(cv_hbm.at[pg], vbuf.at[sl, p], sem.at[1, sl]), 1))
    return copies

  def start_round(r):
    for cp, prio in round_copies(r):
      cp.start(priority=prio)

  @pl.when(i == 0)
  def _():
    kpad[...] = jnp.zeros(kpad.shape, kpad.dtype)
    vpad[...] = jnp.zeros(vpad.shape, vpad.dtype)
    for r in range(ahead):
      start_round(r)

  @pl.when(part == 0)
  def _():
    m_sc[...] = jnp.full(m_sc.shape, NEG, F32)
    l_sc[...] = jnp.zeros(l_sc.shape, F32)
    acc_sc[...] = jnp.zeros(acc_sc.shape, F32)

  qn = qn_ref[0].astype(BF16)
  half_of_lane = lax.broadcasted_iota(jnp.int32, qn.shape, 1) // ATT_HEAD_DIM
  qb = jnp.concatenate([jnp.where(half_of_lane == hj, qn, jnp.zeros_like(qn))
                        for hj in range(2 * ATT_HEADS)], axis=0)
  rows_per_head = 2 * SUBLANES

  def softmax_probs(s):
    m_prev = m_sc[...]
    m_new = jnp.maximum(m_prev, jnp.max(s, axis=-1, keepdims=True))
    alpha = jnp.exp2(m_prev - m_new)
    p = jnp.exp2(s - m_new[:, 0:1])
    l_sc[...] = l_sc[...] * alpha + jnp.sum(p, axis=-1, keepdims=True)
    m_sc[...] = m_new
    return alpha, p.astype(BF16)

  def accumulate(alpha, pb, values_of_head):
    pv = [_dot(pb[h * rows_per_head:(h + 1) * rows_per_head, :], values_of_head(h))
          for h in range(ATT_HEADS)]
    acc_sc[...] = acc_sc[...] * alpha + jnp.concatenate(pv, axis=0)

  def round_scores(r):
    sl = r % ROUND_SLOTS
    kt = jnp.concatenate([kbuf[sl, p].astype(BF16) for p in range(PAGES_PER_GROUP)], axis=1)
    hr, hf = qb.shape[0] // 2, qb.shape[1] // 2
    s = jnp.concatenate([_dot(qb[:hr, :hf], kt[:hf]), _dot(qb[hr:, hf:], kt[hf:])], axis=0)
    if r == rps - 1:
      tail = s[:, -PAGE_SIZE:] + jnp.where(last_part, blast_ref[...], 0.0)
      s = jnp.concatenate([s[:, :-PAGE_SIZE], tail], axis=1)
    return s

  def round_values(r):
    sl = r % ROUND_SLOTS

    def values_of_head(h):
      v = [vbuf[sl, p, pl.ds(h, PAGE_SIZE, stride=ATT_HEADS), :] for p in range(PAGES_PER_GROUP)]
      return jnp.concatenate(v, axis=0).astype(BF16)

    return values_of_head

  x1, h2 = _ffn_pre(x_ref, att_ref, ssd_ref, ga1_ref, sc2_ref, sh2_ref, gf_ref, wo_ref)
  fc = D_FF // rps
  for r in range(rps):
    for cp, _ in round_copies(r):
      cp.wait()
    if r + ahead < rps:
      start_round(r + ahead)
    else:
      pl.when(i + 1 < n_steps)(functools.partial(start_round, r + ahead))
    s = round_scores(r)
    u = jnp.maximum(_dot(h2, wu_ref[:, r * fc:(r + 1) * fc]), 0.0)
    ub = (u * u).astype(BF16)
    alpha, pb = softmax_probs(s)
    accumulate(alpha, pb, round_values(r))
    piece = _dot(ub, wd_ref[r * fc:(r + 1) * fc, :])
    facc_sc[...] = piece if r == 0 else facc_sc[...] + piece
  y_ref[0] = _ffn_post(x1, facc_sc[...], ga2_ref, gl_ref)

  @pl.when(last_part)
  def _():
    t_new = kn_ref.shape[1]
    kpad[0:t_new, :] = kn_ref[0].astype(BF16)
    vpad[0:t_new, :] = vn_ref[0].astype(BF16)
    sn = _dot_nt(qb, kpad[...]) + bnew_ref[...]
    accumulate(*softmax_probs(sn), lambda h: vpad[:, h * ATT_V_DIM:(h + 1) * ATT_V_DIM])
    o = acc_sc[...] / l_sc[...]
    lam = _diff_lambda(lam_ref)
    for h in range(ATT_HEADS):
      o0 = o[(2 * h) * SUBLANES:(2 * h + 1) * SUBLANES, :]
      o1 = o[(2 * h + 1) * SUBLANES:(2 * h + 2) * SUBLANES, :]
      o_ref[0, :, h * ATT_V_DIM:(h + 1) * ATT_V_DIM] = _diff_combine(
          o0, o1, lam, g_ref[...]).astype(o_ref.dtype)


def _ffn_with_sample_attention(x_p, att_p, ssd_p, mods, g_ffn, g_final,
                               w_out_b, w_up_b, w_down_b,
                               page_table, q_new, k_new, v_new, blast, bnew, lam4, subln_g,
                               cache_k, cache_v):
  nb, tb, d = x_p.shape
  tm = FFN_TM
  tiles_per_b = tb // tm
  n_steps = nb * tiles_per_b
  n_seq, n_pages = page_table.shape
  t_new = k_new.shape[1]
  pages_per_step = PAGES_PER_GROUP * ROUNDS_PER_STEP
  steps_per_seq = n_pages // pages_per_step
  ring = ROUND_SLOTS
  q_rows = 2 * ATT_HEADS * t_new
  assert t_new == SUBLANES and n_pages % pages_per_step == 0
  assert n_seq * steps_per_seq == n_steps, "one batch of page rounds per FFN tile"
  assert ROUNDS_PER_STEP % ROUND_SLOTS == 0, "ring slots must be static per round"
  w = ATT_WIDTH
  assert cache_k.shape[1:] == (w, PAGE_SIZE) and cache_v.shape[1:] == (PAGE_SIZE * ATT_HEADS, ATT_V_DIM)
  tok = lambda width: pl.BlockSpec((1, tm, width),
                                   lambda i, pt: (i // tiles_per_b, i % tiles_per_b, 0))
  mspec = lambda col: _mod_spec(mods, tm, col, lambda i, pt: i // tiles_per_b,
                                lambda i, pt: i % tiles_per_b)
  const = lambda shape: pl.BlockSpec(shape, lambda i, pt: (0,) * len(shape),
                                     pipeline_mode=pl.Buffered(1))
  per_seq = lambda r: pl.BlockSpec((1, r, w), lambda i, pt: (i // steps_per_seq, 0, 0))
  grid_spec = pltpu.PrefetchScalarGridSpec(
      num_scalar_prefetch=1,
      grid=(n_steps,),
      in_specs=[tok(d), tok(ATT_WIDTH), tok(SSD_WIDTH),
                mspec(MOD_GA1), mspec(MOD_SC2), mspec(MOD_SH2), mspec(MOD_GA2),
                const((1, d)), const((1, d)),
                const((d, d)), const((d, D_FF)), const((D_FF, d)),
                per_seq(t_new), per_seq(t_new), per_seq(t_new),
                const((q_rows, LANES)), const((q_rows, LANES)),
                const((4, ATT_HEAD_DIM)), const((1, ATT_V_DIM)),
                pl.BlockSpec(memory_space=pl.ANY),
                pl.BlockSpec(memory_space=pl.ANY)],
      out_specs=[tok(d), per_seq(t_new)],
      scratch_shapes=[pltpu.VMEM((ring, PAGES_PER_GROUP) + cache_k.shape[1:], F32),
                      pltpu.VMEM((ring, PAGES_PER_GROUP) + cache_v.shape[1:], F32),
                      pltpu.SemaphoreType.DMA((2, ring)),
                      pltpu.VMEM((q_rows, LANES), F32),
                      pltpu.VMEM((q_rows, LANES), F32),
                      pltpu.VMEM((q_rows, ATT_V_DIM), F32),
                      pltpu.VMEM((PAGE_SIZE, w), BF16),
                      pltpu.VMEM((PAGE_SIZE, w), BF16),
                      pltpu.VMEM((tm, d), F32)])
  return pl.pallas_call(
      functools.partial(_ffn_sample_kernel, n_steps=n_steps, steps_per_seq=steps_per_seq),
      grid_spec=grid_spec,
      out_shape=[jax.ShapeDtypeStruct((nb, tb, d), F32),
                 jax.ShapeDtypeStruct((n_seq, t_new, w), BF16)],
      compiler_params=pltpu.CompilerParams(
          dimension_semantics=("arbitrary",), vmem_limit_bytes=VMEM_LIMIT),
      name="ffn_sample_attention",
  )(page_table, x_p, att_p, ssd_p, mods, mods, mods, mods,
    g_ffn.reshape(1, d), g_final.reshape(1, d), w_out_b, w_up_b, w_down_b,
    q_new, k_new, v_new, blast, bnew, lam4, subln_g, cache_k, cache_v)


def kernel(x_prompt, x_sample, cache_k, cache_v, state_conv, state_ssm, page_table, c_prompt,
           c_sample, rel_bias, w_ada, b_ada, g_mix, g_ffn, w_in, w_out, lam_q1, lam_k1, lam_q2,
           lam_k2, subln_g, conv_w, conv_b, dt_bias, a_log, d_skip, ssd_norm_g, w_up, w_down,
           g_final):
  assert w_ada.shape[0] == 1, "single-layer step"
  bp, sp, d = x_prompt.shape
  bs, ts, _ = x_sample.shape
  n_pool = cache_k.shape[1]

  w_in_t = w_in[0].T.astype(BF16)
  w_t = w_in_t[:IN_MAIN]
  w_dt_p = jnp.pad(w_in_t[IN_MAIN:], ((0, DT_ROWS - SSD_HEADS), (0, 0)))
  w_dt_s = jnp.pad(w_in_t[IN_MAIN:], ((0, LANES - SSD_HEADS), (0, 0)))
  lam4 =jnp.concatenate([lam_q1, lam_k1, lam_q2, lam_k2], axis=0)
  subln = subln_g.reshape(1, ATT_V_DIM)
  dtb_p = jnp.pad(dt_bias, ((0, 0), (0, LANES - SSD_HEADS)))
  alog_p = jnp.pad(a_log, ((0, 0), (0, LANES - SSD_HEADS)))
  dsk_e = jnp.repeat(d_skip[0], SSD_HEAD_DIM).reshape(1, SSD_WIDTH)
  ssd_args = (conv_w[0], conv_b, dtb_p, alog_p, dsk_e, ssd_norm_g)

  mods = _modulation(jnp.concatenate([c_prompt, c_sample], axis=0), w_ada[0], b_ada[0])
  mods_p = mods[:bp].reshape(bp, 1, 6 * d)
  mods_s = jnp.repeat(mods[bp:], ts, axis=0).reshape(1, bs * ts, 6 * d)
  near, diag, slast, snew = _bias_tiles(rel_bias)

  q_hm, kt_hm, v_hm, kt_p, v_p, z_p, xbc_p, dt_p = _in_proj(
      x_prompt, mods_p, g_mix[0], w_t, w_dt_p, tm=PROJ_TM, prompt=True)
  att_p, w_out_b, w_up_b, w_down_b = _prompt_attention(
      q_hm, kt_hm, v_hm, near, diag, lam4, subln, (w_out[0], w_up[0], w_down[0]))
  ssd_p, conv_p, h_p = _ssd(
      xbc_p, z_p, dt_p, jnp.zeros((bp, CONV_W - 1, CONV_DIM), F32),
      jnp.zeros((bp, SSD_WIDTH, D_STATE), F32), *ssd_args, tv=SSD_CHUNK, seqs_per_step=SSD_SEQS_PER_STEP)

  n_tok = bs * ts
  xs3 = x_sample.reshape(1, n_tok, d)
  q_s, k_s, v_s, z_s, xbc_s, dt_s = _in_proj(
      xs3, mods_s, g_mix[0], w_t, w_dt_s, tm=n_tok, prompt=False)
  blast = jnp.repeat(slast, 2, axis=0).reshape(2 * ATT_HEADS * SUBLANES, LANES)
  bnew = jnp.repeat(snew, 2, axis=0).reshape(2 * ATT_HEADS * SUBLANES, LANES)
  per_seq = lambda a: a.reshape(bs, ts, ATT_WIDTH)
  y_p, att_s = _ffn_with_sample_attention(
      x_prompt, att_p, ssd_p, mods_p, g_ffn[0], g_final, w_out_b, w_up_b, w_down_b,
      page_table, per_seq(q_s), per_seq(k_s), per_seq(v_s), blast, bnew, lam4, subln,
      jnp.transpose(cache_k[0], (0, 2, 3, 4, 1)).reshape(n_pool, ATT_WIDTH, PAGE_SIZE),
      cache_v[0].reshape(n_pool, PAGE_SIZE * ATT_HEADS, ATT_V_DIM))
  ssd_s, conv_s, h_s = _ssd(
      xbc_s.reshape(bs, ts, CONV_DIM), z_s.reshape(bs, ts, SSD_WIDTH), dt_s.reshape(bs, ts, LANES),
      state_conv[0], state_ssm[0].reshape(bs, SSD_WIDTH, D_STATE), *ssd_args, tv=ts,
      seqs_per_step=SSD_SHORT_SEQS_PER_STEP)
  y_s = _out_ffn(xs3, att_s.reshape(1, n_tok, ATT_WIDTH), ssd_s.reshape(1, n_tok, SSD_WIDTH),
                 mods_s, g_ffn[0], g_final, w_out_b, w_up_b, w_down_b, tm=n_tok)

  hshape = (SSD_HEADS, SSD_HEAD_DIM, D_STATE)
  return (y_p, y_s.reshape(bs, ts, d),
          kt_p.reshape(1, bp, ATT_HEADS, 2, ATT_HEAD_DIM, sp).transpose(0, 1, 5, 2, 3, 4),
          v_p.reshape(1, bp, sp, ATT_HEADS, ATT_V_DIM),
          conv_p[None], h_p.reshape(1, bp, *hshape),
          k_s.reshape(1, bs, ts, ATT_HEADS, 2, ATT_HEAD_DIM),
          v_s.reshape(1, bs, ts, ATT_HEADS, ATT_V_DIM),
          conv_s[None], h_s.reshape(1, bs, *hshape))
```

```python
import functools
import itertools
import math

import numpy as np
import jax
import jax.numpy as jnp
from jax import lax
from jax.experimental import pallas as pl
from jax.experimental.pallas import tpu as pltpu

F32 = jnp.float32
BF16 = jnp.bfloat16

D_MODEL = 1024
PAGE_SIZE = 128
ATT_HEADS = 4
ATT_HEAD_DIM = 64
ATT_V_DIM = 2 * ATT_HEAD_DIM
ATT_WIDTH = ATT_HEADS * ATT_V_DIM
SSD_HEADS = 8
SSD_HEAD_DIM = 64
SSD_WIDTH = SSD_HEADS * SSD_HEAD_DIM
SSD_GROUPS = 2
D_STATE = 128
CONV_W = 4
CONV_DIM = SSD_WIDTH + 2 * SSD_GROUPS * D_STATE
SSD_CHUNK = 128
IN_WIDTH = 3 * ATT_WIDTH + SSD_WIDTH + CONV_DIM + SSD_HEADS
D_FF = 4 * D_MODEL
N_BUCKETS = 32
MAX_DISTANCE = 128
MAX_EXACT = N_BUCKETS // 2
EPS = 1e-6
LAM_INIT = 0.8 - 0.6 * math.exp(-0.3 * 0)
MOD_SH1, MOD_SC1, MOD_GA1, MOD_SH2, MOD_SC2, MOD_GA2 = range(6)

LANES = 128
SUBLANES = 8
IN_MAIN = IN_WIDTH - SSD_HEADS
DT_ROWS = 2 * SUBLANES
LOG2E = 1.4426950408889634
NEG = -1e30
VMEM_LIMIT = 56 * 1024 * 1024

PROJ_TM = 512
SSD_SEQS_PER_STEP = 4
SSD_SHORT_SEQS_PER_STEP = 8
ATT_T = 256
FFN_TM = 256
PAGES_PER_GROUP = 8
ROUNDS_PER_STEP = 8
ROUND_SLOTS = 4


def _bucket_lower_bounds():
  d = np.arange(0, 4 * MAX_DISTANCE)
  nf = np.maximum(d, 1).astype(np.float64)
  large = MAX_EXACT + (np.log(nf / MAX_EXACT) / math.log(MAX_DISTANCE / MAX_EXACT)
                       * (N_BUCKETS - MAX_EXACT)).astype(np.int64)
  large = np.minimum(large, N_BUCKETS - 1)
  bucket = np.where(d < MAX_EXACT, d, large)
  return [int(np.argmax(bucket >= b)) for b in range(N_BUCKETS)]


_LOWER = _bucket_lower_bounds()
FAR_DIST = _LOWER[N_BUCKETS - 1]
assert FAR_DIST <= PAGE_SIZE and FAR_DIST <= ATT_T


def _silu(x):
  h = 0.5 * x
  return h * jnp.tanh(h) + h


def _dot(a, b):
  return jnp.dot(a, b, preferred_element_type=F32)


def _dot_nt(a, b):
  return lax.dot_general(a, b, (((1,), (1,)), ((), ())), preferred_element_type=F32)


def _split3(x):
  hi = x.astype(BF16)
  r1 = x - hi.astype(F32)
  mid = r1.astype(BF16)
  lo = (r1 - mid.astype(F32)).astype(BF16)
  return hi, mid, lo


def _mod_kernel(c_ref, w_ref, b_ref, o_ref):
  s = _silu(c_ref[...]).astype(BF16)
  o_ref[...] = _dot(s, w_ref[...].astype(BF16)) + b_ref[...]


def _modulation(c_all, w_ada, b_ada):
  n = c_all.shape[0]
  tn = D_MODEL
  return pl.pallas_call(
      _mod_kernel,
      grid=(6 * D_MODEL // tn,),
      in_specs=[pl.BlockSpec((n, D_MODEL), lambda j: (0, 0)),
                pl.BlockSpec((D_MODEL, tn), lambda j: (0, j)),
                pl.BlockSpec((1, tn), lambda j: (0, j))],
      out_specs=pl.BlockSpec((n, tn), lambda j: (0, j)),
      out_shape=jax.ShapeDtypeStruct((n, 6 * D_MODEL), F32),
      compiler_params=pltpu.CompilerParams(vmem_limit_bytes=VMEM_LIMIT),
      name="modulation",
  )(c_all, w_ada, b_ada.reshape(1, -1))


def _bias_kernel(tab_ref, near_ref, diag_ref, slast_ref, snew_ref):
  def bias_of(dist, h):
    val = jnp.full(dist.shape, tab_ref[0, h], F32)
    for b in range(1, N_BUCKETS):
      val = jnp.where(dist >= _LOWER[b], tab_ref[b, h], val)
    return (val - tab_ref[N_BUCKETS - 1, h]) * LOG2E

  t = ATT_T
  r = lax.broadcasted_iota(jnp.int32, (t, t), 0)
  c = lax.broadcasted_iota(jnp.int32, (t, t), 1)
  r8 = lax.broadcasted_iota(jnp.int32, (SUBLANES, LANES), 0)
  c8 = lax.broadcasted_iota(jnp.int32, (SUBLANES, LANES), 1)
  for h in range(ATT_HEADS):
    near_ref[h] = bias_of(t + r - c, h)
    d = r - c
    diag_ref[h] = jnp.where(d >= 0, bias_of(d, h), NEG)
    slast_ref[h] = bias_of(PAGE_SIZE + r8 - c8, h)
    dn = r8 - c8
    snew_ref[h] = jnp.where(dn >= 0, bias_of(dn, h), NEG)


def _bias_tiles(rel_bias):
  t = ATT_T
  return pl.pallas_call(
      _bias_kernel,
      in_specs=[pl.BlockSpec(memory_space=pltpu.SMEM)],
      out_shape=(jax.ShapeDtypeStruct((ATT_HEADS, t, t), F32),
                 jax.ShapeDtypeStruct((ATT_HEADS, t, t), F32),
                 jax.ShapeDtypeStruct((ATT_HEADS, SUBLANES, LANES), F32),
                 jax.ShapeDtypeStruct((ATT_HEADS, SUBLANES, LANES), F32)),
      compiler_params=pltpu.CompilerParams(vmem_limit_bytes=VMEM_LIMIT),
      name="bias_tiles",
  )(rel_bias)


def _causal_conv_silu(cur, hist, cw, cb):
  hp = SUBLANES
  head_row = lax.broadcasted_iota(jnp.int32, (hp, cur.shape[1]), 0)
  y = cb
  for i in range(CONV_W):
    back = CONV_W - 1 - i
    if back == 0:
      tap = cur
    else:
      rolled = pltpu.roll(cur, back, 0)
      head = jnp.where(head_row < back, pltpu.roll(hist, back, 0), rolled[0:hp])
      tap = head if cur.shape[0] == hp else jnp.concatenate([head, rolled[hp:]], axis=0)
    y = y + tap * cw[i:i + 1]
  return _silu(y)


def _inproj_kernel(x_ref, sc_ref, sh_ref, g_ref, wt_ref, wdt_ref, cw_ref, cb_ref, *out_refs,
                   prompt):
  x = x_ref[0]
  var = jnp.mean(x * x, axis=-1, keepdims=True)
  h = x * lax.rsqrt(var + EPS) * g_ref[...]
  h = h * (1.0 + sc_ref[0]) + sh_ref[0]
  hb = h.astype(BF16)
  aw = ATT_WIDTH
  proj = lambda lo, hi: _dot_nt(hb, wt_ref[lo:hi, :])
  xbc = proj(3 * aw + SSD_WIDTH, 3 * aw + SSD_WIDTH + CONV_DIM)
  if prompt:
    xbc_ref, conv_ref, hist_sc = out_refs[6], out_refs[8], out_refs[9]
    tm = xbc.shape[0]
    hist = jnp.where(pl.program_id(1) == 0, 0.0, hist_sc[...])
    for lo in range(0, tm, SSD_CHUNK):
      cur = xbc[lo:lo + SSD_CHUNK]
      xbc_ref[0, lo:lo + SSD_CHUNK, :] = _causal_conv_silu(cur, hist, cw_ref[...], cb_ref[...])
      hist = cur[SSD_CHUNK - SUBLANES:]
    hist_sc[...] = hist
    conv_ref[0] = xbc[tm - (CONV_W - 1):, :]
  else:
    out_refs[4][0] = xbc
  q = proj(0, aw) * (ATT_HEAD_DIM ** -0.5 * LOG2E)
  v = proj(2 * aw, 3 * aw)
  if prompt:
    q_ref, ktb_ref, vb_ref, kt_ref, v_ref, z_ref, xbc_ref, dt_ref, conv_ref, hist_sc = out_refs
    kt = _dot_nt(wt_ref[aw:2 * aw, :], hb)
    kt_ref[0] = kt
    t = ATT_T
    for hd in range(ATT_HEADS):
      sl = slice(hd * ATT_V_DIM, (hd + 1) * ATT_V_DIM)
      vb_ref[0, hd] = v[:, sl].astype(BF16)
      v_ref[0, pl.ds(hd, v.shape[0], stride=ATT_HEADS), :] = v[:, sl]
      q_ref[0, hd] = q[:, sl].astype(BF16)
      for cc in range(kt.shape[1] // t):
        ktb_ref[0, hd, cc] = kt[sl, cc * t:(cc + 1) * t].astype(BF16)
  else:
    q_ref, k_ref, v_ref, z_ref, xbc_ref, dt_ref = out_refs
    q_ref[0] = q
    k_ref[0] = proj(aw, 2 * aw)
    v_ref[0] = v
  o = 3 * aw
  z_ref[0] = proj(o, o + SSD_WIDTH)
  dt_ref[0] = _dot_nt(wdt_ref[...], hb) if prompt else _dot_nt(hb, wdt_ref[...])


def _mod_spec(mods, tm, col, b_of, i_of):
  d = mods.shape[2] // 6
  if mods.shape[1] != 1:
    return pl.BlockSpec((1, tm, d), lambda *g: (b_of(*g), i_of(*g), col))
  return pl.BlockSpec((1, 1, d), lambda *g: (b_of(*g), 0, col))


def _in_proj(x3, mods, g_mix, w_t, w_dt, conv_w, conv_b, tm, prompt):
  nb, tb, d = x3.shape
  nt = tb // tm
  mspec = lambda col: _mod_spec(mods, tm, col, lambda b, i: b, lambda b, i: i)
  hm = pl.BlockSpec((1, ATT_HEADS, tm, ATT_V_DIM), lambda b, i: (b, 0, i, 0))
  tok = lambda w: pl.BlockSpec((1, tm, w), lambda b, i: (b, i, 0))
  hm_shape = jax.ShapeDtypeStruct((nb, ATT_HEADS, tb, ATT_V_DIM), BF16)
  tok_shape = lambda w: jax.ShapeDtypeStruct((nb, tb, w), F32)
  if prompt:
    dt_spec = pl.BlockSpec((1, DT_ROWS, tm), lambda b, i: (b, 0, i))
    dt_shape = jax.ShapeDtypeStruct((nb, DT_ROWS, tb), F32)
  else:
    dt_spec, dt_shape = tok(LANES), tok_shape(LANES)
  tail_specs = [tok(SSD_WIDTH), tok(CONV_DIM), dt_spec]
  tail_shapes = [tok_shape(SSD_WIDTH), tok_shape(CONV_DIM), dt_shape]
  if prompt:
    t = ATT_T
    fm = pl.BlockSpec((1, ATT_HEADS, tm // t, ATT_V_DIM, t), lambda b, i: (b, 0, i, 0, 0))
    fm_shape = jax.ShapeDtypeStruct((nb, ATT_HEADS, tb // t, ATT_V_DIM, t), BF16)
    out_specs = [hm, fm, hm,
                 pl.BlockSpec((1, ATT_WIDTH, tm), lambda b, i: (b, 0, i)),
                 pl.BlockSpec((1, tm * ATT_HEADS, ATT_V_DIM), lambda b, i: (b, i, 0))] + tail_specs
    out_shape = [hm_shape, fm_shape, hm_shape,
                 jax.ShapeDtypeStruct((nb, ATT_WIDTH, tb), F32),
                 jax.ShapeDtypeStruct((nb, tb * ATT_HEADS, ATT_V_DIM), F32)] + tail_shapes
    out_specs.append(pl.BlockSpec((1, CONV_W - 1, CONV_DIM), lambda b, i: (b, 0, 0)))
    out_shape.append(jax.ShapeDtypeStruct((nb, CONV_W - 1, CONV_DIM), F32))
    scratch = [pltpu.VMEM((SUBLANES, CONV_DIM), F32)]
  else:
    out_specs = [tok(ATT_WIDTH)] * 3 + tail_specs
    out_shape = [tok_shape(ATT_WIDTH)] * 3 + tail_shapes
    scratch = []
  return pl.pallas_call(
      functools.partial(_inproj_kernel, prompt=prompt),
      grid=(nb, nt),
      in_specs=[tok(d), mspec(MOD_SC1), mspec(MOD_SH1),
                pl.BlockSpec((1, d), lambda b, i: (0, 0)),
                pl.BlockSpec(w_t.shape, lambda b, i: (0, 0)),
                pl.BlockSpec(w_dt.shape, lambda b, i: (0, 0)),
                pl.BlockSpec(conv_w.shape, lambda b, i: (0, 0)),
                pl.BlockSpec(conv_b.shape, lambda b, i: (0, 0))],
      out_specs=out_specs,
      out_shape=out_shape,
      scratch_shapes=scratch,
      compiler_params=pltpu.CompilerParams(
          dimension_semantics=("arbitrary", "arbitrary"), vmem_limit_bytes=VMEM_LIMIT),
      name="in_proj",
  )(x3, mods, mods, g_mix.reshape(1, d), w_t, w_dt, conv_w, conv_b)


def _diff_lambda(lam_ref):
  lv = lam_ref[...]
  s1 = jnp.sum(lv[0:1] * lv[1:2], axis=-1, keepdims=True)
  s2 = jnp.sum(lv[2:3] * lv[3:4], axis=-1, keepdims=True)
  return jnp.exp(s1) - jnp.exp(s2) + LAM_INIT


def _diff_combine(o0, o1, lam, g):
  d = o0 - lam * o1
  d = d * lax.rsqrt(jnp.mean(d * d, axis=-1, keepdims=True) + EPS) * g
  return d * (1.0 - LAM_INIT)


def _attn_kernel(q_ref, k_ref, v_ref, near_ref, diag_ref, lam_ref, g_ref, *refs):
  weights_f32, weights_bf16 = refs[0:3], refs[4:7]
  o_ref, (m_sc, acc_sc) = refs[3], refs[7:]
  for src, dst in zip(weights_f32, weights_bf16):
    dst[...] = src[...].astype(dst.dtype)
  t = ATT_T
  qi = pl.program_id(1)
  lane = lax.broadcasted_iota(jnp.int32, (t, ATT_V_DIM), 1)
  m_sc[...] = jnp.full(m_sc.shape, NEG, F32)
  acc_sc[...] = jnp.zeros(acc_sc.shape, F32)

  def scores(h, c, bias_refs):
    q = q_ref[0, h]
    zero = jnp.zeros_like(q)
    q2 = jnp.concatenate([jnp.where(lane < ATT_HEAD_DIM, q, zero),
                          jnp.where(lane >= ATT_HEAD_DIM, q, zero)], axis=0)
    kc = jnp.concatenate([k_ref[0, h, c + j] for j in range(len(bias_refs))], axis=1)
    s = _dot(q2, kc)
    if any(b is not None for b in bias_refs):
      bias = jnp.concatenate([jnp.zeros((t, t), F32) if b is None else b[h] for b in bias_refs],
                             axis=1)
      s = s + jnp.concatenate([bias, bias], axis=0)
    return s

  def softmax_probs(h, s):
    m_prev = m_sc[h]
    m_new = jnp.maximum(m_prev, jnp.max(s, axis=-1, keepdims=True))
    alpha = jnp.exp2(m_prev - m_new)
    p = jnp.exp2(s - jnp.concatenate([m_new] * (s.shape[1] // LANES), axis=1))
    m_sc[h] = m_new
    return alpha, p.astype(BF16)

  def step(c, bias_refs):
    keys = len(bias_refs) * t
    start = pl.multiple_of(c * t, t)
    ones = jnp.ones((keys, LANES), BF16)
    for h in range(ATT_HEADS):
      alpha, pb = softmax_probs(h, scores(h, c, bias_refs))
      vaug = jnp.concatenate([v_ref[0, h, pl.ds(start, keys), :], ones], axis=1)
      acc_sc[h] = acc_sc[h] * jnp.concatenate([alpha, alpha], axis=1) + _dot(pb, vaug)

  n_far = jnp.maximum(qi - 1, 0)

  def far_pair(k, carry):
    step(2 * k, (None, None))
    return carry

  lax.fori_loop(0, n_far // 2, far_pair, 0)

  @pl.when(n_far % 2 == 1)
  def _():
    step(n_far - 1, (None,))

  @pl.when(qi >= 1)
  def _():
    step(qi - 1, (near_ref, diag_ref))

  @pl.when(qi == 0)
  def _():
    step(qi, (diag_ref,))

  lam = _diff_lambda(lam_ref)
  for h in range(ATT_HEADS):
    acc = acc_sc[h]
    o = acc[:, :ATT_V_DIM] / acc[:, ATT_V_DIM:]
    d = _diff_combine(o[:t], o[t:], lam, g_ref[...])
    o_ref[0, :, h * ATT_V_DIM:(h + 1) * ATT_V_DIM] = d.astype(o_ref.dtype)


def _prompt_attention(q_hm, kt_hm, v_hm, near, diag, lam4, subln_g, weights):
  b, nh, s, e = q_hm.shape
  t = ATT_T
  nq = s // t
  n_steps = b * nq
  slab = lambda w: pl.BlockSpec((w.shape[0] // n_steps, w.shape[1]),
                                lambda bi, qi: (bi * nq + qi, 0))
  assert all(w.shape[0] % (n_steps * 2 * SUBLANES) == 0 for w in weights)
  return pl.pallas_call(
      _attn_kernel,
      grid=(b, nq),
      in_specs=[pl.BlockSpec((1, nh, t, e), lambda bi, qi: (bi, 0, qi, 0)),
                pl.BlockSpec((1, nh, nq, e, t), lambda bi, qi: (bi, 0, 0, 0, 0)),
                pl.BlockSpec((1, nh, s, e), lambda bi, qi: (bi, 0, 0, 0)),
                pl.BlockSpec((nh, t, t), lambda bi, qi: (0, 0, 0)),
                pl.BlockSpec((nh, t, t), lambda bi, qi: (0, 0, 0)),
                pl.BlockSpec((4, ATT_HEAD_DIM), lambda bi, qi: (0, 0)),
                pl.BlockSpec((1, e), lambda bi, qi: (0, 0))] + [slab(w) for w in weights],
      out_specs=[pl.BlockSpec((1, t, nh * e), lambda bi, qi: (bi, qi, 0))]
      + [slab(w) for w in weights],
      out_shape=[jax.ShapeDtypeStruct((b, s, nh * e), BF16)]
      + [jax.ShapeDtypeStruct(w.shape, BF16) for w in weights],
      scratch_shapes=[pltpu.VMEM((nh, 2 * t, LANES), F32),
                      pltpu.VMEM((nh, 2 * t, 2 * LANES), F32)],
      compiler_params=pltpu.CompilerParams(
          dimension_semantics=("arbitrary", "arbitrary"), vmem_limit_bytes=VMEM_LIMIT),
      name="prompt_attention",
  )(q_hm, kt_hm, v_hm, near, diag, lam4, subln_g, *weights)


def _ssd_kernel(xbc_ref, z_ref, dt_ref, pre_ref, h0_ref, *refs, tv, nc, alternate, activated):
  consts = refs[:8]
  if activated:
    (ssd_ref, h_ref), conv_ref, xp_sc = refs[8:], None, None
  else:
    ssd_ref, conv_ref, h_ref, xp_sc = refs[8:]
  hp = SUBLANES
  b = pl.program_id(0)
  c = pl.program_id(1)
  n_seq = xbc_ref.shape[0]

  if not activated:
    @pl.when((b == 0) & (c == 0))
    def _():
      xp_sc[...] = jnp.zeros(xp_sc.shape, F32)

  @pl.when(c == 0)
  def _():
    for bb in range(n_seq):
      if not activated:
        xp_sc[bb, hp - (CONV_W - 1):hp, :] = pre_ref[bb]
      h_ref[bb] = h0_ref[bb]

  stages = [_ssd_chunk(xbc_ref.at[bb], z_ref.at[bb], dt_ref.at[bb], *consts,
                       ssd_ref.at[bb], h_ref.at[bb], None if activated else xp_sc.at[bb], tv=tv)
            for bb in range(n_seq)]
  for _ in (itertools.zip_longest(*stages) if alternate else itertools.chain(*stages)):
    pass

  if not activated:
    @pl.when(c == nc - 1)
    def _():
      for bb in range(n_seq):
        conv_ref[bb] = xp_sc[bb, hp + tv - (CONV_W - 1):hp + tv, :]


def _ssd_chunk(xbc_ref, z_ref, dt_ref, cw_ref, cb_ref, dtb_ref, alog_ref, dsk_ref, ng_ref, e_ref,
               tri_ref, ssd_ref, h_ref, xp_sc, *, tv):
  L = SSD_CHUNK
  hp = SUBLANES

  def pad_rows(x):
    if tv == L:
      return x
    return jnp.concatenate([x, jnp.zeros((L - tv,) + x.shape[1:], x.dtype)], axis=0)

  if xp_sc is None:
    xa = xbc_ref[...]
  else:
    xp_sc[hp:hp + tv, :] = xbc_ref[...]
    xa = _causal_conv_silu(xp_sc[hp:hp + tv, :], xp_sc[0:hp, :], cw_ref[...], cb_ref[...])
    xp_sc[0:hp, :] = xp_sc[tv:tv + hp, :]
  yield

  xs = xa[:, :SSD_WIDTH]
  gw = SSD_GROUPS * D_STATE
  bmat = pad_rows(xa[:, SSD_WIDTH:SSD_WIDTH + gw]).astype(BF16)
  cmat = pad_rows(xa[:, SSD_WIDTH + gw:]).astype(BF16)

  dt_raw = dt_ref[...]
  if dt_raw.shape[0] != tv:
    dt_raw = jnp.concatenate([dt_raw, jnp.zeros((L - dt_raw.shape[0], L), F32)], axis=0).T
  x = dt_raw + dtb_ref[...]
  dtv = pad_rows(jnp.maximum(x, 0.0) + jnp.log1p(jnp.exp(-jnp.abs(x))))
  a = dtv * (-jnp.exp(alog_ref[...]))

  tri = tri_ref[...]
  acs = sum(_dot(tri, part) for part in _split3(a))
  yield
  e = e_ref[...]
  aexp = sum(_dot(part, e) for part in _split3(acs))
  dtexp = _dot(dtv.astype(BF16), e)
  xd = xs * dtexp[:tv]
  a_last = acs[L - 1:L, :]
  xdd_t = pad_rows(xd * jnp.exp(aexp[L - 1:L, :] - aexp[:tv])).T
  xd = pad_rows(xd)
  acs_t = acs.T
  yield

  ri = lax.broadcasted_iota(jnp.int32, (L, L), 0)
  ci = lax.broadcasted_iota(jnp.int32, (L, L), 1)
  causal = ri >= ci
  lane = lax.broadcasted_iota(jnp.int32, (L, LANES), 1)
  heads_per_group = SSD_HEADS // SSD_GROUPS
  hprev = h_ref[...]
  hb = hprev.astype(BF16)

  ydiag, yoff, upd = [], [], []
  for g in range(SSD_GROUPS):
    cg = cmat[:, g * D_STATE:(g + 1) * D_STATE]
    bg = bmat[:, g * D_STATE:(g + 1) * D_STATE]
    scores = _dot_nt(cg, bg)
    for pair in range(heads_per_group // 2):
      h0 = g * heads_per_group + 2 * pair
      ms = []
      for h in (h0, h0 + 1):
        col = jnp.broadcast_to(acs[:, h:h + 1], (L, L))
        row = acs_t[h:h + 1, :]
        lm = jnp.exp(jnp.where(causal, col - row, NEG))
        ms.append((scores * lm).astype(BF16))
      xpair = xd[:, h0 * SSD_HEAD_DIM:(h0 + 2) * SSD_HEAD_DIM]
      xbd = jnp.concatenate([jnp.where(lane < SSD_HEAD_DIM, xpair, 0.0),
                             jnp.where(lane >= SSD_HEAD_DIM, xpair, 0.0)], axis=0)
      ydiag.append(_dot(jnp.concatenate(ms, axis=1), xbd.astype(BF16)))
      yield
    rows = slice(g * heads_per_group * SSD_HEAD_DIM, (g + 1) * heads_per_group * SSD_HEAD_DIM)
    yoff.append(_dot_nt(cg, hb[rows, :]))
    upd.append(_dot(xdd_t[rows, :].astype(BF16), bg))
    yield

  y = (jnp.concatenate(ydiag, axis=1)[:tv]
       + jnp.concatenate(yoff, axis=1)[:tv] * jnp.exp(aexp[:tv]) + dsk_ref[...] * xs)
  yield

  decayed = []
  chunk_decay = jnp.exp(a_last)
  for h in range(SSD_HEADS):
    dec = jnp.broadcast_to(chunk_decay[:, h:h + 1], (SSD_HEAD_DIM, D_STATE))
    decayed.append(hprev[h * SSD_HEAD_DIM:(h + 1) * SSD_HEAD_DIM, :] * dec)
  h_ref[...] = jnp.concatenate(decayed, axis=0) + jnp.concatenate(upd, axis=0)
  yield

  yv = y * _silu(z_ref[...])
  sq = yv * yv
  half = SSD_WIDTH // SSD_GROUPS
  r0 = lax.rsqrt(jnp.mean(sq[:, :half], axis=-1, keepdims=True) + EPS)
  r1 = lax.rsqrt(jnp.mean(sq[:, half:], axis=-1, keepdims=True) + EPS)
  out = jnp.concatenate([yv[:, :half] * r0, yv[:, half:] * r1], axis=1) * ng_ref[...]
  ssd_ref[...] = out.astype(ssd_ref.dtype)


def _ssd(xbc, z, dt, prefix, h0, conv_w, conv_b, dtb_p, alog_p, dsk_e, norm_g, tv, seqs_per_step,
         activated=False):
  nb, tb, _ = xbc.shape
  nc = tb // tv
  ns = seqs_per_step
  assert nb % ns == 0
  e_np = np.zeros((LANES, SSD_WIDTH), np.float32)
  for h in range(SSD_HEADS):
    e_np[h, h * SSD_HEAD_DIM:(h + 1) * SSD_HEAD_DIM] = 1.0
  tri_np = np.tril(np.ones((SSD_CHUNK, SSD_CHUNK), np.float32))
  full = lambda shape: pl.BlockSpec(shape, lambda b, c: (0,) * len(shape))
  tok = lambda w: pl.BlockSpec((ns, tv, w), lambda b, c: (b, c, 0))
  per_b = lambda r, w: pl.BlockSpec((ns, r, w), lambda b, c: (b, 0, 0))
  if dt.shape[1] == tb:
    dt_spec = tok(LANES)
  else:
    assert tv == SSD_CHUNK and dt.shape == (nb, DT_ROWS, tb)
    dt_spec = pl.BlockSpec((ns, DT_ROWS, tv), lambda b, c: (b, 0, c))
  out_specs = [tok(SSD_WIDTH), per_b(CONV_W - 1, CONV_DIM), per_b(SSD_WIDTH, D_STATE)]
  out_shape = [jax.ShapeDtypeStruct((nb, tb, SSD_WIDTH), BF16),
               jax.ShapeDtypeStruct((nb, CONV_W - 1, CONV_DIM), F32),
               jax.ShapeDtypeStruct((nb, SSD_WIDTH, D_STATE), F32)]
  scratch = [pltpu.VMEM((ns, SSD_CHUNK + SUBLANES, CONV_DIM), F32)]
  if activated:
    del out_specs[1], out_shape[1]
    scratch = []
  return pl.pallas_call(
      functools.partial(_ssd_kernel, tv=tv, nc=nc, alternate=tv < SSD_CHUNK, activated=activated),
      grid=(nb // ns, nc),
      in_specs=[tok(CONV_DIM), tok(SSD_WIDTH), dt_spec,
                per_b(CONV_W - 1, CONV_DIM), per_b(SSD_WIDTH, D_STATE),
                full((CONV_W, CONV_DIM)), full((1, CONV_DIM)), full((1, LANES)), full((1, LANES)),
                full((1, SSD_WIDTH)), full((1, SSD_WIDTH)),
                full((LANES, SSD_WIDTH)), full((SSD_CHUNK, SSD_CHUNK))],
      out_specs=out_specs,
      out_shape=out_shape,
      scratch_shapes=scratch,
      compiler_params=pltpu.CompilerParams(
          dimension_semantics=("arbitrary", "arbitrary"), vmem_limit_bytes=VMEM_LIMIT),
      name="ssd_scan",
  )(xbc, z, dt, prefix, h0, conv_w, conv_b, dtb_p, alog_p, dsk_e, norm_g,
    jnp.asarray(e_np, BF16), jnp.asarray(tri_np, BF16))


def _ffn_pre(x_ref, att_ref, ssd_ref, ga1_ref, sc2_ref, sh2_ref, gf_ref, wo_ref):
  mix = _dot(att_ref[0], wo_ref[0:ATT_WIDTH, :]) + _dot(ssd_ref[0], wo_ref[ATT_WIDTH:, :])
  x1 = x_ref[0] + ga1_ref[0] * mix
  var = jnp.mean(x1 * x1, axis=-1, keepdims=True)
  h2 = x1 * lax.rsqrt(var + EPS) * gf_ref[...]
  return x1, (h2 * (1.0 + sc2_ref[0]) + sh2_ref[0]).astype(BF16)


def _ffn_piece(h2, wu_ref, wd_ref, lo, hi):
  u = jnp.maximum(_dot(h2, wu_ref[:, lo:hi]), 0.0)
  return _dot((u * u).astype(BF16), wd_ref[lo:hi, :])


def _ffn_post(x1, acc, ga2_ref, gl_ref):
  x2 = x1 + ga2_ref[0] * acc
  var2 = jnp.mean(x2 * x2, axis=-1, keepdims=True)
  return x2 * lax.rsqrt(var2 + EPS) * gl_ref[...]


def _ffn_kernel(x_ref, att_ref, ssd_ref, ga1_ref, sc2_ref, sh2_ref, ga2_ref, gf_ref, gl_ref,
                wo_ref, wu_ref, wd_ref, y_ref, x1_sc, h2_sc, acc_sc):
  f = pl.program_id(2)

  @pl.when(f == 0)
  def _():
    x1, h2 = _ffn_pre(x_ref, att_ref, ssd_ref, ga1_ref, sc2_ref, sh2_ref, gf_ref, wo_ref)
    x1_sc[...] = x1
    h2_sc[...] = h2
    acc_sc[...] = jnp.zeros(acc_sc.shape, F32)

  acc_sc[...] += _ffn_piece(h2_sc[...], wu_ref, wd_ref, 0, wu_ref.shape[1])

  @pl.when(f == pl.num_programs(2) - 1)
  def _():
    y_ref[0] = _ffn_post(x1_sc[...], acc_sc[...], ga2_ref, gl_ref)


def _out_ffn(x3, att, ssd, mods, g_ffn, g_final, w_out_b, w_up_b, w_down_b, tm):
  nb, tb, d = x3.shape
  nt = tb // tm
  fc = D_MODEL
  mspec = lambda col: _mod_spec(mods, tm, col, lambda b, i, f: b, lambda b, i, f: i)
  tok = lambda w: pl.BlockSpec((1, tm, w), lambda b, i, f: (b, i, 0))
  const = lambda shape: pl.BlockSpec(shape, lambda b, i, f: (0, 0))
  return pl.pallas_call(
      _ffn_kernel,
      grid=(nb, nt, D_FF // fc),
      in_specs=[tok(d), tok(ATT_WIDTH), tok(SSD_WIDTH),
                mspec(MOD_GA1), mspec(MOD_SC2), mspec(MOD_SH2), mspec(MOD_GA2),
                const((1, d)), const((1, d)), const((d, d)),
                pl.BlockSpec((d, fc), lambda b, i, f: (0, f)),
                pl.BlockSpec((fc, d), lambda b, i, f: (f, 0))],
      out_specs=tok(d),
      out_shape=jax.ShapeDtypeStruct((nb, tb, d), F32),
      scratch_shapes=[pltpu.VMEM((tm, d), F32), pltpu.VMEM((tm, d), BF16),
                      pltpu.VMEM((tm, d), F32)],
      compiler_params=pltpu.CompilerParams(
          dimension_semantics=("arbitrary", "arbitrary", "arbitrary"),
          vmem_limit_bytes=VMEM_LIMIT),
      name="out_ffn",
  )(x3, att, ssd, mods, mods, mods, mods, g_ffn.reshape(1, d), g_final.reshape(1, d),
    w_out_b, w_up_b, w_down_b)


def _ffn_sample_kernel(pt_ref, x_ref, att_ref, ssd_ref, ga1_ref, sc2_ref, sh2_ref, ga2_ref,
                       gf_ref, gl_ref, wo_ref, wu_ref, wd_ref,
                       qn_ref, kn_ref, vn_ref, blast_ref, bnew_ref, lam_ref, g_ref, ck_hbm, cv_hbm,
                       y_ref, o_ref,
                       kbuf, vbuf, sem, m_sc, l_sc, acc_sc, kpad, vpad, facc_sc,
                       *, n_steps, steps_per_seq):
  i = pl.program_id(0)
  seq = i // steps_per_seq
  part = i % steps_per_seq
  rps = ROUNDS_PER_STEP
  ahead = ROUND_SLOTS - 1
  last_part = part == steps_per_seq - 1

  def round_copies(r):
    if r < rps:
      sq, rnd = seq, part * rps + r
    else:
      nxt = i + 1
      sq, rnd = nxt // steps_per_seq, (nxt % steps_per_seq) * rps + r - rps
    sl = r % ROUND_SLOTS
    copies = []
    for p in range(PAGES_PER_GROUP):
      pg = pt_ref[sq, rnd * PAGES_PER_GROUP + p]
      copies.append((pltpu.make_async_copy(ck_hbm.at[pg], kbuf.at[sl, p], sem.at[0, sl]), 0))
      copies.append((pltpu.make_async_copy(cv_hbm.at[pg], vbuf.at[sl, p], sem.at[1, sl]), 1))
    return copies

  def start_round(r):
    for cp, prio in round_copies(r):
      cp.start(priority=prio)

  @pl.when(i == 0)
  def _():
    kpad[...] = jnp.zeros(kpad.shape, kpad.dtype)
    vpad[...] = jnp.zeros(vpad.shape, vpad.dtype)
    for r in range(ahead):
      start_round(r)

  @pl.when(part == 0)
  def _():
    m_sc[...] = jnp.full(m_sc.shape, NEG, F32)
    l_sc[...] = jnp.zeros(l_sc.shape, F32)
    acc_sc[...] = jnp.zeros(acc_sc.shape, F32)

  qn = qn_ref[0].astype(BF16)
  half_of_lane = lax.broadcasted_iota(jnp.int32, qn.shape, 1) // ATT_HEAD_DIM
  qb = jnp.concatenate([jnp.where(half_of_lane == hj, qn, jnp.zeros_like(qn))
                        for hj in range(2 * ATT_HEADS)], axis=0)
  rows_per_head = 2 * SUBLANES

  def softmax_probs(s):
    m_prev = m_sc[...]
    m_new = jnp.maximum(m_prev, jnp.max(s, axis=-1, keepdims=True))
    alpha = jnp.exp2(m_prev - m_new)
    p = jnp.exp2(s - m_new[:, 0:1])
    l_sc[...] = l_sc[...] * alpha + jnp.sum(p, axis=-1, keepdims=True)
    m_sc[...] = m_new
    return alpha, p.astype(BF16)

  def accumulate(alpha, pb, values_of_head):
    pv = [_dot(pb[h * rows_per_head:(h + 1) * rows_per_head, :], values_of_head(h))
          for h in range(ATT_HEADS)]
    acc_sc[...] = acc_sc[...] * alpha + jnp.concatenate(pv, axis=0)

  def round_scores(r):
    sl = r % ROUND_SLOTS
    kt = jnp.concatenate([kbuf[sl, p].astype(BF16) for p in range(PAGES_PER_GROUP)], axis=1)
    hr, hf = qb.shape[0] // 2, qb.shape[1] // 2
    s = jnp.concatenate([_dot(qb[:hr, :hf], kt[:hf]), _dot(qb[hr:, hf:], kt[hf:])], axis=0)
    if r == rps - 1:
      tail = s[:, -PAGE_SIZE:] + jnp.where(last_part, blast_ref[...], 0.0)
      s = jnp.concatenate([s[:, :-PAGE_SIZE], tail], axis=1)
    return s

  def round_values(r):
    sl = r % ROUND_SLOTS

    def values_of_head(h):
      v = [vbuf[sl, p, pl.ds(h, PAGE_SIZE, stride=ATT_HEADS), :] for p in range(PAGES_PER_GROUP)]
      return jnp.concatenate(v, axis=0).astype(BF16)

    return values_of_head

  x1, h2 = _ffn_pre(x_ref, att_ref, ssd_ref, ga1_ref, sc2_ref, sh2_ref, gf_ref, wo_ref)
  fc = D_FF // rps
  for r in range(rps):
    for cp, _ in round_copies(r):
      cp.wait()
    if r + ahead < rps:
      start_round(r + ahead)
    else:
      pl.when(i + 1 < n_steps)(functools.partial(start_round, r + ahead))
    s = round_scores(r)
    u = jnp.maximum(_dot(h2, wu_ref[:, r * fc:(r + 1) * fc]), 0.0)
    ub = (u * u).astype(BF16)
    alpha, pb = softmax_probs(s)
    accumulate(alpha, pb, round_values(r))
    piece = _dot(ub, wd_ref[r * fc:(r + 1) * fc, :])
    facc_sc[...] = piece if r == 0 else facc_sc[...] + piece
  y_ref[0] = _ffn_post(x1, facc_sc[...], ga2_ref, gl_ref)

  @pl.when(last_part)
  def _():
    t_new = kn_ref.shape[1]
    kpad[0:t_new, :] = kn_ref[0].astype(BF16)
    vpad[0:t_new, :] = vn_ref[0].astype(BF16)
    sn = _dot_nt(qb, kpad[...]) + bnew_ref[...]
    accumulate(*softmax_probs(sn), lambda h: vpad[:, h * ATT_V_DIM:(h + 1) * ATT_V_DIM])
    o = acc_sc[...] / l_sc[...]
    lam = _diff_lambda(lam_ref)
    for h in range(ATT_HEADS):
      o0 = o[(2 * h) * SUBLANES:(2 * h + 1) * SUBLANES, :]
      o1 = o[(2 * h + 1) * SUBLANES:(2 * h + 2) * SUBLANES, :]
      o_ref[0, :, h * ATT_V_DIM:(h + 1) * ATT_V_DIM] = _diff_combine(
          o0, o1, lam, g_ref[...]).astype(o_ref.dtype)


def _ffn_with_sample_attention(x_p, att_p, ssd_p, mods, g_ffn, g_final,
                               w_out_b, w_up_b, w_down_b,
                               page_table, q_new, k_new, v_new, blast, bnew, lam4, subln_g,
                               cache_k, cache_v):
  nb, tb, d = x_p.shape
  tm = FFN_TM
  tiles_per_b = tb // tm
  n_steps = nb * tiles_per_b
  n_seq, n_pages = page_table.shape
  t_new = k_new.shape[1]
  pages_per_step = PAGES_PER_GROUP * ROUNDS_PER_STEP
  steps_per_seq = n_pages // pages_per_step
  ring = ROUND_SLOTS
  q_rows = 2 * ATT_HEADS * t_new
  assert t_new == SUBLANES and n_pages % pages_per_step == 0
  assert n_seq * steps_per_seq == n_steps, "one batch of page rounds per FFN tile"
  assert ROUNDS_PER_STEP % ROUND_SLOTS == 0, "ring slots must be static per round"
  w = ATT_WIDTH
  assert cache_k.shape[1:] == (w, PAGE_SIZE) and cache_v.shape[1:] == (PAGE_SIZE * ATT_HEADS, ATT_V_DIM)
  tok = lambda width: pl.BlockSpec((1, tm, width),
                                   lambda i, pt: (i // tiles_per_b, i % tiles_per_b, 0))
  mspec = lambda col: _mod_spec(mods, tm, col, lambda i, pt: i // tiles_per_b,
                                lambda i, pt: i % tiles_per_b)
  const = lambda shape: pl.BlockSpec(shape, lambda i, pt: (0,) * len(shape),
                                     pipeline_mode=pl.Buffered(1))
  per_seq = lambda r: pl.BlockSpec((1, r, w), lambda i, pt: (i // steps_per_seq, 0, 0))
  grid_spec = pltpu.PrefetchScalarGridSpec(
      num_scalar_prefetch=1,
      grid=(n_steps,),
      in_specs=[tok(d), tok(ATT_WIDTH), tok(SSD_WIDTH),
                mspec(MOD_GA1), mspec(MOD_SC2), mspec(MOD_SH2), mspec(MOD_GA2),
                const((1, d)), const((1, d)),
                const((d, d)), const((d, D_FF)), const((D_FF, d)),
                per_seq(t_new), per_seq(t_new), per_seq(t_new),
                const((q_rows, LANES)), const((q_rows, LANES)),
                const((4, ATT_HEAD_DIM)), const((1, ATT_V_DIM)),
                pl.BlockSpec(memory_space=pl.ANY),
                pl.BlockSpec(memory_space=pl.ANY)],
      out_specs=[tok(d), per_seq(t_new)],
      scratch_shapes=[pltpu.VMEM((ring, PAGES_PER_GROUP) + cache_k.shape[1:], F32),
                      pltpu.VMEM((ring, PAGES_PER_GROUP) + cache_v.shape[1:], F32),
                      pltpu.SemaphoreType.DMA((2, ring)),
                      pltpu.VMEM((q_rows, LANES), F32),
                      pltpu.VMEM((q_rows, LANES), F32),
                      pltpu.VMEM((q_rows, ATT_V_DIM), F32),
                      pltpu.VMEM((PAGE_SIZE, w), BF16),
                      pltpu.VMEM((PAGE_SIZE, w), BF16),
                      pltpu.VMEM((tm, d), F32)])
  return pl.pallas_call(
      functools.partial(_ffn_sample_kernel, n_steps=n_steps, steps_per_seq=steps_per_seq),
      grid_spec=grid_spec,
      out_shape=[jax.ShapeDtypeStruct((nb, tb, d), F32),
                 jax.ShapeDtypeStruct((n_seq, t_new, w), BF16)],
      compiler_params=pltpu.CompilerParams(
          dimension_semantics=("arbitrary",), vmem_limit_bytes=VMEM_LIMIT),
      name="ffn_sample_attention",
  )(page_table, x_p, att_p, ssd_p, mods, mods, mods, mods,
    g_ffn.reshape(1, d), g_final.reshape(1, d), w_out_b, w_up_b, w_down_b,
    q_new, k_new, v_new, blast, bnew, lam4, subln_g, cache_k, cache_v)


def kernel(x_prompt, x_sample, cache_k, cache_v, state_conv, state_ssm, page_table, c_prompt,
           c_sample, rel_bias, w_ada, b_ada, g_mix, g_ffn, w_in, w_out, lam_q1, lam_k1, lam_q2,
           lam_k2, subln_g, conv_w, conv_b, dt_bias, a_log, d_skip, ssd_norm_g, w_up, w_down,
           g_final):
  assert w_ada.shape[0] == 1, "single-layer step"
  bp, sp, d = x_prompt.shape
  bs, ts, _ = x_sample.shape
  n_pool = cache_k.shape[1]

  w_in_t = w_in[0].T.astype(BF16)
  w_t = w_in_t[:IN_MAIN]
  w_dt_p = jnp.pad(w_in_t[IN_MAIN:], ((0, DT_ROWS - SSD_HEADS), (0, 0)))
  w_dt_s = jnp.pad(w_in_t[IN_MAIN:], ((0, LANES - SSD_HEADS), (0, 0)))
  lam4 =jnp.concatenate([lam_q1, lam_k1, lam_q2, lam_k2], axis=0)
  subln = subln_g.reshape(1, ATT_V_DIM)
  dtb_p = jnp.pad(dt_bias, ((0, 0), (0, LANES - SSD_HEADS)))
  alog_p = jnp.pad(a_log, ((0, 0), (0, LANES - SSD_HEADS)))
  dsk_e = jnp.repeat(d_skip[0], SSD_HEAD_DIM).reshape(1, SSD_WIDTH)
  ssd_args = (conv_w[0], conv_b, dtb_p, alog_p, dsk_e, ssd_norm_g)

  mods = _modulation(jnp.concatenate([c_prompt, c_sample], axis=0), w_ada[0], b_ada[0])
  mods_p = mods[:bp].reshape(bp, 1, 6 * d)
  mods_s = jnp.repeat(mods[bp:], ts, axis=0).reshape(1, bs * ts, 6 * d)
  near, diag, slast, snew = _bias_tiles(rel_bias)

  q_hm, kt_hm, v_hm, kt_p, v_p, z_p, xbc_p, dt_p, conv_p = _in_proj(
      x_prompt, mods_p, g_mix[0], w_t, w_dt_p, conv_w[0], conv_b, tm=PROJ_TM, prompt=True)
  att_p, w_out_b, w_up_b, w_down_b = _prompt_attention(
      q_hm, kt_hm, v_hm, near, diag, lam4, subln, (w_out[0], w_up[0], w_down[0]))
  ssd_p, h_p = _ssd(
      xbc_p, z_p, dt_p, jnp.zeros((bp, CONV_W - 1, CONV_DIM), F32),
      jnp.zeros((bp, SSD_WIDTH, D_STATE), F32), *ssd_args, tv=SSD_CHUNK,
      seqs_per_step=SSD_SEQS_PER_STEP, activated=True)

  n_tok = bs * ts
  xs3 = x_sample.reshape(1, n_tok, d)
  q_s, k_s, v_s, z_s, xbc_s, dt_s = _in_proj(
      xs3, mods_s, g_mix[0], w_t, w_dt_s, conv_w[0], conv_b, tm=n_tok, prompt=False)
  blast = jnp.repeat(slast, 2, axis=0).reshape(2 * ATT_HEADS * SUBLANES, LANES)
  bnew = jnp.repeat(snew, 2, axis=0).reshape(2 * ATT_HEADS * SUBLANES, LANES)
  per_seq = lambda a: a.reshape(bs, ts, ATT_WIDTH)
  y_p, att_s = _ffn_with_sample_attention(
      x_prompt, att_p, ssd_p, mods_p, g_ffn[0], g_final, w_out_b, w_up_b, w_down_b,
      page_table, per_seq(q_s), per_seq(k_s), per_seq(v_s), blast, bnew, lam4, subln,
      jnp.transpose(cache_k[0], (0, 2, 3, 4, 1)).reshape(n_pool, ATT_WIDTH, PAGE_SIZE),
      cache_v[0].reshape(n_pool, PAGE_SIZE * ATT_HEADS, ATT_V_DIM))
  ssd_s, conv_s, h_s = _ssd(
      xbc_s.reshape(bs, ts, CONV_DIM), z_s.reshape(bs, ts, SSD_WIDTH), dt_s.reshape(bs, ts, LANES),
      state_conv[0], state_ssm[0].reshape(bs, SSD_WIDTH, D_STATE), *ssd_args, tv=ts,
      seqs_per_step=SSD_SHORT_SEQS_PER_STEP)
  y_s = _out_ffn(xs3, att_s.reshape(1, n_tok, ATT_WIDTH), ssd_s.reshape(1, n_tok, SSD_WIDTH),
                 mods_s, g_ffn[0], g_final, w_out_b, w_up_b, w_down_b, tm=n_tok)

  hshape = (SSD_HEADS, SSD_HEAD_DIM, D_STATE)
  return (y_p, y_s.reshape(bs, ts, d),
          kt_p.reshape(1, bp, ATT_HEADS, 2, ATT_HEAD_DIM, sp).transpose(0, 1, 5, 2, 3, 4),
          v_p.reshape(1, bp, sp, ATT_HEADS, ATT_V_DIM),
          conv_p[None], h_p.reshape(1, bp, *hshape),
          k_s.reshape(1, bs, ts, ATT_HEADS, 2, ATT_HEAD_DIM),
          v_s.reshape(1, bs, ts, ATT_HEADS, ATT_V_DIM),
          conv_s[None], h_s.reshape(1, bs, *hshape))
```
